```python
import jax, jax.numpy as jnp
from jax import lax
import numpy as np

D_MODEL = 2048
BATCH = 2
SEQ = 4096
DEPTH = 1

CHUNK = 64
D_MIX = D_MODEL
D_MLSTM = D_MIX // 2
MLSTM_HEADS = 4
MLSTM_HEAD_DIM = D_MLSTM // MLSTM_HEADS
D_LRU = D_MIX - D_MLSTM
LRU_BLOCKS = 8
LRU_BLOCK_DIM = D_LRU // LRU_BLOCKS
LRU_CONV = 4
LRU_C = 8.0
D_FF = 5632
FFN_CONV = 3
EPS = 1e-6
PROJ_COLS = 4 * D_MLSTM + 2 * MLSTM_HEADS + 2 * D_LRU

kernel_name = "hymba_mlstm_rglru_convffn"


def rmsnorm(x, g):
    xf = x.astype(jnp.float32)
    r = xf * lax.rsqrt(jnp.mean(xf * xf, axis=-1, keepdims=True) + EPS)
    return (r * g.astype(jnp.float32)).astype(x.dtype)


def causal_dwconv(x, w, b):
    K = w.shape[0]
    S = x.shape[1]
    xp = jnp.pad(x, ((0, 0), (K - 1, 0), (0, 0)))
    out = b + xp[:, 0:S, :] * w[0]
    for j in range(1, K):
        out = out + xp[:, j:j + S, :] * w[j]
    return out


def mlstm_chunkwise(q, k, v, i_pre, f_pre, head_g):
    B, S, H, dh = q.shape
    NC, L = S // CHUNK, CHUNK
    f32 = jnp.float32

    def to_chunks(t):
        t = t.astype(f32).reshape((B, NC, L, H) + t.shape[3:])
        return jnp.moveaxis(t, 3, 1)

    q = to_chunks(q) * (MLSTM_HEAD_DIM ** -0.5)
    k, v = to_chunks(k), to_chunks(v)
    ig = to_chunks(i_pre)
    logf = jax.nn.log_sigmoid(to_chunks(f_pre))
    b = jnp.cumsum(logf, axis=-1)
    b_tot = b[..., -1]

    w_end = b_tot[..., None] - b + ig
    m_loc = jnp.max(w_end, axis=-1)
    e_end = jnp.exp(w_end - m_loc[..., None])
    C_loc = jnp.einsum('bhcl,bhclk,bhclv->bhckv', e_end, k, v)
    n_loc = jnp.einsum('bhcl,bhclk->bhck', e_end, k)

    def step(carry, inp):
        C, n, m = carry
        bt, ml, Cl, nl = inp
        m_new = jnp.maximum(bt + m, ml)
        a = jnp.exp(bt + m - m_new)
        c = jnp.exp(ml - m_new)
        C_new = a[..., None, None] * C + c[..., None, None] * Cl
        n_new = a[..., None] * n + c[..., None] * nl
        return (C_new, n_new, m_new), (C, n, m)

    init = (jnp.zeros((B, H, dh, dh), f32), jnp.zeros((B, H, dh), f32), jnp.zeros((B, H), f32))
    xs = (jnp.moveaxis(b_tot, 2, 0), jnp.moveaxis(m_loc, 2, 0),
          jnp.moveaxis(C_loc, 2, 0), jnp.moveaxis(n_loc, 2, 0))
    _, (C_prev, n_prev, m_prev) = lax.scan(step, init, xs)
    C_prev = jnp.moveaxis(C_prev, 0, 2)
    n_prev = jnp.moveaxis(n_prev, 0, 2)
    m_prev = jnp.moveaxis(m_prev, 0, 2)

    causal = jnp.tril(jnp.ones((L, L), dtype=bool))
    D = b[..., :, None] - b[..., None, :] + ig[..., None, :]
    D = jnp.where(causal, D, -jnp.inf)
    m_inter = b + m_prev[..., None]
    m_t = jnp.maximum(m_inter, jnp.max(D, axis=-1))
    Wqk = jnp.einsum('bhctk,bhcsk->bhcts', q, k) * jnp.exp(D - m_t[..., None])
    e_inter = jnp.exp(m_inter - m_t)
    num = (jnp.einsum('bhcts,bhcsv->bhctv', Wqk, v)
           + e_inter[..., None] * jnp.einsum('bhctk,bhckv->bhctv', q, C_prev))
    den = jnp.sum(Wqk, axis=-1) + e_inter * jnp.einsum('bhctk,bhck->bhct', q, n_prev)
    h = num / jnp.maximum(jnp.abs(den), jnp.exp(-m_t))[..., None]

    h = h * lax.rsqrt(jnp.mean(h * h, axis=-1, keepdims=True) + EPS)
    h = h * head_g.astype(f32)[None, :, None, None, :]
    return jnp.moveaxis(h, 1, 3).reshape(B, S, H * dh)


def rglru(x_r, conv_w, conv_b, wa, ba, wx, bx, lam):
    B, S, _ = x_r.shape
    f32 = jnp.float32
    xc = causal_dwconv(x_r.astype(f32), conv_w.astype(f32), conv_b.astype(f32))
    xb = xc.reshape(B, S, LRU_BLOCKS, LRU_BLOCK_DIM)
    r = jax.nn.sigmoid(jnp.einsum('bsni,nij->bsnj', xb, wa.astype(f32)).reshape(B, S, D_LRU) + ba)
    i = jax.nn.sigmoid(jnp.einsum('bsni,nij->bsnj', xb, wx.astype(f32)).reshape(B, S, D_LRU) + bx)
    log_a = -LRU_C * r * jax.nn.softplus(-lam.astype(f32))
    a = jnp.exp(log_a)
    u = jnp.sqrt(-jnp.expm1(2.0 * log_a)) * (i * xc)

    def combine(e1, e2):
        a1, b1 = e1
        a2, b2 = e2
        return a1 * a2, a2 * b1 + b2

    _, h = lax.associative_scan(combine, (a, u), axis=1)
    return h


def setup_inputs(seed: int = 0) -> dict:
    key = jax.random.key(seed)
    ks = jax.random.split(key, 24)
    f32 = jnp.float32

    def nrm(k, shape, scale):
        return jax.random.normal(k, shape, f32) * scale

    x = jax.random.normal(ks[0], (BATCH, SEQ, D_MODEL), f32)
    norm_mix_g = 1.0 + nrm(ks[1], (DEPTH, D_MODEL), 0.02)
    w_in = nrm(ks[2], (DEPTH, D_MODEL, PROJ_COLS), D_MODEL ** -0.5)
    i_bias = nrm(ks[3], (DEPTH, MLSTM_HEADS), 0.1)
    f_bias = jnp.linspace(3.0, 6.0, MLSTM_HEADS, dtype=f32)[None, :] + nrm(ks[4], (DEPTH, MLSTM_HEADS), 0.1)
    b_gate_m = jnp.concatenate([i_bias, f_bias], axis=-1)
    mlstm_norm_g = 1.0 + nrm(ks[5], (DEPTH, MLSTM_HEADS, MLSTM_HEAD_DIM), 0.02)
    lru_conv_w = nrm(ks[6], (DEPTH, LRU_CONV, D_LRU), LRU_CONV ** -0.5)
    lru_conv_b = nrm(ks[7], (DEPTH, D_LRU), 0.01)
    lru_wa = nrm(ks[8], (DEPTH, LRU_BLOCKS, LRU_BLOCK_DIM, LRU_BLOCK_DIM), LRU_BLOCK_DIM ** -0.5)
    lru_ba = nrm(ks[9], (DEPTH, D_LRU), 0.01)
    lru_wx = nrm(ks[10], (DEPTH, LRU_BLOCKS, LRU_BLOCK_DIM, LRU_BLOCK_DIM), LRU_BLOCK_DIM ** -0.5)
    lru_bx = nrm(ks[11], (DEPTH, D_LRU), 0.01)
    u = jax.random.uniform(ks[12], (DEPTH, D_LRU), f32, 0.9, 0.999)
    a_base = u ** (1.0 / LRU_C)
    lru_lambda = jnp.log(a_base) - jnp.log1p(-a_base)
    w_out = nrm(ks[13], (DEPTH, D_MIX, D_MODEL), D_MIX ** -0.5)
    norm_ffn_g = 1.0 + nrm(ks[14], (DEPTH, D_MODEL), 0.02)
    w_up = nrm(ks[15], (DEPTH, D_MODEL, 2 * D_FF), D_MODEL ** -0.5)
    ffn_conv_w = nrm(ks[16], (DEPTH, FFN_CONV, D_FF), FFN_CONV ** -0.5)
    ffn_conv_b = nrm(ks[17], (DEPTH, D_FF), 0.01)
    w_down = nrm(ks[18], (DEPTH, D_FF, D_MODEL), D_FF ** -0.5)
    norm_final_g = 1.0 + nrm(ks[19], (D_MODEL,), 0.02)
    return {"x": x, "norm_mix_g": norm_mix_g, "w_in": w_in, "b_gate_m": b_gate_m,
            "mlstm_norm_g": mlstm_norm_g, "lru_conv_w": lru_conv_w, "lru_conv_b": lru_conv_b,
            "lru_wa": lru_wa, "lru_ba": lru_ba, "lru_wx": lru_wx, "lru_bx": lru_bx,
            "lru_lambda": lru_lambda, "w_out": w_out, "norm_ffn_g": norm_ffn_g, "w_up": w_up,
            "ffn_conv_w": ffn_conv_w, "ffn_conv_b": ffn_conv_b, "w_down": w_down,
            "norm_final_g": norm_final_g}


def reference(x, norm_mix_g, w_in, b_gate_m, mlstm_norm_g, lru_conv_w, lru_conv_b,
              lru_wa, lru_ba, lru_wx, lru_bx, lru_lambda, w_out, norm_ffn_g, w_up,
              ffn_conv_w, ffn_conv_b, w_down, norm_final_g):
    B, S, _ = x.shape
    H, dh = MLSTM_HEADS, MLSTM_HEAD_DIM
    split_at = [D_MLSTM, 2 * D_MLSTM, 3 * D_MLSTM, 4 * D_MLSTM,
                4 * D_MLSTM + 2 * H, 4 * D_MLSTM + 2 * H + D_LRU]
    for l in range(DEPTH):
        n1 = rmsnorm(x, norm_mix_g[l])
        proj = n1 @ w_in[l]
        q, k, v, o_pre, if_pre, x_r, g_r = jnp.split(proj, split_at, axis=-1)
        if_pre = if_pre.astype(jnp.float32) + b_gate_m[l].astype(jnp.float32)
        i_pre, f_pre = if_pre[..., :H], if_pre[..., H:]
        h_m = mlstm_chunkwise(q.reshape(B, S, H, dh), k.reshape(B, S, H, dh),
                              v.reshape(B, S, H, dh), i_pre, f_pre, mlstm_norm_g[l])
        h_m = jax.nn.sigmoid(o_pre.astype(jnp.float32)) * h_m
        h_r = rglru(x_r, lru_conv_w[l], lru_conv_b[l], lru_wa[l], lru_ba[l],
                    lru_wx[l], lru_bx[l], lru_lambda[l])
        h_r = h_r * jax.nn.gelu(g_r.astype(jnp.float32))
        mix = jnp.concatenate([h_m, h_r], axis=-1).astype(x.dtype)
        x = x + mix @ w_out[l]
        n2 = rmsnorm(x, norm_ffn_g[l])
        gate, up = jnp.split(n2 @ w_up[l], 2, axis=-1)
        gate = causal_dwconv(gate, ffn_conv_w[l], ffn_conv_b[l])
        x = x + (jax.nn.silu(gate) * up) @ w_down[l]
    return rmsnorm(x, norm_final_g)
```

```python
import functools
import math

import jax
import jax.numpy as jnp
from jax import lax
from jax.experimental import pallas as pl
from jax.experimental.pallas import tpu as pltpu

F32 = jnp.float32
BF16 = jnp.bfloat16
EPS = 1e-6
LANES = 128
SUBLANES = 8
VMEM_LIMIT = 56 * 1024 * 1024

MLSTM_HEADS = 4
LRU_BLOCKS = 8
LRU_CONV = 4
LRU_C = 8.0
FFN_CONV = 3
MLSTM_CHUNK = 256


def _sigmoid(x):
    return 1.0 / (1.0 + jnp.exp(-x))


def _log_sigmoid(x):
    return jnp.minimum(x, 0.0) - jnp.log1p(jnp.exp(-jnp.abs(x)))


def _softplus(x):
    return jnp.maximum(x, 0.0) + jnp.log1p(jnp.exp(-jnp.abs(x)))


def _gelu_tanh(x):
    c = math.sqrt(2.0 / math.pi)
    return 0.5 * x * (1.0 + jnp.tanh(c * (x + 0.044715 * (x * x * x))))


def _rmsnorm(x, g):
    return x * lax.rsqrt(jnp.mean(x * x, axis=-1, keepdims=True) + EPS) * g


def _scan_lanes(x, op, fill):
    n = x.shape[-1]
    lane = lax.broadcasted_iota(jnp.int32, x.shape, x.ndim - 1)
    d = 1
    while d < n:
        shifted = jnp.where(lane >= d, pltpu.roll(x, d, axis=x.ndim - 1), fill)
        x = op(x, shifted)
        d *= 2
    return x


def _inproj_kernel(x_ref, g_ref, w_ref, wif_ref, bif_ref, proj_ref, gate_ref, n1_ref):
    @pl.when(pl.program_id(1) == 0)
    def _():
        n1 = _rmsnorm(x_ref[...], g_ref[...]).astype(BF16)
        n1_ref[...] = n1
        gate_ref[...] = jnp.dot(n1, wif_ref[...], preferred_element_type=F32) + bif_ref[...]

    proj_ref[...] = jnp.dot(n1_ref[...], w_ref[...],
                            preferred_element_type=F32).astype(proj_ref.dtype)


def _inproj(x, g, w_main, w_if, b_if, *, tm, tn):
    T, D = x.shape
    N = w_main.shape[1]
    return pl.pallas_call(
        _inproj_kernel,
        grid=(T // tm, N // tn),
        in_specs=[
            pl.BlockSpec((tm, D), lambda m, n: (m, 0)),
            pl.BlockSpec((1, D), lambda m, n: (0, 0)),
            pl.BlockSpec((D, tn), lambda m, n: (0, n)),
            pl.BlockSpec((D, LANES), lambda m, n: (0, 0)),
            pl.BlockSpec((1, LANES), lambda m, n: (0, 0)),
        ],
        out_specs=[
            pl.BlockSpec((tm, tn), lambda m, n: (m, n)),
            pl.BlockSpec((tm, LANES), lambda m, n: (m, 0)),
        ],
        out_shape=[
            jax.ShapeDtypeStruct((T, N), BF16),
            jax.ShapeDtypeStruct((T, LANES), F32),
        ],
        scratch_shapes=[pltpu.VMEM((tm, D), BF16)],
        compiler_params=pltpu.CompilerParams(
            dimension_semantics=("arbitrary", "arbitrary"),
            vmem_limit_bytes=VMEM_LIMIT),
        name="inproj",
    )(x, g, w_main, w_if, b_if)


def _mlstm_kernel(q_ref, k_ref, v_ref, o_ref, gate_ref, hg_ref, out_ref,
                  c_ref, n_ref, m_ref, *, heads, dh, chunk):
    L = chunk
    scale = dh ** -0.5

    @pl.when(pl.program_id(1) == 0)
    def _():
        c_ref[...] = jnp.zeros_like(c_ref)
        n_ref[...] = jnp.zeros_like(n_ref)
        m_ref[...] = jnp.zeros_like(m_ref)

    g8 = gate_ref[...].T[0:SUBLANES, :]
    g8r = pltpu.roll(g8, heads, axis=0)
    sub = lax.broadcasted_iota(jnp.int32, g8.shape, 0)
    ig = jnp.where(sub < heads, g8, g8r)
    fg = jnp.where(sub < heads, g8r, g8)
    b = _scan_lanes(_log_sigmoid(fg), jnp.add, 0.0)
    a_row = ig - b
    cm = _scan_lanes(a_row, jnp.maximum, -jnp.inf)
    b_tot = b[:, L - 1:L]
    m_prev = m_ref[:, 0:1]
    m_inter = b + m_prev
    m_t = jnp.maximum(m_inter, b + cm)
    e_inter = jnp.exp(m_inter - m_t)
    floor = jnp.exp(-m_t)
    m_new = jnp.maximum(b_tot + m_prev, b_tot + cm[:, L - 1:L])
    decay = jnp.exp(b_tot + m_prev - m_new)
    e_end = jnp.exp(b_tot + a_row - m_new)
    m_ref[...] = jnp.broadcast_to(m_new, m_ref.shape)

    packed = jnp.concatenate(
        [b - m_t, e_inter, floor, e_end,
         jnp.zeros((LANES - 4 * SUBLANES, L), F32)], axis=0)
    cols = packed.T
    e_end_b = e_end.astype(BF16)

    row_i = lax.broadcasted_iota(jnp.int32, (L, L), 0)
    col_i = lax.broadcasted_iota(jnp.int32, (L, L), 1)
    causal = row_i >= col_i

    for h in range(heads):
        sl = slice(h * dh, (h + 1) * dh)
        q = q_ref[:, sl]
        k = k_ref[:, sl]
        v = v_ref[:, sl]
        bm_col = cols[:, h:h + 1]
        einter_col = cols[:, SUBLANES + h:SUBLANES + h + 1]
        floor_col = cols[:, 2 * SUBLANES + h:2 * SUBLANES + h + 1]
        eend_col = cols[:, 3 * SUBLANES + h:3 * SUBLANES + h + 1]

        s = lax.dot_general(q, k, (((1,), (1,)), ((), ())), preferred_element_type=F32)
        dmat = jnp.where(causal, bm_col + a_row[h:h + 1, :], -jnp.inf)
        w = (s * scale) * jnp.exp(dmat)
        c_prev = c_ref[h]
        n_prev = n_ref[h]
        inter = jnp.dot(q, c_prev.astype(BF16), preferred_element_type=F32) * scale
        num = jnp.dot(w.astype(BF16), v, preferred_element_type=F32) + einter_col * inter
        qn = jnp.sum(q.astype(F32) * n_prev, axis=-1, keepdims=True) * scale
        den = jnp.sum(w, axis=-1, keepdims=True) + einter_col * qn
        hh = num / jnp.maximum(jnp.abs(den), floor_col)
        hh = _rmsnorm(hh, hg_ref[h])
        out_ref[:, sl] = (_sigmoid(o_ref[:, sl].astype(F32)) * hh).astype(out_ref.dtype)

        ev = (eend_col * v.astype(F32)).astype(BF16)
        c_loc = lax.dot_general(k, ev, (((0,), (0,)), ((), ())), preferred_element_type=F32)
        n_loc = jnp.dot(e_end_b, k, preferred_element_type=F32)[h:h + 1, :]
        dec = decay[h:h + 1, :]
        c_ref[h] = dec * c_prev + c_loc
        n_ref[h] = dec * n_prev + n_loc


def _mlstm(proj, gates, head_g, *, batch, seq, heads, dh, chunk):
    T = batch * seq
    W = heads * dh
    nc = seq // chunk
    blk = lambda j: pl.BlockSpec((chunk, W), lambda b, c, j=j: (b * nc + c, j))
    kern = functools.partial(_mlstm_kernel, heads=heads, dh=dh, chunk=chunk)
    return pl.pallas_call(
        kern,
        grid=(batch, nc),
        in_specs=[
            blk(0), blk(1), blk(2), blk(3),
            pl.BlockSpec((chunk, LANES), lambda b, c: (b * nc + c, 0)),
            pl.BlockSpec((heads, 1, dh), lambda b, c: (0, 0, 0)),
        ],
        out_specs=pl.BlockSpec((chunk, W), lambda b, c: (b * nc + c, 0)),
        out_shape=jax.ShapeDtypeStruct((T, W), BF16),
        scratch_shapes=[
            pltpu.VMEM((heads, dh, dh), F32),
            pltpu.VMEM((heads, 1, dh), F32),
            pltpu.VMEM((SUBLANES, LANES), F32),
        ],
        compiler_params=pltpu.CompilerParams(
            dimension_semantics=("arbitrary", "arbitrary"),
            vmem_limit_bytes=VMEM_LIMIT),
        name="mlstm",
    )(proj, proj, proj, proj, gates, head_g)


def _lru_kernel(xr_ref, gr_ref, cw_ref, cb_ref, wax_ref, ba_ref, bx_ref, lam_ref, out_ref,
                xtail_ref, hc_ref, a_scr, u_scr, *, ts, blocks, bd):
    C = blocks * bd

    @pl.when(pl.program_id(1) == 0)
    def _():
        xtail_ref[...] = jnp.zeros_like(xtail_ref)
        hc_ref[...] = jnp.zeros_like(hc_ref)

    x = xr_ref[...].astype(F32)
    prev = xtail_ref[...]
    row8 = lax.broadcasted_iota(jnp.int32, (SUBLANES, C), 0)
    cw = cw_ref[...]
    xc = cb_ref[...] + cw[LRU_CONV - 1:LRU_CONV, :] * x
    for j in range(1, LRU_CONV):
        r = pltpu.roll(x, j, axis=0)
        top = jnp.where(row8 < j, pltpu.roll(prev, j, axis=0), r[0:SUBLANES, :])
        xs = jnp.concatenate([top, r[SUBLANES:, :]], axis=0)
        xc = xc + cw[LRU_CONV - 1 - j:LRU_CONV - j, :] * xs
    xtail_ref[...] = x[ts - SUBLANES:ts, :]

    xcb = xc.astype(BF16)
    sp = _softplus(-lam_ref[...])
    rowm = lax.broadcasted_iota(jnp.int32, (ts, bd), 0) % SUBLANES
    for n in range(blocks):
        sl = slice(n * bd, (n + 1) * bd)
        ax = jnp.dot(xcb[:, sl], wax_ref[n], preferred_element_type=F32)
        r = _sigmoid(ax[:, :bd] + ba_ref[:, sl])
        i = _sigmoid(ax[:, bd:] + bx_ref[:, sl])
        log_a = (-LRU_C) * r * sp[:, sl]
        a = jnp.exp(log_a)
        u = jnp.sqrt(1.0 - a * a) * (i * xc[:, sl])
        d = 1
        while d < SUBLANES:
            a_sh = pltpu.roll(a, d, axis=0)
            u_sh = pltpu.roll(u, d, axis=0)
            keep = rowm >= d
            u = jnp.where(keep, a * u_sh + u, u)
            a = jnp.where(keep, a * a_sh, a)
            d *= 2
        a_scr[:, sl] = a
        u_scr[:, sl] = u

    def body(t, carry):
        r0 = pl.multiple_of(t * SUBLANES, SUBLANES)
        h = a_scr[pl.ds(r0, SUBLANES), :] * carry + u_scr[pl.ds(r0, SUBLANES), :]
        u_scr[pl.ds(r0, SUBLANES), :] = h
        return jnp.broadcast_to(h[SUBLANES - 1:SUBLANES, :], (SUBLANES, C))

    hc_ref[...] = lax.fori_loop(0, ts // SUBLANES, body, hc_ref[...], unroll=8)
    out_ref[...] = (u_scr[...] * _gelu_tanh(gr_ref[...].astype(F32))).astype(out_ref.dtype)


def _rglru(proj, conv_w, conv_b, w_ax, ba, bx, lam, *, batch, seq, col_block, ts):
    T = batch * seq
    blocks, bd, _ = w_ax.shape
    C = blocks * bd
    ns = seq // ts
    vec = lambda rows: pl.BlockSpec((rows, C), lambda b, s: (0, 0))
    kern = functools.partial(_lru_kernel, ts=ts, blocks=blocks, bd=bd)
    return pl.pallas_call(
        kern,
        grid=(batch, ns),
        in_specs=[
            pl.BlockSpec((ts, C), lambda b, s: (b * ns + s, col_block)),
            pl.BlockSpec((ts, C), lambda b, s: (b * ns + s, col_block + 1)),
            vec(LRU_CONV), vec(1),
            pl.BlockSpec((blocks, bd, 2 * bd), lambda b, s: (0, 0, 0)),
            vec(1), vec(1), vec(1),
        ],
        out_specs=pl.BlockSpec((ts, C), lambda b, s: (b * ns + s, 0)),
        out_shape=jax.ShapeDtypeStruct((T, C), BF16),
        scratch_shapes=[
            pltpu.VMEM((SUBLANES, C), F32),
            pltpu.VMEM((SUBLANES, C), F32),
            pltpu.VMEM((ts, C), F32),
            pltpu.VMEM((ts, C), F32),
        ],
        compiler_params=pltpu.CompilerParams(
            dimension_semantics=("arbitrary", "arbitrary"),
            vmem_limit_bytes=VMEM_LIMIT),
        name="rglru",
    )(proj, proj, conv_w, conv_b, w_ax, ba, bx, lam)


def _outproj_kernel(x_ref, hm_ref, hr_ref, wm_ref, wr_ref, g_ref, x1_ref, n2_ref):
    x1 = (x_ref[...]
          + jnp.dot(hm_ref[...], wm_ref[...], preferred_element_type=F32)
          + jnp.dot(hr_ref[...], wr_ref[...], preferred_element_type=F32))
    x1_ref[...] = x1
    n2_ref[...] = _rmsnorm(x1, g_ref[...]).astype(n2_ref.dtype)


def _outproj(x, hm, hr, w_out, g, *, tm):
    T, D = x.shape
    Wm = hm.shape[1]
    Wr = hr.shape[1]
    return pl.pallas_call(
        _outproj_kernel,
        grid=(T // tm,),
        in_specs=[
            pl.BlockSpec((tm, D), lambda m: (m, 0)),
            pl.BlockSpec((tm, Wm), lambda m: (m, 0)),
            pl.BlockSpec((tm, Wr), lambda m: (m, 0)),
            pl.BlockSpec((Wm, D), lambda m: (0, 0)),
            pl.BlockSpec((Wr, D), lambda m: (Wm // Wr, 0)),
            pl.BlockSpec((1, D), lambda m: (0, 0)),
        ],
        out_specs=[
            pl.BlockSpec((tm, D), lambda m: (m, 0)),
            pl.BlockSpec((tm, D), lambda m: (m, 0)),
        ],
        out_shape=[
            jax.ShapeDtypeStruct((T, D), F32),
            jax.ShapeDtypeStruct((T, D), BF16),
        ],
        compiler_params=pltpu.CompilerParams(
            dimension_semantics=("arbitrary",),
            vmem_limit_bytes=VMEM_LIMIT),
        name="outproj",
    )(x, hm, hr, w_out, w_out, g)


def _ffn_kernel(n2_ref, x1_ref, wg_ref, wu_ref, cw_ref, cb_ref, wd_ref, gf_ref, y_ref,
                acc_ref, tail_ref, *, tm, tiles_per_seq, final_norm):
    m = pl.program_id(0)
    f = pl.program_id(1)
    nf = pl.num_programs(1)

    n2 = n2_ref[...]
    gate = jnp.dot(n2, wg_ref[...], preferred_element_type=F32)
    up = jnp.dot(n2, wu_ref[...], preferred_element_type=F32)

    prev = jnp.where(m % tiles_per_seq == 0, 0.0, tail_ref[f])
    tail_ref[f] = gate[tm - SUBLANES:tm, :]

    row8 = lax.broadcasted_iota(jnp.int32, prev.shape, 0)
    cw = cw_ref[...]
    conv = cb_ref[...] + cw[FFN_CONV - 1:FFN_CONV, :] * gate
    for j in range(1, FFN_CONV):
        r = pltpu.roll(gate, j, axis=0)
        top = jnp.where(row8 < j, pltpu.roll(prev, j, axis=0), r[0:SUBLANES, :])
        gs = jnp.concatenate([top, r[SUBLANES:, :]], axis=0)
        conv = conv + cw[FFN_CONV - 1 - j:FFN_CONV - j, :] * gs
    hid = (conv * _sigmoid(conv)) * up
    contrib = jnp.dot(hid.astype(BF16), wd_ref[...], preferred_element_type=F32)

    @pl.when(f == 0)
    def _():
        acc_ref[...] = x1_ref[...] + contrib

    @pl.when(f > 0)
    def _():
        acc_ref[...] += contrib

    @pl.when(f == nf - 1)
    def _():
        xo = acc_ref[...]
        y_ref[...] = _rmsnorm(xo, gf_ref[...]) if final_norm else xo


def _ffn(n2, x1, w_up, conv_w, conv_b, w_down, gf, *, seq, tm, tf, final_norm):
    T, D = n2.shape
    F = w_down.shape[0]
    nf = F // tf
    kern = functools.partial(_ffn_kernel, tm=tm, tiles_per_seq=seq // tm, final_norm=final_norm)
    return pl.pallas_call(
        kern,
        grid=(T // tm, nf),
        in_specs=[
            pl.BlockSpec((tm, D), lambda m, f: (m, 0)),
            pl.BlockSpec((tm, D), lambda m, f: (m, 0)),
            pl.BlockSpec((D, tf), lambda m, f: (0, f)),
            pl.BlockSpec((D, tf), lambda m, f: (0, nf + f)),
            pl.BlockSpec((FFN_CONV, tf), lambda m, f: (0, f)),
            pl.BlockSpec((1, tf), lambda m, f: (0, f)),
            pl.BlockSpec((tf, D), lambda m, f: (f, 0)),
            pl.BlockSpec((1, D), lambda m, f: (0, 0)),
        ],
        out_specs=pl.BlockSpec((tm, D), lambda m, f: (m, 0)),
        out_shape=jax.ShapeDtypeStruct((T, D), F32),
        scratch_shapes=[
            pltpu.VMEM((tm, D), F32),
            pltpu.VMEM((nf, SUBLANES, tf), F32),
        ],
        compiler_params=pltpu.CompilerParams(
            dimension_semantics=("arbitrary", "arbitrary"),
            vmem_limit_bytes=VMEM_LIMIT),
        name="convffn",
    )(n2, x1, w_up, w_up, conv_w, conv_b, w_down, gf)


def kernel(x, norm_mix_g, w_in, b_gate_m, mlstm_norm_g, lru_conv_w, lru_conv_b, lru_wa, lru_ba,
           lru_wx, lru_bx, lru_lambda, w_out, norm_ffn_g, w_up, ffn_conv_w, ffn_conv_b, w_down,
           norm_final_g):
    B, S, D = x.shape
    T = B * S
    depth = w_in.shape[0]
    H = MLSTM_HEADS
    dh = mlstm_norm_g.shape[-1]
    d_m = H * dh
    d_r = lru_conv_w.shape[-1]
    assert d_m == d_r, "column-block addressing assumes equal head-group widths"
    n_gate = 2 * H
    gate_lo = 4 * d_m

    xf = x.reshape(T, D)
    for l in range(depth):
        wl = w_in[l]
        w_main = jnp.concatenate([wl[:, :gate_lo], wl[:, gate_lo + n_gate:]], axis=1).astype(BF16)
        w_if = jnp.pad(wl[:, gate_lo:gate_lo + n_gate], ((0, 0), (0, LANES - n_gate))).astype(BF16)
        b_if = jnp.pad(b_gate_m[l].astype(F32), (0, LANES - n_gate)).reshape(1, LANES)
        proj, gates = _inproj(xf, norm_mix_g[l].reshape(1, D), w_main, w_if, b_if, tm=1024, tn=1024)

        hm = _mlstm(proj, gates, mlstm_norm_g[l].reshape(H, 1, dh),
                    batch=B, seq=S, heads=H, dh=dh, chunk=MLSTM_CHUNK)

        w_ax = jnp.concatenate([lru_wa[l], lru_wx[l]], axis=-1).astype(BF16)
        hr = _rglru(proj, lru_conv_w[l], lru_conv_b[l].reshape(1, d_r), w_ax,
                    lru_ba[l].reshape(1, d_r), lru_bx[l].reshape(1, d_r),
                    lru_lambda[l].reshape(1, d_r),
                    batch=B, seq=S, col_block=gate_lo // d_r, ts=512)

        x1, n2 = _outproj(xf, hm, hr, w_out[l].astype(BF16), norm_ffn_g[l].reshape(1, D), tm=512)

        xf = _ffn(n2, x1, w_up[l].astype(BF16), ffn_conv_w[l], ffn_conv_b[l].reshape(1, -1),
                  w_down[l].astype(BF16), norm_final_g.reshape(1, D), seq=S, tm=512, tf=512,
                  final_norm=(l == depth - 1))
    return xf.reshape(B, S, D)
```

```python
import functools
import math

import jax
import jax.numpy as jnp
from jax import lax
from jax.experimental import pallas as pl
from jax.experimental.pallas import tpu as pltpu

F32 = jnp.float32
BF16 = jnp.bfloat16
EPS = 1e-6
LANES = 128
SUBLANES = 8
VMEM_LIMIT = 56 * 1024 * 1024

MLSTM_HEADS = 4
LRU_BLOCKS = 8
LRU_CONV = 4
LRU_C = 8.0
FFN_CONV = 3
MLSTM_CHUNK = 256


def _sigmoid(x):
    return 1.0 / (1.0 + jnp.exp(-x))


def _log_sigmoid(x):
    return jnp.minimum(x, 0.0) - jnp.log1p(jnp.exp(-jnp.abs(x)))


def _softplus(x):
    return jnp.maximum(x, 0.0) + jnp.log1p(jnp.exp(-jnp.abs(x)))


def _gelu_tanh(x):
    c = math.sqrt(2.0 / math.pi)
    return 0.5 * x * (1.0 + jnp.tanh(c * (x + 0.044715 * (x * x * x))))


def _rmsnorm(x, g):
    return x * lax.rsqrt(jnp.mean(x * x, axis=-1, keepdims=True) + EPS) * g


def _scan_lanes(x, op, fill):
    n = x.shape[-1]
    lane = lax.broadcasted_iota(jnp.int32, x.shape, x.ndim - 1)
    d = 1
    while d < n:
        shifted = jnp.where(lane >= d, pltpu.roll(x, d, axis=x.ndim - 1), fill)
        x = op(x, shifted)
        d *= 2
    return x


def _win_cast_kernel(w_ref, nxt_ref, gcol_ref, out_ref, wif_ref, *, first_shifted, shift):
    j = pl.program_id(0)
    tn = w_ref.shape[1]

    @pl.when(j < first_shifted)
    def _():
        out_ref[...] = w_ref[...].astype(BF16)

    @pl.when(j >= first_shifted)
    def _():
        wcat = jnp.concatenate([w_ref[...], nxt_ref[...]], axis=1)
        out_ref[...] = pltpu.roll(wcat, wcat.shape[1] - shift, axis=1)[:, :tn].astype(BF16)

    wif_ref[...] = gcol_ref[...].astype(BF16)


def _win_cast(w, *, gate_lo, n_gate, tn):
    D, P = w.shape
    N = P - n_gate
    assert gate_lo % tn == 0 and N % tn == 0 and n_gate < LANES
    first_shifted = gate_lo // tn
    per = tn // LANES
    kern = functools.partial(_win_cast_kernel, first_shifted=first_shifted, shift=n_gate)
    return pl.pallas_call(
        kern,
        grid=(N // tn,),
        in_specs=[
            pl.BlockSpec((D, tn), lambda j: (0, j)),
            pl.BlockSpec((D, LANES), lambda j: (0, jnp.maximum(j, first_shifted) * per + per)),
            pl.BlockSpec((D, LANES), lambda j: (0, gate_lo // LANES)),
        ],
        out_specs=[
            pl.BlockSpec((D, tn), lambda j: (0, j)),
            pl.BlockSpec((D, LANES), lambda j: (0, 0)),
        ],
        out_shape=[
            jax.ShapeDtypeStruct((D, N), BF16),
            jax.ShapeDtypeStruct((D, LANES), BF16),
        ],
        compiler_params=pltpu.CompilerParams(
            dimension_semantics=("arbitrary",),
            vmem_limit_bytes=VMEM_LIMIT),
        name="win_cast",
    )(w, w, w)


def _inproj_kernel(x_ref, g_ref, w_ref, wif_ref, bif_ref, proj_ref, gate_ref, n1_ref):
    @pl.when(pl.program_id(1) == 0)
    def _():
        n1 = _rmsnorm(x_ref[...], g_ref[...]).astype(BF16)
        n1_ref[...] = n1
        gate_ref[...] = jnp.dot(n1, wif_ref[...], preferred_element_type=F32) + bif_ref[...]

    proj_ref[...] = jnp.dot(n1_ref[...], w_ref[...],
                            preferred_element_type=F32).astype(proj_ref.dtype)


def _inproj(x, g, w_main, w_if, b_if, *, tm, tn):
    T, D = x.shape
    N = w_main.shape[1]
    return pl.pallas_call(
        _inproj_kernel,
        grid=(T // tm, N // tn),
        in_specs=[
            pl.BlockSpec((tm, D), lambda m, n: (m, 0)),
            pl.BlockSpec((1, D), lambda m, n: (0, 0)),
            pl.BlockSpec((D, tn), lambda m, n: (0, n)),
            pl.BlockSpec((D, LANES), lambda m, n: (0, 0)),
            pl.BlockSpec((1, LANES), lambda m, n: (0, 0)),
        ],
        out_specs=[
            pl.BlockSpec((tm, tn), lambda m, n: (m, n)),
            pl.BlockSpec((tm, LANES), lambda m, n: (m, 0)),
        ],
        out_shape=[
            jax.ShapeDtypeStruct((T, N), BF16),
            jax.ShapeDtypeStruct((T, LANES), F32),
        ],
        scratch_shapes=[pltpu.VMEM((tm, D), BF16)],
        compiler_params=pltpu.CompilerParams(
            dimension_semantics=("arbitrary", "arbitrary"),
            vmem_limit_bytes=VMEM_LIMIT),
        name="inproj",
    )(x, g, w_main, w_if, b_if)


def _mlstm_kernel(q_ref, k_ref, v_ref, o_ref, gate_ref, hg_ref, wup_ref, out_ref, wup_bf_ref,
                  c_ref, n_ref, m_ref, *, heads, dh, chunk):
    L = chunk
    scale = dh ** -0.5

    wup_bf_ref[...] = wup_ref[...].astype(BF16)

    @pl.when(pl.program_id(1) == 0)
    def _():
        c_ref[...] = jnp.zeros_like(c_ref)
        n_ref[...] = jnp.zeros_like(n_ref)
        m_ref[...] = jnp.zeros_like(m_ref)

    g8 = gate_ref[...].T[0:SUBLANES, :]
    g8r = pltpu.roll(g8, heads, axis=0)
    sub = lax.broadcasted_iota(jnp.int32, g8.shape, 0)
    ig = jnp.where(sub < heads, g8, g8r)
    fg = jnp.where(sub < heads, g8r, g8)
    b = _scan_lanes(_log_sigmoid(fg), jnp.add, 0.0)
    a_row = ig - b
    cm = _scan_lanes(a_row, jnp.maximum, -jnp.inf)
    b_tot = b[:, L - 1:L]
    m_prev = m_ref[:, 0:1]
    m_inter = b + m_prev
    m_t = jnp.maximum(m_inter, b + cm)
    e_inter = jnp.exp(m_inter - m_t)
    floor = jnp.exp(-m_t)
    m_new = jnp.maximum(b_tot + m_prev, b_tot + cm[:, L - 1:L])
    decay = jnp.exp(b_tot + m_prev - m_new)
    e_end = jnp.exp(b_tot + a_row - m_new)
    m_ref[...] = jnp.broadcast_to(m_new, m_ref.shape)

    packed = jnp.concatenate(
        [b - m_t, e_inter, floor, e_end,
         jnp.zeros((LANES - 4 * SUBLANES, L), F32)], axis=0)
    cols = packed.T
    e_end_b = e_end.astype(BF16)

    row_i = lax.broadcasted_iota(jnp.int32, (L, L), 0)
    col_i = lax.broadcasted_iota(jnp.int32, (L, L), 1)
    causal = row_i >= col_i

    for h in range(heads):
        sl = slice(h * dh, (h + 1) * dh)
        q = q_ref[:, sl]
        k = k_ref[:, sl]
        v = v_ref[:, sl]
        bm_col = cols[:, h:h + 1]
        einter_col = cols[:, SUBLANES + h:SUBLANES + h + 1]
        floor_col = cols[:, 2 * SUBLANES + h:2 * SUBLANES + h + 1]
        eend_col = cols[:, 3 * SUBLANES + h:3 * SUBLANES + h + 1]

        s = lax.dot_general(q, k, (((1,), (1,)), ((), ())), preferred_element_type=F32)
        dmat = jnp.where(causal, bm_col + a_row[h:h + 1, :], -jnp.inf)
        w = (s * scale) * jnp.exp(dmat)
        c_prev = c_ref[h]
        n_prev = n_ref[h]
        inter = jnp.dot(q, c_prev.astype(BF16), preferred_element_type=F32) * scale
        num = jnp.dot(w.astype(BF16), v, preferred_element_type=F32) + einter_col * inter
        qn = jnp.sum(q.astype(F32) * n_prev, axis=-1, keepdims=True) * scale
        den = jnp.sum(w, axis=-1, keepdims=True) + einter_col * qn
        hh = num / jnp.maximum(jnp.abs(den), floor_col)
        hh = _rmsnorm(hh, hg_ref[h])
        out_ref[:, sl] = (_sigmoid(o_ref[:, sl].astype(F32)) * hh).astype(out_ref.dtype)

        ev = (eend_col * v.astype(F32)).astype(BF16)
        c_loc = lax.dot_general(k, ev, (((0,), (0,)), ((), ())), preferred_element_type=F32)
        n_loc = jnp.dot(e_end_b, k, preferred_element_type=F32)[h:h + 1, :]
        dec = decay[h:h + 1, :]
        c_ref[h] = dec * c_prev + c_loc
        n_ref[h] = dec * n_prev + n_loc


def _mlstm(proj, gates, head_g, w_up, *, batch, seq, heads, dh, chunk):
    T = batch * seq
    W = heads * dh
    nc = seq // chunk
    steps = batch * nc
    up_rows = w_up.shape[0] // steps
    assert up_rows * steps == w_up.shape[0] and up_rows % 16 == 0
    up_spec = pl.BlockSpec((up_rows, w_up.shape[1]), lambda b, c: (b * nc + c, 0))
    blk = lambda j: pl.BlockSpec((chunk, W), lambda b, c, j=j: (b * nc + c, j))
    kern = functools.partial(_mlstm_kernel, heads=heads, dh=dh, chunk=chunk)
    return pl.pallas_call(
        kern,
        grid=(batch, nc),
        in_specs=[
            blk(0), blk(1), blk(2), blk(3),
            pl.BlockSpec((chunk, LANES), lambda b, c: (b * nc + c, 0)),
            pl.BlockSpec((heads, 1, dh), lambda b, c: (0, 0, 0)),
            up_spec,
        ],
        out_specs=[pl.BlockSpec((chunk, W), lambda b, c: (b * nc + c, 0)), up_spec],
        out_shape=[jax.ShapeDtypeStruct((T, W), BF16), jax.ShapeDtypeStruct(w_up.shape, BF16)],
        scratch_shapes=[
            pltpu.VMEM((heads, dh, dh), F32),
            pltpu.VMEM((heads, 1, dh), F32),
            pltpu.VMEM((SUBLANES, LANES), F32),
        ],
        compiler_params=pltpu.CompilerParams(
            dimension_semantics=("arbitrary", "arbitrary"),
            vmem_limit_bytes=VMEM_LIMIT),
        name="mlstm",
    )(proj, proj, proj, proj, gates, head_g, w_up)


def _lru_kernel(xr_ref, gr_ref, cw_ref, cb_ref, wax_ref, ba_ref, bx_ref, lam_ref, wdn_ref, wout_ref,
                out_ref, wdn_bf_ref, wout_bf_ref,
                xtail_ref, hc_ref, a_scr, u_scr, *, ts, blocks, bd):
    C = blocks * bd

    wdn_bf_ref[...] = wdn_ref[...].astype(BF16)
    wout_bf_ref[...] = wout_ref[...].astype(BF16)

    @pl.when(pl.program_id(1) == 0)
    def _():
        xtail_ref[...] = jnp.zeros_like(xtail_ref)
        hc_ref[...] = jnp.zeros_like(hc_ref)

    x = xr_ref[...].astype(F32)
    prev = xtail_ref[...]
    row8 = lax.broadcasted_iota(jnp.int32, (SUBLANES, C), 0)
    cw = cw_ref[...]
    xc = cb_ref[...] + cw[LRU_CONV - 1:LRU_CONV, :] * x
    for j in range(1, LRU_CONV):
        r = pltpu.roll(x, j, axis=0)
        top = jnp.where(row8 < j, pltpu.roll(prev, j, axis=0), r[0:SUBLANES, :])
        xs = jnp.concatenate([top, r[SUBLANES:, :]], axis=0)
        xc = xc + cw[LRU_CONV - 1 - j:LRU_CONV - j, :] * xs
    xtail_ref[...] = x[ts - SUBLANES:ts, :]

    xcb = xc.astype(BF16)
    sp = _softplus(-lam_ref[...])
    rowm = lax.broadcasted_iota(jnp.int32, (ts, bd), 0) % SUBLANES
    for n in range(blocks):
        sl = slice(n * bd, (n + 1) * bd)
        ax = jnp.dot(xcb[:, sl], wax_ref[n], preferred_element_type=F32)
        r = _sigmoid(ax[:, :bd] + ba_ref[:, sl])
        i = _sigmoid(ax[:, bd:] + bx_ref[:, sl])
        log_a = (-LRU_C) * r * sp[:, sl]
        a = jnp.exp(log_a)
        u = jnp.sqrt(1.0 - a * a) * (i * xc[:, sl])
        d = 1
        while d < SUBLANES:
            a_sh = pltpu.roll(a, d, axis=0)
            u_sh = pltpu.roll(u, d, axis=0)
            keep = rowm >= d
            u = jnp.where(keep, a * u_sh + u, u)
            a = jnp.where(keep, a * a_sh, a)
            d *= 2
        a_scr[:, sl] = a
        u_scr[:, sl] = u

    def body(t, carry):
        r0 = pl.multiple_of(t * SUBLANES, SUBLANES)
        h = a_scr[pl.ds(r0, SUBLANES), :] * carry + u_scr[pl.ds(r0, SUBLANES), :]
        u_scr[pl.ds(r0, SUBLANES), :] = h
        return jnp.broadcast_to(h[SUBLANES - 1:SUBLANES, :], (SUBLANES, C))

    hc_ref[...] = lax.fori_loop(0, ts // SUBLANES, body, hc_ref[...], unroll=8)
    out_ref[...] = (u_scr[...] * _gelu_tanh(gr_ref[...].astype(F32))).astype(out_ref.dtype)


def _rglru(proj, conv_w, conv_b, w_ax, ba, bx, lam, w_down, w_out, *, batch, seq, col_block, ts):
    T = batch * seq
    blocks, bd, _ = w_ax.shape
    C = blocks * bd
    ns = seq // ts
    steps = batch * ns

    def slab(w):
        rows = w.shape[0] // steps
        assert rows * steps == w.shape[0] and rows % 16 == 0
        return pl.BlockSpec((rows, w.shape[1]), lambda b, s: (b * ns + s, 0))
    vec = lambda rows: pl.BlockSpec((rows, C), lambda b, s: (0, 0))
    kern = functools.partial(_lru_kernel, ts=ts, blocks=blocks, bd=bd)
    return pl.pallas_call(
        kern,
        grid=(batch, ns),
        in_specs=[
            pl.BlockSpec((ts, C), lambda b, s: (b * ns + s, col_block)),
            pl.BlockSpec((ts, C), lambda b, s: (b * ns + s, col_block + 1)),
            vec(LRU_CONV), vec(1),
            pl.BlockSpec((blocks, bd, 2 * bd), lambda b, s: (0, 0, 0)),
            vec(1), vec(1), vec(1),
            slab(w_down), slab(w_out),
        ],
        out_specs=[pl.BlockSpec((ts, C), lambda b, s: (b * ns + s, 0)), slab(w_down), slab(w_out)],
        out_shape=[jax.ShapeDtypeStruct((T, C), BF16),
                   jax.ShapeDtypeStruct(w_down.shape, BF16),
                   jax.ShapeDtypeStruct(w_out.shape, BF16)],
        scratch_shapes=[
            pltpu.VMEM((SUBLANES, C), F32),
            pltpu.VMEM((SUBLANES, C), F32),
            pltpu.VMEM((ts, C), F32),
            pltpu.VMEM((ts, C), F32),
        ],
        compiler_params=pltpu.CompilerParams(
            dimension_semantics=("arbitrary", "arbitrary"),
            vmem_limit_bytes=VMEM_LIMIT),
        name="rglru",
    )(proj, proj, conv_w, conv_b, w_ax, ba, bx, lam, w_down, w_out)


def _outproj_kernel(x_ref, hm_ref, hr_ref, wm_ref, wr_ref, g_ref, x1_ref, n2_ref):
    x1 = (x_ref[...]
          + jnp.dot(hm_ref[...], wm_ref[...], preferred_element_type=F32)
          + jnp.dot(hr_ref[...], wr_ref[...], preferred_element_type=F32))
    x1_ref[...] = x1
    n2_ref[...] = _rmsnorm(x1, g_ref[...]).astype(n2_ref.dtype)


def _outproj(x, hm, hr, w_out, g, *, tm):
    T, D = x.shape
    Wm = hm.shape[1]
    Wr = hr.shape[1]
    return pl.pallas_call(
        _outproj_kernel,
        grid=(T // tm,),
        in_specs=[
            pl.BlockSpec((tm, D), lambda m: (m, 0)),
            pl.BlockSpec((tm, Wm), lambda m: (m, 0)),
            pl.BlockSpec((tm, Wr), lambda m: (m, 0)),
            pl.BlockSpec((Wm, D), lambda m: (0, 0)),
            pl.BlockSpec((Wr, D), lambda m: (Wm // Wr, 0)),
            pl.BlockSpec((1, D), lambda m: (0, 0)),
        ],
        out_specs=[
            pl.BlockSpec((tm, D), lambda m: (m, 0)),
            pl.BlockSpec((tm, D), lambda m: (m, 0)),
        ],
        out_shape=[
            jax.ShapeDtypeStruct((T, D), F32),
            jax.ShapeDtypeStruct((T, D), BF16),
        ],
        compiler_params=pltpu.CompilerParams(
            dimension_semantics=("arbitrary",),
            vmem_limit_bytes=VMEM_LIMIT),
        name="outproj",
    )(x, hm, hr, w_out, w_out, g)


def _ffn_kernel(n2_ref, x1_ref, wg_ref, wu_ref, cw_ref, cb_ref, wd_ref, gf_ref, y_ref,
                acc_ref, tail_ref, *, tm, tiles_per_seq, final_norm):
    m = pl.program_id(0)
    f = pl.program_id(1)
    nf = pl.num_programs(1)

    n2 = n2_ref[...]
    gate = jnp.dot(n2, wg_ref[...], preferred_element_type=F32)
    up = jnp.dot(n2, wu_ref[...], preferred_element_type=F32)

    prev = jnp.where(m % tiles_per_seq == 0, 0.0, tail_ref[f])
    tail_ref[f] = gate[tm - SUBLANES:tm, :]

    row8 = lax.broadcasted_iota(jnp.int32, prev.shape, 0)
    cw = cw_ref[...]
    conv = cb_ref[...] + cw[FFN_CONV - 1:FFN_CONV, :] * gate
    for j in range(1, FFN_CONV):
        r = pltpu.roll(gate, j, axis=0)
        top = jnp.where(row8 < j, pltpu.roll(prev, j, axis=0), r[0:SUBLANES, :])
        gs = jnp.concatenate([top, r[SUBLANES:, :]], axis=0)
        conv = conv + cw[FFN_CONV - 1 - j:FFN_CONV - j, :] * gs
    hid = (conv * _sigmoid(conv)) * up
    contrib = jnp.dot(hid.astype(BF16), wd_ref[...], preferred_element_type=F32)

    @pl.when(f == 0)
    def _():
        acc_ref[...] = x1_ref[...] + contrib

    @pl.when(f > 0)
    def _():
        acc_ref[...] += contrib

    @pl.when(f == nf - 1)
    def _():
        xo = acc_ref[...]
        y_ref[...] = _rmsnorm(xo, gf_ref[...]) if final_norm else xo


def _ffn(n2, x1, w_up, conv_w, conv_b, w_down, gf, *, seq, tm, tf, final_norm):
    T, D = n2.shape
    F = w_down.shape[0]
    nf = F // tf
    kern = functools.partial(_ffn_kernel, tm=tm, tiles_per_seq=seq // tm, final_norm=final_norm)
    return pl.pallas_call(
        kern,
        grid=(T // tm, nf),
        in_specs=[
            pl.BlockSpec((tm, D), lambda m, f: (m, 0)),
            pl.BlockSpec((tm, D), lambda m, f: (m, 0)),
            pl.BlockSpec((D, tf), lambda m, f: (0, f)),
            pl.BlockSpec((D, tf), lambda m, f: (0, nf + f)),
            pl.BlockSpec((FFN_CONV, tf), lambda m, f: (0, f)),
            pl.BlockSpec((1, tf), lambda m, f: (0, f)),
            pl.BlockSpec((tf, D), lambda m, f: (f, 0)),
            pl.BlockSpec((1, D), lambda m, f: (0, 0)),
        ],
        out_specs=pl.BlockSpec((tm, D), lambda m, f: (m, 0)),
        out_shape=jax.ShapeDtypeStruct((T, D), F32),
        scratch_shapes=[
            pltpu.VMEM((tm, D), F32),
            pltpu.VMEM((nf, SUBLANES, tf), F32),
        ],
        compiler_params=pltpu.CompilerParams(
            dimension_semantics=("arbitrary", "arbitrary"),
            vmem_limit_bytes=VMEM_LIMIT),
        name="convffn",
    )(n2, x1, w_up, w_up, conv_w, conv_b, w_down, gf)


def kernel(x, norm_mix_g, w_in, b_gate_m, mlstm_norm_g, lru_conv_w, lru_conv_b, lru_wa, lru_ba,
           lru_wx, lru_bx, lru_lambda, w_out, norm_ffn_g, w_up, ffn_conv_w, ffn_conv_b, w_down,
           norm_final_g):
    B, S, D = x.shape
    T = B * S
    depth = w_in.shape[0]
    H = MLSTM_HEADS
    dh = mlstm_norm_g.shape[-1]
    d_m = H * dh
    d_r = lru_conv_w.shape[-1]
    assert d_m == d_r, "column-block addressing assumes equal head-group widths"
    n_gate = 2 * H
    gate_lo = 4 * d_m

    xf = x.reshape(T, D)
    for l in range(depth):
        w_main, w_if = _win_cast(w_in[l], gate_lo=gate_lo, n_gate=n_gate, tn=512)
        b_if = jnp.pad(b_gate_m[l].astype(F32), (0, LANES - n_gate)).reshape(1, LANES)
        proj, gates = _inproj(xf, norm_mix_g[l].reshape(1, D), w_main, w_if, b_if, tm=1024, tn=1024)

        hm, w_up_bf = _mlstm(proj, gates, mlstm_norm_g[l].reshape(H, 1, dh), w_up[l],
                             batch=B, seq=S, heads=H, dh=dh, chunk=MLSTM_CHUNK)

        w_ax = jnp.concatenate([lru_wa[l], lru_wx[l]], axis=-1).astype(BF16)
        hr, w_down_bf, w_out_bf = _rglru(
            proj, lru_conv_w[l], lru_conv_b[l].reshape(1, d_r), w_ax,
            lru_ba[l].reshape(1, d_r), lru_bx[l].reshape(1, d_r), lru_lambda[l].reshape(1, d_r),
            w_down[l], w_out[l], batch=B, seq=S, col_block=gate_lo // d_r, ts=512)

        x1, n2 = _outproj(xf, hm, hr, w_out_bf, norm_ffn_g[l].reshape(1, D), tm=512)

        xf = _ffn(n2, x1, w_up_bf, ffn_conv_w[l], ffn_conv_b[l].reshape(1, -1),
                  w_down_bf, norm_final_g.reshape(1, D), seq=S, tm=512, tf=512,
                  final_norm=(l == depth - 1))
    return xf.reshape(B, S, D)
```

```python
import functools
import math

import jax
import jax.numpy as jnp
from jax import lax
from jax.experimental import pallas as pl
from jax.experimental.pallas import tpu as pltpu

F32 = jnp.float32
BF16 = jnp.bfloat16
EPS = 1e-6
LANES = 128
SUBLANES = 8
VMEM_LIMIT = 56 * 1024 * 1024

MLSTM_HEADS = 4
LRU_BLOCKS = 8
LRU_CONV = 4
LRU_C = 8.0
FFN_CONV = 3
MLSTM_CHUNK = 256


def _sigmoid(x):
    return 1.0 / (1.0 + jnp.exp(-x))


def _log_sigmoid(x):
    return jnp.minimum(x, 0.0) - jnp.log1p(jnp.exp(-jnp.abs(x)))


def _softplus(x):
    return jnp.maximum(x, 0.0) + jnp.log1p(jnp.exp(-jnp.abs(x)))


def _gelu_tanh(x):
    c = math.sqrt(2.0 / math.pi)
    return 0.5 * x * (1.0 + jnp.tanh(c * (x + 0.044715 * (x * x * x))))


def _rmsnorm(x, g):
    return x * lax.rsqrt(jnp.mean(x * x, axis=-1, keepdims=True) + EPS) * g


def _scan_lanes(x, op, fill):
    n = x.shape[-1]
    lane = lax.broadcasted_iota(jnp.int32, x.shape, x.ndim - 1)
    d = 1
    while d < n:
        shifted = jnp.where(lane >= d, pltpu.roll(x, d, axis=x.ndim - 1), fill)
        x = op(x, shifted)
        d *= 2
    return x


def _win_cast_kernel(w_ref, nxt_ref, grow_ref, out_ref, wif_ref, *, first_shifted, shift):
    j = pl.program_id(0)

    @pl.when(j < first_shifted)
    def _():
        out_ref[...] = w_ref[...].astype(BF16)

    @pl.when(j >= first_shifted)
    def _():
        out_ref[...] = jnp.concatenate([w_ref[shift:, :], nxt_ref[...]], axis=0).astype(BF16)

    wif_ref[...] = grow_ref[...].astype(BF16)


def _win_cast(wt, *, gate_lo, n_gate, tn):
    P, D = wt.shape
    N = P - n_gate
    assert gate_lo % tn == 0 and N % tn == 0 and tn % n_gate == 0 and n_gate == SUBLANES
    first_shifted = gate_lo // tn
    per = tn // n_gate
    kern = functools.partial(_win_cast_kernel, first_shifted=first_shifted, shift=n_gate)
    return pl.pallas_call(
        kern,
        grid=(N // tn,),
        in_specs=[
            pl.BlockSpec((tn, D), lambda j: (j, 0)),
            pl.BlockSpec((n_gate, D), lambda j: (jnp.maximum(j, first_shifted) * per + per, 0)),
            pl.BlockSpec((LANES, D), lambda j: (gate_lo // LANES, 0)),
        ],
        out_specs=[
            pl.BlockSpec((tn, D), lambda j: (j, 0)),
            pl.BlockSpec((LANES, D), lambda j: (0, 0)),
        ],
        out_shape=[
            jax.ShapeDtypeStruct((N, D), BF16),
            jax.ShapeDtypeStruct((LANES, D), BF16),
        ],
        compiler_params=pltpu.CompilerParams(
            dimension_semantics=("arbitrary",),
            vmem_limit_bytes=VMEM_LIMIT),
        name="win_cast",
    )(wt, wt, wt)


_NT = (((1,), (1,)), ((), ()))


def _inproj_kernel(x_ref, g_ref, w_ref, wif_ref, bif_ref, proj_ref, gate_ref, n1_ref):
    @pl.when(pl.program_id(1) == 0)
    def _():
        n1 = _rmsnorm(x_ref[...], g_ref[...]).astype(BF16)
        n1_ref[...] = n1
        gate_ref[...] = lax.dot_general(n1, wif_ref[...], _NT,
                                        preferred_element_type=F32) + bif_ref[...]

    proj_ref[...] = lax.dot_general(n1_ref[...], w_ref[...], _NT,
                                    preferred_element_type=F32).astype(proj_ref.dtype)


def _inproj(x, g, w_main_t, w_if_t, b_if, *, tm, tn):
    T, D = x.shape
    N = w_main_t.shape[0]
    return pl.pallas_call(
        _inproj_kernel,
        grid=(T // tm, N // tn),
        in_specs=[
            pl.BlockSpec((tm, D), lambda m, n: (m, 0)),
            pl.BlockSpec((1, D), lambda m, n: (0, 0)),
            pl.BlockSpec((tn, D), lambda m, n: (n, 0)),
            pl.BlockSpec((LANES, D), lambda m, n: (0, 0)),
            pl.BlockSpec((1, LANES), lambda m, n: (0, 0)),
        ],
        out_specs=[
            pl.BlockSpec((tm, tn), lambda m, n: (m, n)),
            pl.BlockSpec((tm, LANES), lambda m, n: (m, 0)),
        ],
        out_shape=[
            jax.ShapeDtypeStruct((T, N), BF16),
            jax.ShapeDtypeStruct((T, LANES), F32),
        ],
        scratch_shapes=[pltpu.VMEM((tm, D), BF16)],
        compiler_params=pltpu.CompilerParams(
            dimension_semantics=("arbitrary", "arbitrary"),
            vmem_limit_bytes=VMEM_LIMIT),
        name="inproj",
    )(x, g, w_main_t, w_if_t, b_if)


def _mlstm_kernel(q_ref, k_ref, v_ref, o_ref, gate_ref, hg_ref, wup_ref, out_ref, wup_bf_ref,
                  c_ref, n_ref, m_ref, *, heads, dh, chunk):
    L = chunk
    scale = dh ** -0.5

    wup_bf_ref[...] = wup_ref[...].astype(BF16)

    @pl.when(pl.program_id(1) == 0)
    def _():
        c_ref[...] = jnp.zeros_like(c_ref)
        n_ref[...] = jnp.zeros_like(n_ref)
        m_ref[...] = jnp.zeros_like(m_ref)

    g8 = gate_ref[...].T[0:SUBLANES, :]
    g8r = pltpu.roll(g8, heads, axis=0)
    sub = lax.broadcasted_iota(jnp.int32, g8.shape, 0)
    ig = jnp.where(sub < heads, g8, g8r)
    fg = jnp.where(sub < heads, g8r, g8)
    b = _scan_lanes(_log_sigmoid(fg), jnp.add, 0.0)
    a_row = ig - b
    cm = _scan_lanes(a_row, jnp.maximum, -jnp.inf)
    b_tot = b[:, L - 1:L]
    m_prev = m_ref[:, 0:1]
    m_inter = b + m_prev
    m_t = jnp.maximum(m_inter, b + cm)
    e_inter = jnp.exp(m_inter - m_t)
    floor = jnp.exp(-m_t)
    m_new = jnp.maximum(b_tot + m_prev, b_tot + cm[:, L - 1:L])
    decay = jnp.exp(b_tot + m_prev - m_new)
    e_end = jnp.exp(b_tot + a_row - m_new)
    m_ref[...] = jnp.broadcast_to(m_new, m_ref.shape)

    packed = jnp.concatenate(
        [b - m_t, e_inter, floor, e_end,
         jnp.zeros((LANES - 4 * SUBLANES, L), F32)], axis=0)
    cols = packed.T
    e_end_b = e_end.astype(BF16)

    row_i = lax.broadcasted_iota(jnp.int32, (L, L), 0)
    col_i = lax.broadcasted_iota(jnp.int32, (L, L), 1)
    causal = row_i >= col_i

    for h in range(heads):
        sl = slice(h * dh, (h + 1) * dh)
        q = q_ref[:, sl]
        k = k_ref[:, sl]
        v = v_ref[:, sl]
        bm_col = cols[:, h:h + 1]
        einter_col = cols[:, SUBLANES + h:SUBLANES + h + 1]
        floor_col = cols[:, 2 * SUBLANES + h:2 * SUBLANES + h + 1]
        eend_col = cols[:, 3 * SUBLANES + h:3 * SUBLANES + h + 1]

        s = lax.dot_general(q, k, (((1,), (1,)), ((), ())), preferred_element_type=F32)
        dmat = jnp.where(causal, bm_col + a_row[h:h + 1, :], -jnp.inf)
        w = (s * scale) * jnp.exp(dmat)
        c_prev = c_ref[h]
        n_prev = n_ref[h]
        inter = jnp.dot(q, c_prev.astype(BF16), preferred_element_type=F32) * scale
        num = jnp.dot(w.astype(BF16), v, preferred_element_type=F32) + einter_col * inter
        qn = jnp.sum(q.astype(F32) * n_prev, axis=-1, keepdims=True) * scale
        den = jnp.sum(w, axis=-1, keepdims=True) + einter_col * qn
        hh = num / jnp.maximum(jnp.abs(den), floor_col)
        hh = _rmsnorm(hh, hg_ref[h])
        out_ref[:, sl] = (_sigmoid(o_ref[:, sl].astype(F32)) * hh).astype(out_ref.dtype)

        ev = (eend_col * v.astype(F32)).astype(BF16)
        c_loc = lax.dot_general(k, ev, (((0,), (0,)), ((), ())), preferred_element_type=F32)
        n_loc = jnp.dot(e_end_b, k, preferred_element_type=F32)[h:h + 1, :]
        dec = decay[h:h + 1, :]
        c_ref[h] = dec * c_prev + c_loc
        n_ref[h] = dec * n_prev + n_loc


def _mlstm(proj, gates, head_g, w_up, *, batch, seq, heads, dh, chunk):
    T = batch * seq
    W = heads * dh
    nc = seq // chunk
    steps = batch * nc
    up_rows = w_up.shape[0] // steps
    assert up_rows * steps == w_up.shape[0] and up_rows % 16 == 0
    up_spec = pl.BlockSpec((up_rows, w_up.shape[1]), lambda b, c: (b * nc + c, 0))
    blk = lambda j: pl.BlockSpec((chunk, W), lambda b, c, j=j: (b * nc + c, j))
    kern = functools.partial(_mlstm_kernel, heads=heads, dh=dh, chunk=chunk)
    return pl.pallas_call(
        kern,
        grid=(batch, nc),
        in_specs=[
            blk(0), blk(1), blk(2), blk(3),
            pl.BlockSpec((chunk, LANES), lambda b, c: (b * nc + c, 0)),
            pl.BlockSpec((heads, 1, dh), lambda b, c: (0, 0, 0)),
            up_spec,
        ],
        out_specs=[pl.BlockSpec((chunk, W), lambda b, c: (b * nc + c, 0)), up_spec],
        out_shape=[jax.ShapeDtypeStruct((T, W), BF16), jax.ShapeDtypeStruct(w_up.shape, BF16)],
        scratch_shapes=[
            pltpu.VMEM((heads, dh, dh), F32),
            pltpu.VMEM((heads, 1, dh), F32),
            pltpu.VMEM((SUBLANES, LANES), F32),
        ],
        compiler_params=pltpu.CompilerParams(
            dimension_semantics=("arbitrary", "arbitrary"),
            vmem_limit_bytes=VMEM_LIMIT),
        name="mlstm",
    )(proj, proj, proj, proj, gates, head_g, w_up)


def _lru_kernel(xr_ref, gr_ref, cw_ref, cb_ref, wax_ref, ba_ref, bx_ref, lam_ref, wdn_ref, wout_ref,
                out_ref, wdn_bf_ref, wout_bf_ref,
                xtail_ref, hc_ref, a_scr, u_scr, *, ts, blocks, bd):
    C = blocks * bd

    wdn_bf_ref[...] = wdn_ref[...].astype(BF16)
    wout_bf_ref[...] = wout_ref[...].astype(BF16)

    @pl.when(pl.program_id(1) == 0)
    def _():
        xtail_ref[...] = jnp.zeros_like(xtail_ref)
        hc_ref[...] = jnp.zeros_like(hc_ref)

    x = xr_ref[...].astype(F32)
    prev = xtail_ref[...]
    row8 = lax.broadcasted_iota(jnp.int32, (SUBLANES, C), 0)
    cw = cw_ref[...]
    xc = cb_ref[...] + cw[LRU_CONV - 1:LRU_CONV, :] * x
    for j in range(1, LRU_CONV):
        r = pltpu.roll(x, j, axis=0)
        top = jnp.where(row8 < j, pltpu.roll(prev, j, axis=0), r[0:SUBLANES, :])
        xs = jnp.concatenate([top, r[SUBLANES:, :]], axis=0)
        xc = xc + cw[LRU_CONV - 1 - j:LRU_CONV - j, :] * xs
    xtail_ref[...] = x[ts - SUBLANES:ts, :]

    xcb = xc.astype(BF16)
    sp = _softplus(-lam_ref[...])
    rowm = lax.broadcasted_iota(jnp.int32, (ts, bd), 0) % SUBLANES
    for n in range(blocks):
        sl = slice(n * bd, (n + 1) * bd)
        ax = jnp.dot(xcb[:, sl], wax_ref[n], preferred_element_type=F32)
        r = _sigmoid(ax[:, :bd] + ba_ref[:, sl])
        i = _sigmoid(ax[:, bd:] + bx_ref[:, sl])
        log_a = (-LRU_C) * r * sp[:, sl]
        a = jnp.exp(log_a)
        u = jnp.sqrt(1.0 - a * a) * (i * xc[:, sl])
        d = 1
        while d < SUBLANES:
            a_sh = pltpu.roll(a, d, axis=0)
            u_sh = pltpu.roll(u, d, axis=0)
            keep = rowm >= d
            u = jnp.where(keep, a * u_sh + u, u)
            a = jnp.where(keep, a * a_sh, a)
            d *= 2
        a_scr[:, sl] = a
        u_scr[:, sl] = u

    def body(t, carry):
        r0 = pl.multiple_of(t * SUBLANES, SUBLANES)
        h = a_scr[pl.ds(r0, SUBLANES), :] * carry + u_scr[pl.ds(r0, SUBLANES), :]
        u_scr[pl.ds(r0, SUBLANES), :] = h
        return jnp.broadcast_to(h[SUBLANES - 1:SUBLANES, :], (SUBLANES, C))

    hc_ref[...] = lax.fori_loop(0, ts // SUBLANES, body, hc_ref[...], unroll=8)
    out_ref[...] = (u_scr[...] * _gelu_tanh(gr_ref[...].astype(F32))).astype(out_ref.dtype)


def _rglru(proj, conv_w, conv_b, w_ax, ba, bx, lam, w_down, w_out, *, batch, seq, col_block, ts):
    T = batch * seq
    blocks, bd, _ = w_ax.shape
    C = blocks * bd
    ns = seq // ts
    steps = batch * ns

    def slab(w):
        rows = w.shape[0] // steps
        assert rows * steps == w.shape[0] and rows % 16 == 0
        return pl.BlockSpec((rows, w.shape[1]), lambda b, s: (b * ns + s, 0))
    vec = lambda rows: pl.BlockSpec((rows, C), lambda b, s: (0, 0))
    kern = functools.partial(_lru_kernel, ts=ts, blocks=blocks, bd=bd)
    return pl.pallas_call(
        kern,
        grid=(batch, ns),
        in_specs=[
            pl.BlockSpec((ts, C), lambda b, s: (b * ns + s, col_block)),
            pl.BlockSpec((ts, C), lambda b, s: (b * ns + s, col_block + 1)),
            vec(LRU_CONV), vec(1),
            pl.BlockSpec((blocks, bd, 2 * bd), lambda b, s: (0, 0, 0)),
            vec(1), vec(1), vec(1),
            slab(w_down), slab(w_out),
        ],
        out_specs=[pl.BlockSpec((ts, C), lambda b, s: (b * ns + s, 0)), slab(w_down), slab(w_out)],
        out_shape=[jax.ShapeDtypeStruct((T, C), BF16),
                   jax.ShapeDtypeStruct(w_down.shape, BF16),
                   jax.ShapeDtypeStruct(w_out.shape, BF16)],
        scratch_shapes=[
            pltpu.VMEM((SUBLANES, C), F32),
            pltpu.VMEM((SUBLANES, C), F32),
            pltpu.VMEM((ts, C), F32),
            pltpu.VMEM((ts, C), F32),
        ],
        compiler_params=pltpu.CompilerParams(
            dimension_semantics=("arbitrary", "arbitrary"),
            vmem_limit_bytes=VMEM_LIMIT),
        name="rglru",
    )(proj, proj, conv_w, conv_b, w_ax, ba, bx, lam, w_down, w_out)


def _outproj_kernel(x_ref, hm_ref, hr_ref, wm_ref, wr_ref, g_ref, x1_ref, n2_ref):
    x1 = (x_ref[...]
          + jnp.dot(hm_ref[...], wm_ref[...], preferred_element_type=F32)
          + jnp.dot(hr_ref[...], wr_ref[...], preferred_element_type=F32))
    x1_ref[...] = x1
    n2_ref[...] = _rmsnorm(x1, g_ref[...]).astype(n2_ref.dtype)


def _outproj(x, hm, hr, w_out, g, *, tm):
    T, D = x.shape
    Wm = hm.shape[1]
    Wr = hr.shape[1]
    return pl.pallas_call(
        _outproj_kernel,
        grid=(T // tm,),
        in_specs=[
            pl.BlockSpec((tm, D), lambda m: (m, 0)),
            pl.BlockSpec((tm, Wm), lambda m: (m, 0)),
            pl.BlockSpec((tm, Wr), lambda m: (m, 0)),
            pl.BlockSpec((Wm, D), lambda m: (0, 0)),
            pl.BlockSpec((Wr, D), lambda m: (Wm // Wr, 0)),
            pl.BlockSpec((1, D), lambda m: (0, 0)),
        ],
        out_specs=[
            pl.BlockSpec((tm, D), lambda m: (m, 0)),
            pl.BlockSpec((tm, D), lambda m: (m, 0)),
        ],
        out_shape=[
            jax.ShapeDtypeStruct((T, D), F32),
            jax.ShapeDtypeStruct((T, D), BF16),
        ],
        compiler_params=pltpu.CompilerParams(
            dimension_semantics=("arbitrary",),
            vmem_limit_bytes=VMEM_LIMIT),
        name="outproj",
    )(x, hm, hr, w_out, w_out, g)


def _ffn_kernel(n2_ref, x1_ref, wg_ref, wu_ref, cw_ref, cb_ref, wd_ref, gf_ref, y_ref,
                acc_ref, tail_ref, *, tm, tiles_per_seq, final_norm):
    m = pl.program_id(0)
    f = pl.program_id(1)
    nf = pl.num_programs(1)

    n2 = n2_ref[...]
    gate = jnp.dot(n2, wg_ref[...], preferred_element_type=F32)
    up = jnp.dot(n2, wu_ref[...], preferred_element_type=F32)

    prev = jnp.where(m % tiles_per_seq == 0, 0.0, tail_ref[f])
    tail_ref[f] = gate[tm - SUBLANES:tm, :]

    row8 = lax.broadcasted_iota(jnp.int32, prev.shape, 0)
    cw = cw_ref[...]
    conv = cb_ref[...] + cw[FFN_CONV - 1:FFN_CONV, :] * gate
    for j in range(1, FFN_CONV):
        r = pltpu.roll(gate, j, axis=0)
        top = jnp.where(row8 < j, pltpu.roll(prev, j, axis=0), r[0:SUBLANES, :])
        gs = jnp.concatenate([top, r[SUBLANES:, :]], axis=0)
        conv = conv + cw[FFN_CONV - 1 - j:FFN_CONV - j, :] * gs
    hid = (conv * _sigmoid(conv)) * up
    contrib = jnp.dot(hid.astype(BF16), wd_ref[...], preferred_element_type=F32)

    @pl.when(f == 0)
    def _():
        acc_ref[...] = x1_ref[...] + contrib

    @pl.when(f > 0)
    def _():
        acc_ref[...] += contrib

    @pl.when(f == nf - 1)
    def _():
        xo = acc_ref[...]
        y_ref[...] = _rmsnorm(xo, gf_ref[...]) if final_norm else xo


def _ffn(n2, x1, w_up, conv_w, conv_b, w_down, gf, *, seq, tm, tf, final_norm):
    T, D = n2.shape
    F = w_down.shape[0]
    nf = F // tf
    kern = functools.partial(_ffn_kernel, tm=tm, tiles_per_seq=seq // tm, final_norm=final_norm)
    return pl.pallas_call(
        kern,
        grid=(T // tm, nf),
        in_specs=[
            pl.BlockSpec((tm, D), lambda m, f: (m, 0)),
            pl.BlockSpec((tm, D), lambda m, f: (m, 0)),
            pl.BlockSpec((D, tf), lambda m, f: (0, f)),
            pl.BlockSpec((D, tf), lambda m, f: (0, nf + f)),
            pl.BlockSpec((FFN_CONV, tf), lambda m, f: (0, f)),
            pl.BlockSpec((1, tf), lambda m, f: (0, f)),
            pl.BlockSpec((tf, D), lambda m, f: (f, 0)),
            pl.BlockSpec((1, D), lambda m, f: (0, 0)),
        ],
        out_specs=pl.BlockSpec((tm, D), lambda m, f: (m, 0)),
        out_shape=jax.ShapeDtypeStruct((T, D), F32),
        scratch_shapes=[
            pltpu.VMEM((tm, D), F32),
            pltpu.VMEM((nf, SUBLANES, tf), F32),
        ],
        compiler_params=pltpu.CompilerParams(
            dimension_semantics=("arbitrary", "arbitrary"),
            vmem_limit_bytes=VMEM_LIMIT),
        name="convffn",
    )(n2, x1, w_up, w_up, conv_w, conv_b, w_down, gf)


def kernel(x, norm_mix_g, w_in, b_gate_m, mlstm_norm_g, lru_conv_w, lru_conv_b, lru_wa, lru_ba,
           lru_wx, lru_bx, lru_lambda, w_out, norm_ffn_g, w_up, ffn_conv_w, ffn_conv_b, w_down,
           norm_final_g):
    B, S, D = x.shape
    T = B * S
    depth = w_in.shape[0]
    H = MLSTM_HEADS
    dh = mlstm_norm_g.shape[-1]
    d_m = H * dh
    d_r = lru_conv_w.shape[-1]
    assert d_m == d_r, "column-block addressing assumes equal head-group widths"
    n_gate = 2 * H
    gate_lo = 4 * d_m

    xf = x.reshape(T, D)
    for l in range(depth):
        w_main, w_if = _win_cast(jnp.swapaxes(w_in[l], 0, 1), gate_lo=gate_lo, n_gate=n_gate, tn=512)
        b_if = jnp.pad(b_gate_m[l].astype(F32), (0, LANES - n_gate)).reshape(1, LANES)
        proj, gates = _inproj(xf, norm_mix_g[l].reshape(1, D), w_main, w_if, b_if, tm=1024, tn=1024)

        hm, w_up_bf = _mlstm(proj, gates, mlstm_norm_g[l].reshape(H, 1, dh), w_up[l],
                             batch=B, seq=S, heads=H, dh=dh, chunk=MLSTM_CHUNK)

        w_ax = jnp.concatenate([lru_wa[l], lru_wx[l]], axis=-1).astype(BF16)
        hr, w_down_bf, w_out_bf = _rglru(
            proj, lru_conv_w[l], lru_conv_b[l].reshape(1, d_r), w_ax,
            lru_ba[l].reshape(1, d_r), lru_bx[l].reshape(1, d_r), lru_lambda[l].reshape(1, d_r),
            w_down[l], w_out[l], batch=B, seq=S, col_block=gate_lo // d_r, ts=512)

        x1, n2 = _outproj(xf, hm, hr, w_out_bf, norm_ffn_g[l].reshape(1, D), tm=512)

        xf = _ffn(n2, x1, w_up_bf, ffn_conv_w[l], ffn_conv_b[l].reshape(1, -1),
                  w_down_bf, norm_final_g.reshape(1, D), seq=S, tm=512, tf=512,
                  final_norm=(l == depth - 1))
    return xf.reshape(B, S, D)
```

```python
import functools
import math

import jax
import jax.numpy as jnp
from jax import lax
from jax.experimental import pallas as pl
from jax.experimental.pallas import tpu as pltpu

F32 = jnp.float32
BF16 = jnp.bfloat16
EPS = 1e-6
LANES = 128
SUBLANES = 8
VMEM_LIMIT = 56 * 1024 * 1024

MLSTM_HEADS = 4
LRU_BLOCKS = 8
LRU_CONV = 4
LRU_C = 8.0
FFN_CONV = 3
MLSTM_CHUNK = 256


def _sigmoid(x):
    return 1.0 / (1.0 + jnp.exp(-x))


def _log_sigmoid(x):
    return jnp.minimum(x, 0.0) - jnp.log1p(jnp.exp(-jnp.abs(x)))


def _softplus(x):
    return jnp.maximum(x, 0.0) + jnp.log1p(jnp.exp(-jnp.abs(x)))


def _gelu_tanh(x):
    c = math.sqrt(2.0 / math.pi)
    return 0.5 * x * (1.0 + jnp.tanh(c * (x + 0.044715 * (x * x * x))))


def _rmsnorm(x, g):
    return x * lax.rsqrt(jnp.mean(x * x, axis=-1, keepdims=True) + EPS) * g


def _scan_lanes(x, op, fill):
    n = x.shape[-1]
    lane = lax.broadcasted_iota(jnp.int32, x.shape, x.ndim - 1)
    d = 1
    while d < n:
        shifted = jnp.where(lane >= d, pltpu.roll(x, d, axis=x.ndim - 1), fill)
        x = op(x, shifted)
        d *= 2
    return x


def _win_cast_kernel(w_ref, nxt_ref, grow_ref, out_ref, wif_ref, *, first_shifted, shift):
    j = pl.program_id(0)

    @pl.when(j < first_shifted)
    def _():
        out_ref[...] = w_ref[...].astype(BF16)

    @pl.when(j >= first_shifted)
    def _():
        out_ref[...] = jnp.concatenate([w_ref[shift:, :], nxt_ref[...]], axis=0).astype(BF16)

    wif_ref[...] = grow_ref[...].astype(BF16)


def _win_cast(wt, *, gate_lo, n_gate, tn):
    P, D = wt.shape
    N = P - n_gate
    assert gate_lo % tn == 0 and N % tn == 0 and tn % n_gate == 0 and n_gate == SUBLANES
    first_shifted = gate_lo // tn
    per = tn // n_gate
    kern = functools.partial(_win_cast_kernel, first_shifted=first_shifted, shift=n_gate)
    return pl.pallas_call(
        kern,
        grid=(N // tn,),
        in_specs=[
            pl.BlockSpec((tn, D), lambda j: (j, 0)),
            pl.BlockSpec((n_gate, D), lambda j: (jnp.maximum(j, first_shifted) * per + per, 0)),
            pl.BlockSpec((LANES, D), lambda j: (gate_lo // LANES, 0)),
        ],
        out_specs=[
            pl.BlockSpec((tn, D), lambda j: (j, 0)),
            pl.BlockSpec((LANES, D), lambda j: (0, 0)),
        ],
        out_shape=[
            jax.ShapeDtypeStruct((N, D), BF16),
            jax.ShapeDtypeStruct((LANES, D), BF16),
        ],
        compiler_params=pltpu.CompilerParams(
            dimension_semantics=("arbitrary",),
            vmem_limit_bytes=VMEM_LIMIT),
        name="win_cast",
    )(wt, wt, wt)


_NT = (((1,), (1,)), ((), ()))


def _inproj_kernel(x_ref, g_ref, w_ref, wif_ref, bif_ref, proj_ref, gate_ref, n1_ref):
    @pl.when(pl.program_id(1) == 0)
    def _():
        n1 = _rmsnorm(x_ref[...], g_ref[...]).astype(BF16)
        n1_ref[...] = n1
        gate_ref[...] = lax.dot_general(n1, wif_ref[...], _NT,
                                        preferred_element_type=F32) + bif_ref[...]

    proj_ref[...] = lax.dot_general(n1_ref[...], w_ref[...], _NT,
                                    preferred_element_type=F32).astype(proj_ref.dtype)


def _inproj(x, g, w_main_t, w_if_t, b_if, *, tm, tn):
    T, D = x.shape
    N = w_main_t.shape[0]
    return pl.pallas_call(
        _inproj_kernel,
        grid=(T // tm, N // tn),
        in_specs=[
            pl.BlockSpec((tm, D), lambda m, n: (m, 0)),
            pl.BlockSpec((1, D), lambda m, n: (0, 0)),
            pl.BlockSpec((tn, D), lambda m, n: (n, 0)),
            pl.BlockSpec((LANES, D), lambda m, n: (0, 0)),
            pl.BlockSpec((1, LANES), lambda m, n: (0, 0)),
        ],
        out_specs=[
            pl.BlockSpec((tm, tn), lambda m, n: (m, n)),
            pl.BlockSpec((tm, LANES), lambda m, n: (m, 0)),
        ],
        out_shape=[
            jax.ShapeDtypeStruct((T, N), BF16),
            jax.ShapeDtypeStruct((T, LANES), F32),
        ],
        scratch_shapes=[pltpu.VMEM((tm, D), BF16)],
        compiler_params=pltpu.CompilerParams(
            dimension_semantics=("arbitrary", "arbitrary"),
            vmem_limit_bytes=VMEM_LIMIT),
        name="inproj",
    )(x, g, w_main_t, w_if_t, b_if)


def _mlstm_kernel(q_ref, k_ref, v_ref, o_ref, gate_ref, hg_ref, wup_ref, out_ref, wup_bf_ref,
                  c_ref, n_ref, m_ref, *, heads, dh, chunk):
    L = chunk
    scale = dh ** -0.5

    wup_bf_ref[...] = wup_ref[...].astype(BF16)

    @pl.when(pl.program_id(1) == 0)
    def _():
        c_ref[...] = jnp.zeros_like(c_ref)
        n_ref[...] = jnp.zeros_like(n_ref)
        m_ref[...] = jnp.zeros_like(m_ref)

    g8 = gate_ref[...].T[0:SUBLANES, :]
    g8r = pltpu.roll(g8, heads, axis=0)
    sub = lax.broadcasted_iota(jnp.int32, g8.shape, 0)
    ig = jnp.where(sub < heads, g8, g8r)
    fg = jnp.where(sub < heads, g8r, g8)
    b = _scan_lanes(_log_sigmoid(fg), jnp.add, 0.0)
    a_row = ig - b
    cm = _scan_lanes(a_row, jnp.maximum, -jnp.inf)
    b_tot = b[:, L - 1:L]
    m_prev = m_ref[:, 0:1]
    m_inter = b + m_prev
    m_t = jnp.maximum(m_inter, b + cm)
    e_inter = jnp.exp(m_inter - m_t)
    floor = jnp.exp(-m_t)
    m_new = jnp.maximum(b_tot + m_prev, b_tot + cm[:, L - 1:L])
    decay = jnp.exp(b_tot + m_prev - m_new)
    e_end = jnp.exp(b_tot + a_row - m_new)
    m_ref[...] = jnp.broadcast_to(m_new, m_ref.shape)

    packed = jnp.concatenate(
        [b - m_t, e_inter, floor, e_end,
         jnp.zeros((LANES - 4 * SUBLANES, L), F32)], axis=0)
    cols = packed.T
    e_end_b = e_end.astype(BF16)

    row_i = lax.broadcasted_iota(jnp.int32, (L, L), 0)
    col_i = lax.broadcasted_iota(jnp.int32, (L, L), 1)
    causal = row_i >= col_i

    for h in range(heads):
        sl = slice(h * dh, (h + 1) * dh)
        q = q_ref[:, sl]
        k = k_ref[:, sl]
        v = v_ref[:, sl]
        bm_col = cols[:, h:h + 1]
        einter_col = cols[:, SUBLANES + h:SUBLANES + h + 1]
        floor_col = cols[:, 2 * SUBLANES + h:2 * SUBLANES + h + 1]
        eend_col = cols[:, 3 * SUBLANES + h:3 * SUBLANES + h + 1]

        s = lax.dot_general(q, k, (((1,), (1,)), ((), ())), preferred_element_type=F32)
        dmat = jnp.where(causal, bm_col + a_row[h:h + 1, :], -jnp.inf)
        w = (s * scale) * jnp.exp(dmat)
        c_prev = c_ref[h]
        n_prev = n_ref[h]
        inter = jnp.dot(q, c_prev.astype(BF16), preferred_element_type=F32) * scale
        num = jnp.dot(w.astype(BF16), v, preferred_element_type=F32) + einter_col * inter
        qn = jnp.sum(q.astype(F32) * n_prev, axis=-1, keepdims=True) * scale
        den = jnp.sum(w, axis=-1, keepdims=True) + einter_col * qn
        hh = num / jnp.maximum(jnp.abs(den), floor_col)
        hh = _rmsnorm(hh, hg_ref[h])
        out_ref[:, sl] = (_sigmoid(o_ref[:, sl].astype(F32)) * hh).astype(out_ref.dtype)

        ev = (eend_col * v.astype(F32)).astype(BF16)
        c_loc = lax.dot_general(k, ev, (((0,), (0,)), ((), ())), preferred_element_type=F32)
        n_loc = jnp.dot(e_end_b, k, preferred_element_type=F32)[h:h + 1, :]
        dec = decay[h:h + 1, :]
        c_ref[h] = dec * c_prev + c_loc
        n_ref[h] = dec * n_prev + n_loc


def _mlstm(proj, gates, head_g, w_up, *, batch, seq, heads, dh, chunk):
    T = batch * seq
    W = heads * dh
    nc = seq // chunk
    steps = batch * nc
    up_rows = w_up.shape[0] // steps
    assert up_rows * steps == w_up.shape[0] and up_rows % 16 == 0
    up_spec = pl.BlockSpec((up_rows, w_up.shape[1]), lambda b, c: (b * nc + c, 0))
    blk = lambda j: pl.BlockSpec((chunk, W), lambda b, c, j=j: (b * nc + c, j))
    kern = functools.partial(_mlstm_kernel, heads=heads, dh=dh, chunk=chunk)
    return pl.pallas_call(
        kern,
        grid=(batch, nc),
        in_specs=[
            blk(0), blk(1), blk(2), blk(3),
            pl.BlockSpec((chunk, LANES), lambda b, c: (b * nc + c, 0)),
            pl.BlockSpec((heads, 1, dh), lambda b, c: (0, 0, 0)),
            up_spec,
        ],
        out_specs=[pl.BlockSpec((chunk, W), lambda b, c: (b * nc + c, 0)), up_spec],
        out_shape=[jax.ShapeDtypeStruct((T, W), BF16), jax.ShapeDtypeStruct(w_up.shape, BF16)],
        scratch_shapes=[
            pltpu.VMEM((heads, dh, dh), F32),
            pltpu.VMEM((heads, 1, dh), F32),
            pltpu.VMEM((SUBLANES, LANES), F32),
        ],
        compiler_params=pltpu.CompilerParams(
            dimension_semantics=("arbitrary", "arbitrary"),
            vmem_limit_bytes=VMEM_LIMIT),
        name="mlstm",
    )(proj, proj, proj, proj, gates, head_g, w_up)


def _lru_kernel(xr_ref, gr_ref, cw_ref, cb_ref, wax_ref, ba_ref, bx_ref, lam_ref, wdn_ref, wout_ref,
                out_ref, wdn_bf_ref, wout_bf_ref,
                xtail_ref, hc_ref, a_scr, u_scr, *, ts, blocks, bd):
    C = blocks * bd

    wdn_bf_ref[...] = wdn_ref[...].astype(BF16)
    wout_bf_ref[...] = wout_ref[...].astype(BF16)

    @pl.when(pl.program_id(1) == 0)
    def _():
        xtail_ref[...] = jnp.zeros_like(xtail_ref)
        hc_ref[...] = jnp.zeros_like(hc_ref)

    x = xr_ref[...].astype(F32)
    prev = xtail_ref[...]
    row8 = lax.broadcasted_iota(jnp.int32, (SUBLANES, C), 0)
    cw = cw_ref[...]
    xc = cb_ref[...] + cw[LRU_CONV - 1:LRU_CONV, :] * x
    for j in range(1, LRU_CONV):
        r = pltpu.roll(x, j, axis=0)
        top = jnp.where(row8 < j, pltpu.roll(prev, j, axis=0), r[0:SUBLANES, :])
        xs = jnp.concatenate([top, r[SUBLANES:, :]], axis=0)
        xc = xc + cw[LRU_CONV - 1 - j:LRU_CONV - j, :] * xs
    xtail_ref[...] = x[ts - SUBLANES:ts, :]

    xcb = xc.astype(BF16)
    sp = _softplus(-lam_ref[...])
    rowm = lax.broadcasted_iota(jnp.int32, (ts, bd), 0) % SUBLANES
    for n in range(blocks):
        sl = slice(n * bd, (n + 1) * bd)
        ax = jnp.dot(xcb[:, sl], wax_ref[n], preferred_element_type=F32)
        r = _sigmoid(ax[:, :bd] + ba_ref[:, sl])
        i = _sigmoid(ax[:, bd:] + bx_ref[:, sl])
        log_a = (-LRU_C) * r * sp[:, sl]
        a = jnp.exp(log_a)
        u = jnp.sqrt(1.0 - a * a) * (i * xc[:, sl])
        d = 1
        while d < SUBLANES:
            a_sh = pltpu.roll(a, d, axis=0)
            u_sh = pltpu.roll(u, d, axis=0)
            keep = rowm >= d
            u = jnp.where(keep, a * u_sh + u, u)
            a = jnp.where(keep, a * a_sh, a)
            d *= 2
        a_scr[:, sl] = a
        u_scr[:, sl] = u

    def body(t, carry):
        r0 = pl.multiple_of(t * SUBLANES, SUBLANES)
        h = a_scr[pl.ds(r0, SUBLANES), :] * carry + u_scr[pl.ds(r0, SUBLANES), :]
        u_scr[pl.ds(r0, SUBLANES), :] = h
        return jnp.broadcast_to(h[SUBLANES - 1:SUBLANES, :], (SUBLANES, C))

    hc_ref[...] = lax.fori_loop(0, ts // SUBLANES, body, hc_ref[...], unroll=8)
    out_ref[...] = (u_scr[...] * _gelu_tanh(gr_ref[...].astype(F32))).astype(out_ref.dtype)


def _rglru(proj, conv_w, conv_b, w_ax, ba, bx, lam, w_down, w_out, *, batch, seq, col_block, ts):
    T = batch * seq
    blocks, bd, _ = w_ax.shape
    C = blocks * bd
    ns = seq // ts
    steps = batch * ns

    def slab(w):
        rows = w.shape[0] // steps
        assert rows * steps == w.shape[0] and rows % 16 == 0
        return pl.BlockSpec((rows, w.shape[1]), lambda b, s: (b * ns + s, 0))
    vec = lambda rows: pl.BlockSpec((rows, C), lambda b, s: (0, 0))
    kern = functools.partial(_lru_kernel, ts=ts, blocks=blocks, bd=bd)
    return pl.pallas_call(
        kern,
        grid=(batch, ns),
        in_specs=[
            pl.BlockSpec((ts, C), lambda b, s: (b * ns + s, col_block)),
            pl.BlockSpec((ts, C), lambda b, s: (b * ns + s, col_block + 1)),
            vec(LRU_CONV), vec(1),
            pl.BlockSpec((blocks, bd, 2 * bd), lambda b, s: (0, 0, 0)),
            vec(1), vec(1), vec(1),
            slab(w_down), slab(w_out),
        ],
        out_specs=[pl.BlockSpec((ts, C), lambda b, s: (b * ns + s, 0)), slab(w_down), slab(w_out)],
        out_shape=[jax.ShapeDtypeStruct((T, C), BF16),
                   jax.ShapeDtypeStruct(w_down.shape, BF16),
                   jax.ShapeDtypeStruct(w_out.shape, BF16)],
        scratch_shapes=[
            pltpu.VMEM((SUBLANES, C), F32),
            pltpu.VMEM((SUBLANES, C), F32),
            pltpu.VMEM((ts, C), F32),
            pltpu.VMEM((ts, C), F32),
        ],
        compiler_params=pltpu.CompilerParams(
            dimension_semantics=("arbitrary", "arbitrary"),
            vmem_limit_bytes=VMEM_LIMIT),
        name="rglru",
    )(proj, proj, conv_w, conv_b, w_ax, ba, bx, lam, w_down, w_out)


def _outproj_kernel(x_ref, hm_ref, hr_ref, wm_ref, wr_ref, g_ref, x1_ref, n2_ref):
    x1 = (x_ref[...]
          + jnp.dot(hm_ref[...], wm_ref[...], preferred_element_type=F32)
          + jnp.dot(hr_ref[...], wr_ref[...], preferred_element_type=F32))
    x1_ref[...] = x1
    n2_ref[...] = _rmsnorm(x1, g_ref[...]).astype(n2_ref.dtype)


def _outproj(x, hm, hr, w_out, g, *, tm):
    T, D = x.shape
    Wm = hm.shape[1]
    Wr = hr.shape[1]
    return pl.pallas_call(
        _outproj_kernel,
        grid=(T // tm,),
        in_specs=[
            pl.BlockSpec((tm, D), lambda m: (m, 0)),
            pl.BlockSpec((tm, Wm), lambda m: (m, 0)),
            pl.BlockSpec((tm, Wr), lambda m: (m, 0)),
            pl.BlockSpec((Wm, D), lambda m: (0, 0)),
            pl.BlockSpec((Wr, D), lambda m: (Wm // Wr, 0)),
            pl.BlockSpec((1, D), lambda m: (0, 0)),
        ],
        out_specs=[
            pl.BlockSpec((tm, D), lambda m: (m, 0)),
            pl.BlockSpec((tm, D), lambda m: (m, 0)),
        ],
        out_shape=[
            jax.ShapeDtypeStruct((T, D), F32),
            jax.ShapeDtypeStruct((T, D), BF16),
        ],
        compiler_params=pltpu.CompilerParams(
            dimension_semantics=("arbitrary",),
            vmem_limit_bytes=VMEM_LIMIT),
        name="outproj",
    )(x, hm, hr, w_out, w_out, g)


def _ffn_kernel(n2_ref, x1_ref, wg_ref, wu_ref, cw_ref, cb_ref, wd_ref, gf_ref, y_ref,
                acc_ref, tail_ref, *, tm, tiles_per_seq, final_norm):
    m = pl.program_id(0)
    f = pl.program_id(1)
    nf = pl.num_programs(1)

    @pl.when(f == 0)
    def _():
        acc_ref[...] = x1_ref[...]

    n2 = n2_ref[...]
    gate = jnp.dot(n2, wg_ref[...], preferred_element_type=F32)
    up = jnp.dot(n2, wu_ref[...], preferred_element_type=F32)

    prev = jnp.where(m % tiles_per_seq == 0, 0.0, tail_ref[f])
    tail_ref[f] = gate[tm - SUBLANES:tm, :]

    row8 = lax.broadcasted_iota(jnp.int32, prev.shape, 0)
    cw = cw_ref[...]
    conv = cb_ref[...] + cw[FFN_CONV - 1:FFN_CONV, :] * gate
    for j in range(1, FFN_CONV):
        r = pltpu.roll(gate, j, axis=0)
        top = jnp.where(row8 < j, pltpu.roll(prev, j, axis=0), r[0:SUBLANES, :])
        gs = jnp.concatenate([top, r[SUBLANES:, :]], axis=0)
        conv = conv + cw[FFN_CONV - 1 - j:FFN_CONV - j, :] * gs
    hid = (conv * _sigmoid(conv)) * up
    acc_ref[...] += jnp.dot(hid.astype(BF16), wd_ref[...], preferred_element_type=F32)

    @pl.when(f == nf - 1)
    def _():
        xo = acc_ref[...]
        y_ref[...] = _rmsnorm(xo, gf_ref[...]) if final_norm else xo


def _ffn(n2, x1, w_up, conv_w, conv_b, w_down, gf, *, seq, tm, tf, final_norm):
    T, D = n2.shape
    F = w_down.shape[0]
    nf = F // tf
    kern = functools.partial(_ffn_kernel, tm=tm, tiles_per_seq=seq // tm, final_norm=final_norm)
    return pl.pallas_call(
        kern,
        grid=(T // tm, nf),
        in_specs=[
            pl.BlockSpec((tm, D), lambda m, f: (m, 0)),
            pl.BlockSpec((tm, D), lambda m, f: (m, 0)),
            pl.BlockSpec((D, tf), lambda m, f: (0, f)),
            pl.BlockSpec((D, tf), lambda m, f: (0, nf + f)),
            pl.BlockSpec((FFN_CONV, tf), lambda m, f: (0, f)),
            pl.BlockSpec((1, tf), lambda m, f: (0, f)),
            pl.BlockSpec((tf, D), lambda m, f: (f, 0)),
            pl.BlockSpec((1, D), lambda m, f: (0, 0)),
        ],
        out_specs=pl.BlockSpec((tm, D), lambda m, f: (m, 0)),
        out_shape=jax.ShapeDtypeStruct((T, D), F32),
        scratch_shapes=[
            pltpu.VMEM((tm, D), F32),
            pltpu.VMEM((nf, SUBLANES, tf), F32),
        ],
        compiler_params=pltpu.CompilerParams(
            dimension_semantics=("arbitrary", "arbitrary"),
            vmem_limit_bytes=VMEM_LIMIT),
        name="convffn",
    )(n2, x1, w_up, w_up, conv_w, conv_b, w_down, gf)


def kernel(x, norm_mix_g, w_in, b_gate_m, mlstm_norm_g, lru_conv_w, lru_conv_b, lru_wa, lru_ba,
           lru_wx, lru_bx, lru_lambda, w_out, norm_ffn_g, w_up, ffn_conv_w, ffn_conv_b, w_down,
           norm_final_g):
    B, S, D = x.shape
    T = B * S
    depth = w_in.shape[0]
    H = MLSTM_HEADS
    dh = mlstm_norm_g.shape[-1]
    d_m = H * dh
    d_r = lru_conv_w.shape[-1]
    assert d_m == d_r, "column-block addressing assumes equal head-group widths"
    n_gate = 2 * H
    gate_lo = 4 * d_m

    xf = x.reshape(T, D)
    for l in range(depth):
        w_main, w_if = _win_cast(jnp.swapaxes(w_in[l], 0, 1), gate_lo=gate_lo, n_gate=n_gate, tn=512)
        b_if = jnp.pad(b_gate_m[l].astype(F32), (0, LANES - n_gate)).reshape(1, LANES)
        proj, gates = _inproj(xf, norm_mix_g[l].reshape(1, D), w_main, w_if, b_if, tm=1024, tn=1024)

        hm, w_up_bf = _mlstm(proj, gates, mlstm_norm_g[l].reshape(H, 1, dh), w_up[l],
                             batch=B, seq=S, heads=H, dh=dh, chunk=MLSTM_CHUNK)

        w_ax = jnp.concatenate([lru_wa[l], lru_wx[l]], axis=-1).astype(BF16)
        hr, w_down_bf, w_out_bf = _rglru(
            proj, lru_conv_w[l], lru_conv_b[l].reshape(1, d_r), w_ax,
            lru_ba[l].reshape(1, d_r), lru_bx[l].reshape(1, d_r), lru_lambda[l].reshape(1, d_r),
            w_down[l], w_out[l], batch=B, seq=S, col_block=gate_lo // d_r, ts=512)

        x1, n2 = _outproj(xf, hm, hr, w_out_bf, norm_ffn_g[l].reshape(1, D), tm=512)

        xf = _ffn(n2, x1, w_up_bf, ffn_conv_w[l], ffn_conv_b[l].reshape(1, -1),
                  w_down_bf, norm_final_g.reshape(1, D), seq=S, tm=512, tf=512,
                  final_norm=(l == depth - 1))
    return xf.reshape(B, S, D)
```

```python
import functools
import math

import jax
import jax.numpy as jnp
from jax import lax
from jax.experimental import pallas as pl
from jax.experimental.pallas import tpu as pltpu

F32 = jnp.float32
BF16 = jnp.bfloat16
EPS = 1e-6
LANES = 128
SUBLANES = 8
VMEM_LIMIT = 56 * 1024 * 1024

MLSTM_HEADS = 4
LRU_BLOCKS = 8
LRU_CONV = 4
LRU_C = 8.0
FFN_CONV = 3
MLSTM_CHUNK = 256


def _sigmoid(x):
    return 0.5 * jnp.tanh(0.5 * x) + 0.5


def _log_sigmoid(x):
    return jnp.minimum(x, 0.0) - jnp.log1p(jnp.exp(-jnp.abs(x)))


def _softplus(x):
    return jnp.maximum(x, 0.0) + jnp.log1p(jnp.exp(-jnp.abs(x)))


def _gelu_tanh(x):
    c = math.sqrt(2.0 / math.pi)
    t = jnp.tanh(x * (c + (c * 0.044715) * (x * x)))
    hx = 0.5 * x
    return hx + hx * t


def _rmsnorm(x, g):
    return x * lax.rsqrt(jnp.mean(x * x, axis=-1, keepdims=True) + EPS) * g


def _scan_lanes(x, op, fill):
    n = x.shape[-1]
    lane = lax.broadcasted_iota(jnp.int32, x.shape, x.ndim - 1)
    d = 1
    while d < n:
        shifted = jnp.where(lane >= d, pltpu.roll(x, d, axis=x.ndim - 1), fill)
        x = op(x, shifted)
        d *= 2
    return x


def _win_cast_kernel(w_ref, nxt_ref, grow_ref, out_ref, wif_ref, *, first_shifted, shift):
    j = pl.program_id(0)

    @pl.when(j < first_shifted)
    def _():
        out_ref[...] = w_ref[...].astype(BF16)

    @pl.when(j >= first_shifted)
    def _():
        out_ref[...] = jnp.concatenate([w_ref[shift:, :], nxt_ref[...]], axis=0).astype(BF16)

    wif_ref[...] = grow_ref[...].astype(BF16)


def _win_cast(wt, *, gate_lo, n_gate, tn):
    P, D = wt.shape
    N = P - n_gate
    assert gate_lo % tn == 0 and N % tn == 0 and tn % n_gate == 0 and n_gate == SUBLANES
    first_shifted = gate_lo // tn
    per = tn // n_gate
    kern = functools.partial(_win_cast_kernel, first_shifted=first_shifted, shift=n_gate)
    return pl.pallas_call(
        kern,
        grid=(N // tn,),
        in_specs=[
            pl.BlockSpec((tn, D), lambda j: (j, 0)),
            pl.BlockSpec((n_gate, D), lambda j: (jnp.maximum(j, first_shifted) * per + per, 0)),
            pl.BlockSpec((LANES, D), lambda j: (gate_lo // LANES, 0)),
        ],
        out_specs=[
            pl.BlockSpec((tn, D), lambda j: (j, 0)),
            pl.BlockSpec((LANES, D), lambda j: (0, 0)),
        ],
        out_shape=[
            jax.ShapeDtypeStruct((N, D), BF16),
            jax.ShapeDtypeStruct((LANES, D), BF16),
        ],
        compiler_params=pltpu.CompilerParams(
            dimension_semantics=("arbitrary",),
            vmem_limit_bytes=VMEM_LIMIT),
        name="win_cast",
    )(wt, wt, wt)


_NT = (((1,), (1,)), ((), ()))


def _inproj_kernel(x_ref, g_ref, w_ref, wif_ref, bif_ref, proj_ref, gate_ref, n1_ref):
    @pl.when(pl.program_id(1) == 0)
    def _():
        n1 = _rmsnorm(x_ref[...], g_ref[...]).astype(BF16)
        n1_ref[...] = n1
        gate_ref[...] = lax.dot_general(n1, wif_ref[...], _NT,
                                        preferred_element_type=F32) + bif_ref[...]

    proj_ref[...] = lax.dot_general(n1_ref[...], w_ref[...], _NT,
                                    preferred_element_type=F32).astype(proj_ref.dtype)


def _inproj(x, g, w_main_t, w_if_t, b_if, *, tm, tn):
    T, D = x.shape
    N = w_main_t.shape[0]
    return pl.pallas_call(
        _inproj_kernel,
        grid=(T // tm, N // tn),
        in_specs=[
            pl.BlockSpec((tm, D), lambda m, n: (m, 0)),
            pl.BlockSpec((1, D), lambda m, n: (0, 0)),
            pl.BlockSpec((tn, D), lambda m, n: (n, 0)),
            pl.BlockSpec((LANES, D), lambda m, n: (0, 0)),
            pl.BlockSpec((1, LANES), lambda m, n: (0, 0)),
        ],
        out_specs=[
            pl.BlockSpec((tm, tn), lambda m, n: (m, n)),
            pl.BlockSpec((tm, LANES), lambda m, n: (m, 0)),
        ],
        out_shape=[
            jax.ShapeDtypeStruct((T, N), BF16),
            jax.ShapeDtypeStruct((T, LANES), F32),
        ],
        scratch_shapes=[pltpu.VMEM((tm, D), BF16)],
        compiler_params=pltpu.CompilerParams(
            dimension_semantics=("arbitrary", "arbitrary"),
            vmem_limit_bytes=VMEM_LIMIT),
        name="inproj",
    )(x, g, w_main_t, w_if_t, b_if)


def _mlstm_kernel(q_ref, k_ref, v_ref, o_ref, gate_ref, hg_ref, wup_ref, out_ref, wup_bf_ref,
                  c_ref, n_ref, m_ref, *, heads, dh, chunk):
    L = chunk
    scale = dh ** -0.5

    wup_bf_ref[...] = wup_ref[...].astype(BF16)

    @pl.when(pl.program_id(1) == 0)
    def _():
        c_ref[...] = jnp.zeros_like(c_ref)
        n_ref[...] = jnp.zeros_like(n_ref)
        m_ref[...] = jnp.zeros_like(m_ref)

    g8 = gate_ref[...].T[0:SUBLANES, :]
    g8r = pltpu.roll(g8, heads, axis=0)
    sub = lax.broadcasted_iota(jnp.int32, g8.shape, 0)
    ig = jnp.where(sub < heads, g8, g8r)
    fg = jnp.where(sub < heads, g8r, g8)
    b = _scan_lanes(_log_sigmoid(fg), jnp.add, 0.0)
    a_row = ig - b
    cm = _scan_lanes(a_row, jnp.maximum, -jnp.inf)
    b_tot = b[:, L - 1:L]
    m_prev = m_ref[:, 0:1]
    m_inter = b + m_prev
    m_t = jnp.maximum(m_inter, b + cm)
    e_inter = jnp.exp(m_inter - m_t)
    floor = jnp.exp(-m_t)
    m_new = jnp.maximum(b_tot + m_prev, b_tot + cm[:, L - 1:L])
    decay = jnp.exp(b_tot + m_prev - m_new)
    e_end = jnp.exp(b_tot + a_row - m_new)
    m_ref[...] = jnp.broadcast_to(m_new, m_ref.shape)

    packed = jnp.concatenate(
        [b - m_t, e_inter, floor, e_end,
         jnp.zeros((LANES - 4 * SUBLANES, L), F32)], axis=0)
    cols = packed.T
    e_end_b = e_end.astype(BF16)

    row_i = lax.broadcasted_iota(jnp.int32, (L, L), 0)
    col_i = lax.broadcasted_iota(jnp.int32, (L, L), 1)
    causal = row_i >= col_i

    for h in range(heads):
        sl = slice(h * dh, (h + 1) * dh)
        q = q_ref[:, sl]
        k = k_ref[:, sl]
        v = v_ref[:, sl]
        bm_col = cols[:, h:h + 1]
        einter_col = cols[:, SUBLANES + h:SUBLANES + h + 1]
        floor_col = cols[:, 2 * SUBLANES + h:2 * SUBLANES + h + 1]
        eend_col = cols[:, 3 * SUBLANES + h:3 * SUBLANES + h + 1]

        s = lax.dot_general(q, k, (((1,), (1,)), ((), ())), preferred_element_type=F32)
        dmat = jnp.where(causal, bm_col + a_row[h:h + 1, :], -jnp.inf)
        w = (s * scale) * jnp.exp(dmat)
        c_prev = c_ref[h]
        n_prev = n_ref[h]
        inter = jnp.dot(q, c_prev.astype(BF16), preferred_element_type=F32) * scale
        num = jnp.dot(w.astype(BF16), v, preferred_element_type=F32) + einter_col * inter
        qn = jnp.sum(q.astype(F32) * n_prev, axis=-1, keepdims=True) * scale
        den = jnp.sum(w, axis=-1, keepdims=True) + einter_col * qn
        hh = num / jnp.maximum(jnp.abs(den), floor_col)
        hh = _rmsnorm(hh, hg_ref[h])
        out_ref[:, sl] = (_sigmoid(o_ref[:, sl].astype(F32)) * hh).astype(out_ref.dtype)

        ev = (eend_col * v.astype(F32)).astype(BF16)
        c_loc = lax.dot_general(k, ev, (((0,), (0,)), ((), ())), preferred_element_type=F32)
        n_loc = jnp.dot(e_end_b, k, preferred_element_type=F32)[h:h + 1, :]
        dec = decay[h:h + 1, :]
        c_ref[h] = dec * c_prev + c_loc
        n_ref[h] = dec * n_prev + n_loc


def _mlstm(proj, gates, head_g, w_up, *, batch, seq, heads, dh, chunk):
    T = batch * seq
    W = heads * dh
    nc = seq // chunk
    steps = batch * nc
    up_rows = w_up.shape[0] // steps
    assert up_rows * steps == w_up.shape[0] and up_rows % 16 == 0
    up_spec = pl.BlockSpec((up_rows, w_up.shape[1]), lambda b, c: (b * nc + c, 0))
    blk = lambda j: pl.BlockSpec((chunk, W), lambda b, c, j=j: (b * nc + c, j))
    kern = functools.partial(_mlstm_kernel, heads=heads, dh=dh, chunk=chunk)
    return pl.pallas_call(
        kern,
        grid=(batch, nc),
        in_specs=[
            blk(0), blk(1), blk(2), blk(3),
            pl.BlockSpec((chunk, LANES), lambda b, c: (b * nc + c, 0)),
            pl.BlockSpec((heads, 1, dh), lambda b, c: (0, 0, 0)),
            up_spec,
        ],
        out_specs=[pl.BlockSpec((chunk, W), lambda b, c: (b * nc + c, 0)), up_spec],
        out_shape=[jax.ShapeDtypeStruct((T, W), BF16), jax.ShapeDtypeStruct(w_up.shape, BF16)],
        scratch_shapes=[
            pltpu.VMEM((heads, dh, dh), F32),
            pltpu.VMEM((heads, 1, dh), F32),
            pltpu.VMEM((SUBLANES, LANES), F32),
        ],
        compiler_params=pltpu.CompilerParams(
            dimension_semantics=("arbitrary", "arbitrary"),
            vmem_limit_bytes=VMEM_LIMIT),
        name="mlstm",
    )(proj, proj, proj, proj, gates, head_g, w_up)


def _lru_kernel(xr_ref, gr_ref, cw_ref, cb_ref, wax_ref, ba_ref, bx_ref, lam_ref, wdn_ref, wout_ref,
                out_ref, wdn_bf_ref, wout_bf_ref,
                xtail_ref, hc_ref, perm_ref, unperm_ref, a_scr, u_scr, c_scr, h_scr,
                *, ts, blocks, bd):
    C = blocks * bd
    seg = ts // SUBLANES
    halo = LRU_CONV - 1

    wdn_bf_ref[...] = wdn_ref[...].astype(BF16)
    wout_bf_ref[...] = wout_ref[...].astype(BF16)

    @pl.when(pl.program_id(1) == 0)
    def _():
        xtail_ref[...] = jnp.zeros_like(xtail_ref)
        hc_ref[...] = jnp.zeros_like(hc_ref)

    @pl.when((pl.program_id(0) == 0) & (pl.program_id(1) == 0))
    def _():
        ri = lax.broadcasted_iota(jnp.int32, (ts, ts), 0)
        ci = lax.broadcasted_iota(jnp.int32, (ts, ts), 1)
        t_of_row = (ri & (SUBLANES - 1)) * seg + (ri >> 3)
        t_of_col = (ci & (SUBLANES - 1)) * seg + (ci >> 3)
        perm_ref[...] = jnp.where(ci == t_of_row, 1.0, 0.0).astype(BF16)
        unperm_ref[...] = jnp.where(ri == t_of_col, 1.0, 0.0).astype(BF16)

    xb = xr_ref[...]
    xp = jnp.dot(perm_ref[...], xb, preferred_element_type=F32)
    gp = jnp.dot(perm_ref[...], gr_ref[...], preferred_element_type=F32)
    gelu_g = _gelu_tanh(gp)
    prev = xtail_ref[...]
    xtail_ref[...] = xb[ts - SUBLANES:ts, :].astype(F32)
    row8 = lax.broadcasted_iota(jnp.int32, (SUBLANES, C), 0)

    wrapped = []
    for k in range(1, halo + 1):
        r0 = (seg - k) * SUBLANES
        up = pltpu.roll(xp[r0:r0 + SUBLANES, :], 1, axis=0)
        wrapped.append(jnp.where(row8 == 0, prev[SUBLANES - k:SUBLANES - k + 1, :], up))

    cw = cw_ref[...]
    xc = cb_ref[...] + cw[LRU_CONV - 1:LRU_CONV, :] * xp
    for j in range(1, LRU_CONV):
        xs = jnp.concatenate([wrapped[k - 1] for k in range(j, 0, -1)]
                             + [xp[0:ts - j * SUBLANES, :]], axis=0)
        xc = xc + cw[LRU_CONV - 1 - j:LRU_CONV - j, :] * xs

    xcb = xc.astype(BF16)
    half_xc = 0.5 * xc
    k = (-0.5 * LRU_C * math.log2(math.e)) * _softplus(-lam_ref[...])
    half_ba = 0.5 * ba_ref[...]
    half_bx = 0.5 * bx_ref[...]
    for n in range(blocks):
        sl = slice(n * bd, (n + 1) * bd)
        ax = jnp.dot(xcb[:, sl], wax_ref[n], preferred_element_type=F32)
        t_r = jnp.tanh(ax[:, :bd] + half_ba[:, sl])
        t_i = jnp.tanh(ax[:, bd:] + half_bx[:, sl])
        a = jnp.exp2(k[:, sl] + k[:, sl] * t_r)
        ixc = half_xc[:, sl] + half_xc[:, sl] * t_i
        y = 1.0 - a * a
        a_scr[:, sl] = a
        u_scr[:, sl] = jnp.where(y > 0.0, y * lax.rsqrt(y), 0.0) * ixc

    def body(v, carry):
        h, c = carry
        rows = pl.ds(pl.multiple_of(v * SUBLANES, SUBLANES), SUBLANES)
        av = a_scr[rows, :]
        h = av * h + u_scr[rows, :]
        c = av * c
        h_scr[rows, :] = h
        c_scr[rows, :] = c
        return h, c

    init = (jnp.zeros((SUBLANES, C), F32), jnp.ones((SUBLANES, C), F32))
    h_end, c_end = lax.fori_loop(0, seg, body, init, unroll=4)

    state = hc_ref[0:1, :]
    entering = []
    for s in range(SUBLANES):
        entering.append(state)
        state = c_end[s:s + 1, :] * state + h_end[s:s + 1, :]
    hc_ref[0:1, :] = state
    entering = jnp.concatenate(entering, axis=0)

    h = (h_scr[...].reshape(seg, SUBLANES, C)
         + c_scr[...].reshape(seg, SUBLANES, C) * entering[None]).reshape(ts, C)
    outp = (h * gelu_g).astype(BF16)
    out_ref[...] = jnp.dot(unperm_ref[...], outp,
                           preferred_element_type=F32).astype(out_ref.dtype)


def _rglru(proj, conv_w, conv_b, w_ax, ba, bx, lam, w_down, w_out, *, batch, seq, col_block, ts):
    T = batch * seq
    blocks, bd, _ = w_ax.shape
    C = blocks * bd
    ns = seq // ts
    steps = batch * ns

    def slab(w):
        rows = w.shape[0] // steps
        assert rows * steps == w.shape[0] and rows % 16 == 0
        return pl.BlockSpec((rows, w.shape[1]), lambda b, s: (b * ns + s, 0))
    vec = lambda rows: pl.BlockSpec((rows, C), lambda b, s: (0, 0))
    assert ts % (SUBLANES * SUBLANES) == 0 and bd == LANES
    kern = functools.partial(_lru_kernel, ts=ts, blocks=blocks, bd=bd)
    return pl.pallas_call(
        kern,
        grid=(batch, ns),
        in_specs=[
            pl.BlockSpec((ts, C), lambda b, s: (b * ns + s, col_block)),
            pl.BlockSpec((ts, C), lambda b, s: (b * ns + s, col_block + 1)),
            vec(LRU_CONV), vec(1),
            pl.BlockSpec((blocks, bd, 2 * bd), lambda b, s: (0, 0, 0)),
            vec(1), vec(1), vec(1),
            slab(w_down), slab(w_out),
        ],
        out_specs=[pl.BlockSpec((ts, C), lambda b, s: (b * ns + s, 0)), slab(w_down), slab(w_out)],
        out_shape=[jax.ShapeDtypeStruct((T, C), BF16),
                   jax.ShapeDtypeStruct(w_down.shape, BF16),
                   jax.ShapeDtypeStruct(w_out.shape, BF16)],
        scratch_shapes=[
            pltpu.VMEM((SUBLANES, C), F32),
            pltpu.VMEM((SUBLANES, C), F32),
            pltpu.VMEM((ts, ts), BF16),
            pltpu.VMEM((ts, ts), BF16),
        ] + [pltpu.VMEM((ts, C), F32)] * 4,
        compiler_params=pltpu.CompilerParams(
            dimension_semantics=("arbitrary", "arbitrary"),
            vmem_limit_bytes=VMEM_LIMIT),
        name="rglru",
    )(proj, proj, conv_w, conv_b, w_ax, ba, bx, lam, w_down, w_out)


def _outproj_kernel(x_ref, hm_ref, hr_ref, wm_ref, wr_ref, g_ref, x1_ref, n2_ref):
    x1 = (x_ref[...]
          + jnp.dot(hm_ref[...], wm_ref[...], preferred_element_type=F32)
          + jnp.dot(hr_ref[...], wr_ref[...], preferred_element_type=F32))
    x1_ref[...] = x1
    n2_ref[...] = _rmsnorm(x1, g_ref[...]).astype(n2_ref.dtype)


def _outproj(x, hm, hr, w_out, g, *, tm):
    T, D = x.shape
    Wm = hm.shape[1]
    Wr = hr.shape[1]
    return pl.pallas_call(
        _outproj_kernel,
        grid=(T // tm,),
        in_specs=[
            pl.BlockSpec((tm, D), lambda m: (m, 0)),
            pl.BlockSpec((tm, Wm), lambda m: (m, 0)),
            pl.BlockSpec((tm, Wr), lambda m: (m, 0)),
            pl.BlockSpec((Wm, D), lambda m: (0, 0)),
            pl.BlockSpec((Wr, D), lambda m: (Wm // Wr, 0)),
            pl.BlockSpec((1, D), lambda m: (0, 0)),
        ],
        out_specs=[
            pl.BlockSpec((tm, D), lambda m: (m, 0)),
            pl.BlockSpec((tm, D), lambda m: (m, 0)),
        ],
        out_shape=[
            jax.ShapeDtypeStruct((T, D), F32),
            jax.ShapeDtypeStruct((T, D), BF16),
        ],
        compiler_params=pltpu.CompilerParams(
            dimension_semantics=("arbitrary",),
            vmem_limit_bytes=VMEM_LIMIT),
        name="outproj",
    )(x, hm, hr, w_out, w_out, g)


def _ffn_kernel(n2_ref, x1_ref, wg_ref, wu_ref, cw_ref, cb_ref, wd_ref, gf_ref, y_ref,
                acc_ref, tail_ref, *, tm, tiles_per_seq, final_norm):
    m = pl.program_id(0)
    f = pl.program_id(1)
    nf = pl.num_programs(1)

    @pl.when(f == 0)
    def _():
        acc_ref[...] = x1_ref[...]

    n2 = n2_ref[...]
    gate = jnp.dot(n2, wg_ref[...], preferred_element_type=F32)
    up = jnp.dot(n2, wu_ref[...], preferred_element_type=F32)

    prev = jnp.where(m % tiles_per_seq == 0, 0.0, tail_ref[f])
    tail_ref[f] = gate[tm - SUBLANES:tm, :]

    row8 = lax.broadcasted_iota(jnp.int32, prev.shape, 0)
    cw = cw_ref[...]
    conv = cb_ref[...] + cw[FFN_CONV - 1:FFN_CONV, :] * gate
    for j in range(1, FFN_CONV):
        r = pltpu.roll(gate, j, axis=0)
        top = jnp.where(row8 < j, pltpu.roll(prev, j, axis=0), r[0:SUBLANES, :])
        gs = jnp.concatenate([top, r[SUBLANES:, :]], axis=0)
        conv = conv + cw[FFN_CONV - 1 - j:FFN_CONV - j, :] * gs
    hid = (conv * _sigmoid(conv)) * up
    acc_ref[...] += jnp.dot(hid.astype(BF16), wd_ref[...], preferred_element_type=F32)

    @pl.when(f == nf - 1)
    def _():
        xo = acc_ref[...]
        y_ref[...] = _rmsnorm(xo, gf_ref[...]) if final_norm else xo


def _ffn(n2, x1, w_up, conv_w, conv_b, w_down, gf, *, seq, tm, tf, final_norm):
    T, D = n2.shape
    F = w_down.shape[0]
    nf = F // tf
    kern = functools.partial(_ffn_kernel, tm=tm, tiles_per_seq=seq // tm, final_norm=final_norm)
    return pl.pallas_call(
        kern,
        grid=(T // tm, nf),
        in_specs=[
            pl.BlockSpec((tm, D), lambda m, f: (m, 0)),
            pl.BlockSpec((tm, D), lambda m, f: (m, 0)),
            pl.BlockSpec((D, tf), lambda m, f: (0, f)),
            pl.BlockSpec((D, tf), lambda m, f: (0, nf + f)),
            pl.BlockSpec((FFN_CONV, tf), lambda m, f: (0, f)),
            pl.BlockSpec((1, tf), lambda m, f: (0, f)),
            pl.BlockSpec((tf, D), lambda m, f: (f, 0)),
            pl.BlockSpec((1, D), lambda m, f: (0, 0)),
        ],
        out_specs=pl.BlockSpec((tm, D), lambda m, f: (m, 0)),
        out_shape=jax.ShapeDtypeStruct((T, D), F32),
        scratch_shapes=[
            pltpu.VMEM((tm, D), F32),
            pltpu.VMEM((nf, SUBLANES, tf), F32),
        ],
        compiler_params=pltpu.CompilerParams(
            dimension_semantics=("arbitrary", "arbitrary"),
            vmem_limit_bytes=VMEM_LIMIT),
        name="convffn",
    )(n2, x1, w_up, w_up, conv_w, conv_b, w_down, gf)


def kernel(x, norm_mix_g, w_in, b_gate_m, mlstm_norm_g, lru_conv_w, lru_conv_b, lru_wa, lru_ba,
           lru_wx, lru_bx, lru_lambda, w_out, norm_ffn_g, w_up, ffn_conv_w, ffn_conv_b, w_down,
           norm_final_g):
    B, S, D = x.shape
    T = B * S
    depth = w_in.shape[0]
    H = MLSTM_HEADS
    dh = mlstm_norm_g.shape[-1]
    d_m = H * dh
    d_r = lru_conv_w.shape[-1]
    assert d_m == d_r, "column-block addressing assumes equal head-group widths"
    n_gate = 2 * H
    gate_lo = 4 * d_m

    xf = x.reshape(T, D)
    for l in range(depth):
        w_main, w_if = _win_cast(jnp.swapaxes(w_in[l], 0, 1), gate_lo=gate_lo, n_gate=n_gate, tn=512)
        b_if = jnp.pad(b_gate_m[l].astype(F32), (0, LANES - n_gate)).reshape(1, LANES)
        proj, gates = _inproj(xf, norm_mix_g[l].reshape(1, D), w_main, w_if, b_if, tm=1024, tn=1024)

        hm, w_up_bf = _mlstm(proj, gates, mlstm_norm_g[l].reshape(H, 1, dh), w_up[l],
                             batch=B, seq=S, heads=H, dh=dh, chunk=MLSTM_CHUNK)

        w_ax = (0.5 * jnp.concatenate([lru_wa[l], lru_wx[l]], axis=-1)).astype(BF16)
        hr, w_down_bf, w_out_bf = _rglru(
            proj, lru_conv_w[l], lru_conv_b[l].reshape(1, d_r), w_ax,
            lru_ba[l].reshape(1, d_r), lru_bx[l].reshape(1, d_r), lru_lambda[l].reshape(1, d_r),
            w_down[l], w_out[l], batch=B, seq=S, col_block=gate_lo // d_r, ts=512)

        x1, n2 = _outproj(xf, hm, hr, w_out_bf, norm_ffn_g[l].reshape(1, D), tm=512)

        xf = _ffn(n2, x1, w_up_bf, ffn_conv_w[l], ffn_conv_b[l].reshape(1, -1),
                  w_down_bf, norm_final_g.reshape(1, D), seq=S, tm=512, tf=512,
                  final_norm=(l == depth - 1))
    return xf.reshape(B, S, D)
```

```python
import functools
import math

import jax
import jax.numpy as jnp
from jax import lax
from jax.experimental import pallas as pl
from jax.experimental.pallas import tpu as pltpu

F32 = jnp.float32
BF16 = jnp.bfloat16
EPS = 1e-6
LANES = 128
SUBLANES = 8
VMEM_LIMIT = 56 * 1024 * 1024

MLSTM_HEADS = 4
LRU_BLOCKS = 8
LRU_CONV = 4
LRU_C = 8.0
FFN_CONV = 3
MLSTM_CHUNK = 256


def _sigmoid(x):
    return 0.5 * jnp.tanh(0.5 * x) + 0.5


def _log_sigmoid(x):
    return jnp.minimum(x, 0.0) - jnp.log1p(jnp.exp(-jnp.abs(x)))


def _softplus(x):
    return jnp.maximum(x, 0.0) + jnp.log1p(jnp.exp(-jnp.abs(x)))


def _gelu_tanh(x):
    c = math.sqrt(2.0 / math.pi)
    t = jnp.tanh(x * (c + (c * 0.044715) * (x * x)))
    hx = 0.5 * x
    return hx + hx * t


def _rmsnorm(x, g):
    return x * lax.rsqrt(jnp.mean(x * x, axis=-1, keepdims=True) + EPS) * g


def _scan_lanes(x, op, fill):
    n = x.shape[-1]
    lane = lax.broadcasted_iota(jnp.int32, x.shape, x.ndim - 1)
    d = 1
    while d < n:
        shifted = jnp.where(lane >= d, pltpu.roll(x, d, axis=x.ndim - 1), fill)
        x = op(x, shifted)
        d *= 2
    return x


def _win_cast_kernel(w_ref, nxt_ref, grow_ref, out_ref, wif_ref, *, first_shifted, shift):
    j = pl.program_id(0)

    @pl.when(j < first_shifted)
    def _():
        out_ref[...] = w_ref[...].astype(BF16)

    @pl.when(j >= first_shifted)
    def _():
        out_ref[...] = jnp.concatenate([w_ref[shift:, :], nxt_ref[...]], axis=0).astype(BF16)

    wif_ref[...] = grow_ref[...].astype(BF16)


def _win_cast(wt, *, gate_lo, n_gate, tn):
    P, D = wt.shape
    N = P - n_gate
    assert gate_lo % tn == 0 and N % tn == 0 and tn % n_gate == 0 and n_gate == SUBLANES
    first_shifted = gate_lo // tn
    per = tn // n_gate
    kern = functools.partial(_win_cast_kernel, first_shifted=first_shifted, shift=n_gate)
    return pl.pallas_call(
        kern,
        grid=(N // tn,),
        in_specs=[
            pl.BlockSpec((tn, D), lambda j: (j, 0)),
            pl.BlockSpec((n_gate, D), lambda j: (jnp.maximum(j, first_shifted) * per + per, 0)),
            pl.BlockSpec((LANES, D), lambda j: (gate_lo // LANES, 0)),
        ],
        out_specs=[
            pl.BlockSpec((tn, D), lambda j: (j, 0)),
            pl.BlockSpec((LANES, D), lambda j: (0, 0)),
        ],
        out_shape=[
            jax.ShapeDtypeStruct((N, D), BF16),
            jax.ShapeDtypeStruct((LANES, D), BF16),
        ],
        compiler_params=pltpu.CompilerParams(
            dimension_semantics=("arbitrary",),
            vmem_limit_bytes=VMEM_LIMIT),
        name="win_cast",
    )(wt, wt, wt)


_NT = (((1,), (1,)), ((), ()))


def _inproj_kernel(x_ref, g_ref, w_ref, wif_ref, bif_ref, proj_ref, gate_ref, n1_ref):
    @pl.when(pl.program_id(1) == 0)
    def _():
        n1 = _rmsnorm(x_ref[...], g_ref[...]).astype(BF16)
        n1_ref[...] = n1
        gate_ref[...] = lax.dot_general(n1, wif_ref[...], _NT,
                                        preferred_element_type=F32) + bif_ref[...]

    proj_ref[...] = lax.dot_general(n1_ref[...], w_ref[...], _NT,
                                    preferred_element_type=F32).astype(proj_ref.dtype)


def _inproj(x, g, w_main_t, w_if_t, b_if, *, tm, tn):
    T, D = x.shape
    N = w_main_t.shape[0]
    return pl.pallas_call(
        _inproj_kernel,
        grid=(T // tm, N // tn),
        in_specs=[
            pl.BlockSpec((tm, D), lambda m, n: (m, 0)),
            pl.BlockSpec((1, D), lambda m, n: (0, 0)),
            pl.BlockSpec((tn, D), lambda m, n: (n, 0)),
            pl.BlockSpec((LANES, D), lambda m, n: (0, 0)),
            pl.BlockSpec((1, LANES), lambda m, n: (0, 0)),
        ],
        out_specs=[
            pl.BlockSpec((tm, tn), lambda m, n: (m, n)),
            pl.BlockSpec((tm, LANES), lambda m, n: (m, 0)),
        ],
        out_shape=[
            jax.ShapeDtypeStruct((T, N), BF16),
            jax.ShapeDtypeStruct((T, LANES), F32),
        ],
        scratch_shapes=[pltpu.VMEM((tm, D), BF16)],
        compiler_params=pltpu.CompilerParams(
            dimension_semantics=("arbitrary", "arbitrary"),
            vmem_limit_bytes=VMEM_LIMIT),
        name="inproj",
    )(x, g, w_main_t, w_if_t, b_if)


def _mlstm_kernel(q_ref, k_ref, v_ref, o_ref, gate_ref, hg_ref, wup_ref, out_ref, wup_bf_ref,
                  s_ref, m_ref, tri_ref, sel_ref, *, heads, dh, chunk):
    L = chunk
    scale = dh ** -0.5

    wup_bf_ref[...] = wup_ref[...].astype(BF16)

    @pl.when(pl.program_id(1) == 0)
    def _():
        s_ref[...] = jnp.zeros_like(s_ref)
        m_ref[...] = jnp.zeros_like(m_ref)

    row_i = lax.broadcasted_iota(jnp.int32, (L, L), 0)
    col_i = lax.broadcasted_iota(jnp.int32, (L, L), 1)
    causal = row_i >= col_i

    @pl.when((pl.program_id(0) == 0) & (pl.program_id(1) == 0))
    def _():
        tri_ref[...] = jnp.where(causal, 1.0, 0.0).astype(BF16)
        sr = lax.broadcasted_iota(jnp.int32, sel_ref.shape, 0)
        sc = lax.broadcasted_iota(jnp.int32, sel_ref.shape, 1)
        lane = sr & (LANES - 1)
        blk = sc >> 7
        from_gate = sr < 3 * LANES
        hit_gate = from_gate & (blk < heads) & (lane == blk)
        hit_sum = (~from_gate) & (blk >= heads) & ((lane == blk - heads) | (lane == blk - heads + SUBLANES))
        sel_ref[...] = jnp.where(hit_gate | hit_sum, 1.0, 0.0).astype(BF16)

    def split(x, parts):
        out = []
        for _ in range(parts - 1):
            p = x.astype(BF16)
            out.append(p)
            x = x - p.astype(F32)
        out.append(x.astype(BF16))
        return out

    def rep(x, width):
        return jnp.concatenate([x] * (width // LANES), axis=1)

    g = gate_ref[...]
    tri = tri_ref[...]
    sel = sel_ref[...]

    g8 = g.T[0:SUBLANES, :]
    g8r = pltpu.roll(g8, heads, axis=0)
    sub = lax.broadcasted_iota(jnp.int32, g8.shape, 0)
    ig = jnp.where(sub < heads, g8, g8r)
    fg = jnp.where(sub < heads, g8r, g8)
    logf = jnp.concatenate(split(_log_sigmoid(fg), 2), axis=0)
    br = lax.dot_general(logf, tri, _NT, preferred_element_type=F32)
    b = br[0:SUBLANES, :] + br[SUBLANES:, :]

    logf_pad = jnp.concatenate([logf, jnp.zeros((LANES - 2 * SUBLANES, L), BF16)], axis=0)
    b_nat = lax.dot_general(tri, logf_pad, _NT, preferred_element_type=F32)
    reps = jnp.dot(jnp.concatenate(split(g, 3) + split(b_nat, 3), axis=1), sel,
                   preferred_element_type=F32)
    a_row = ig - b
    b_tot = b[:, L - 1:L]
    m_prev = m_ref[:, 0:1]
    m_new = jnp.maximum(b_tot + m_prev, b_tot + jnp.max(a_row, axis=1, keepdims=True))
    decay = jnp.exp(b_tot + m_prev - m_new)
    end_shift = b_tot - m_new
    m_ref[...] = jnp.broadcast_to(m_new, m_ref.shape)
    ones_blk = jnp.ones((L, LANES), BF16)

    for h in range(heads):
        sl = slice(h * dh, (h + 1) * dh)
        q = q_ref[:, sl] * scale
        k = k_ref[:, sl]
        v = v_ref[:, sl]
        ig_r = reps[:, h * LANES:(h + 1) * LANES]
        b_r = reps[:, (heads + h) * LANES:(heads + h + 1) * LANES]
        a_h = a_row[h:h + 1, :]
        mp = m_prev[h:h + 1, :]

        s = lax.dot_general(q, k, _NT, preferred_element_type=F32)
        cm = jnp.max(jnp.where(causal, a_h, -jnp.inf), axis=-1, keepdims=True)
        mm_r = jnp.maximum(jnp.broadcast_to(cm, (L, LANES)), mp)
        w = s * jnp.exp(jnp.where(causal, a_h - rep(mm_r, L), -jnp.inf))
        einter_r = jnp.exp(mp - mm_r)
        floor_r = jnp.exp(-(b_r + mm_r))
        eend_r = jnp.exp(end_shift[h:h + 1, :] + ig_r - b_r)

        state = s_ref[h]
        intra = jnp.dot(w.astype(BF16), jnp.concatenate([v, ones_blk], axis=1),
                        preferred_element_type=F32)
        inter = jnp.dot(q, state.astype(BF16), preferred_element_type=F32)
        tot = intra + rep(einter_r, dh + LANES) * inter
        num = tot[:, :dh]
        den_r = tot[:, dh:]
        inv_r = 1.0 / jnp.maximum(jnp.abs(den_r), floor_r)
        ssq_r = jnp.broadcast_to(jnp.sum(num * num, axis=-1, keepdims=True), (L, LANES))
        rs_r = inv_r * lax.rsqrt(ssq_r * (inv_r * inv_r) * (1.0 / dh) + EPS)
        gate_o = _sigmoid(o_ref[:, sl].astype(F32))
        out_ref[:, sl] = ((num * rep(rs_r, dh)) * hg_ref[h] * gate_o).astype(out_ref.dtype)

        ev = jnp.concatenate(
            [(rep(eend_r, dh) * v.astype(F32)).astype(BF16), eend_r.astype(BF16)], axis=1)
        loc = lax.dot_general(k, ev, (((0,), (0,)), ((), ())), preferred_element_type=F32)
        s_ref[h] = decay[h:h + 1, :] * state + loc


def _mlstm(proj, gates, head_g, w_up, *, batch, seq, heads, dh, chunk):
    T = batch * seq
    W = heads * dh
    nc = seq // chunk
    steps = batch * nc
    up_rows = w_up.shape[0] // steps
    assert up_rows * steps == w_up.shape[0] and up_rows % 16 == 0
    up_spec = pl.BlockSpec((up_rows, w_up.shape[1]), lambda b, c: (b * nc + c, 0))
    blk = lambda j: pl.BlockSpec((chunk, W), lambda b, c, j=j: (b * nc + c, j))
    kern = functools.partial(_mlstm_kernel, heads=heads, dh=dh, chunk=chunk)
    return pl.pallas_call(
        kern,
        grid=(batch, nc),
        in_specs=[
            blk(0), blk(1), blk(2), blk(3),
            pl.BlockSpec((chunk, LANES), lambda b, c: (b * nc + c, 0)),
            pl.BlockSpec((heads, 1, dh), lambda b, c: (0, 0, 0)),
            up_spec,
        ],
        out_specs=[pl.BlockSpec((chunk, W), lambda b, c: (b * nc + c, 0)), up_spec],
        out_shape=[jax.ShapeDtypeStruct((T, W), BF16), jax.ShapeDtypeStruct(w_up.shape, BF16)],
        scratch_shapes=[
            pltpu.VMEM((heads, dh, dh + LANES), F32),
            pltpu.VMEM((SUBLANES, LANES), F32),
            pltpu.VMEM((chunk, chunk), BF16),
            pltpu.VMEM((6 * LANES, 2 * heads * LANES), BF16),
        ],
        compiler_params=pltpu.CompilerParams(
            dimension_semantics=("arbitrary", "arbitrary"),
            vmem_limit_bytes=VMEM_LIMIT),
        name="mlstm",
    )(proj, proj, proj, proj, gates, head_g, w_up)


def _lru_kernel(xr_ref, gr_ref, cw_ref, cb_ref, wax_ref, ba_ref, bx_ref, lam_ref, wdn_ref, wout_ref,
                out_ref, wdn_bf_ref, wout_bf_ref,
                xtail_ref, hc_ref, perm_ref, unperm_ref, a_scr, u_scr, c_scr, h_scr,
                *, ts, blocks, bd):
    C = blocks * bd
    seg = ts // SUBLANES
    halo = LRU_CONV - 1

    wdn_bf_ref[...] = wdn_ref[...].astype(BF16)
    wout_bf_ref[...] = wout_ref[...].astype(BF16)

    @pl.when(pl.program_id(1) == 0)
    def _():
        xtail_ref[...] = jnp.zeros_like(xtail_ref)
        hc_ref[...] = jnp.zeros_like(hc_ref)

    @pl.when((pl.program_id(0) == 0) & (pl.program_id(1) == 0))
    def _():
        ri = lax.broadcasted_iota(jnp.int32, (ts, ts), 0)
        ci = lax.broadcasted_iota(jnp.int32, (ts, ts), 1)
        t_of_row = (ri & (SUBLANES - 1)) * seg + (ri >> 3)
        t_of_col = (ci & (SUBLANES - 1)) * seg + (ci >> 3)
        perm_ref[...] = jnp.where(ci == t_of_row, 1.0, 0.0).astype(BF16)
        unperm_ref[...] = jnp.where(ri == t_of_col, 1.0, 0.0).astype(BF16)

    xb = xr_ref[...]
    xp = jnp.dot(perm_ref[...], xb, preferred_element_type=F32)
    gp = jnp.dot(perm_ref[...], gr_ref[...], preferred_element_type=F32)
    gelu_g = _gelu_tanh(gp)
    prev = xtail_ref[...]
    xtail_ref[...] = xb[ts - SUBLANES:ts, :].astype(F32)
    row8 = lax.broadcasted_iota(jnp.int32, (SUBLANES, C), 0)

    wrapped = []
    for k in range(1, halo + 1):
        r0 = (seg - k) * SUBLANES
        up = pltpu.roll(xp[r0:r0 + SUBLANES, :], 1, axis=0)
        wrapped.append(jnp.where(row8 == 0, prev[SUBLANES - k:SUBLANES - k + 1, :], up))

    cw = cw_ref[...]
    xc = cb_ref[...] + cw[LRU_CONV - 1:LRU_CONV, :] * xp
    for j in range(1, LRU_CONV):
        xs = jnp.concatenate([wrapped[k - 1] for k in range(j, 0, -1)]
                             + [xp[0:ts - j * SUBLANES, :]], axis=0)
        xc = xc + cw[LRU_CONV - 1 - j:LRU_CONV - j, :] * xs

    xcb = xc.astype(BF16)
    half_xc = 0.5 * xc
    k = (-0.5 * LRU_C * math.log2(math.e)) * _softplus(-lam_ref[...])
    half_ba = 0.5 * ba_ref[...]
    half_bx = 0.5 * bx_ref[...]
    for n in range(blocks):
        sl = slice(n * bd, (n + 1) * bd)
        ax = jnp.dot(xcb[:, sl], wax_ref[n], preferred_element_type=F32)
        t_r = jnp.tanh(ax[:, :bd] + half_ba[:, sl])
        t_i = jnp.tanh(ax[:, bd:] + half_bx[:, sl])
        a = jnp.exp2(k[:, sl] + k[:, sl] * t_r)
        ixc = half_xc[:, sl] + half_xc[:, sl] * t_i
        y = 1.0 - a * a
        a_scr[:, sl] = a
        u_scr[:, sl] = jnp.where(y > 0.0, y * lax.rsqrt(y), 0.0) * ixc

    def body(v, carry):
        h, c = carry
        rows = pl.ds(pl.multiple_of(v * SUBLANES, SUBLANES), SUBLANES)
        av = a_scr[rows, :]
        h = av * h + u_scr[rows, :]
        c = av * c
        h_scr[rows, :] = h
        c_scr[rows, :] = c
        return h, c

    init = (jnp.zeros((SUBLANES, C), F32), jnp.ones((SUBLANES, C), F32))
    h_end, c_end = lax.fori_loop(0, seg, body, init, unroll=4)

    state = hc_ref[0:1, :]
    entering = []
    for s in range(SUBLANES):
        entering.append(state)
        state = c_end[s:s + 1, :] * state + h_end[s:s + 1, :]
    hc_ref[0:1, :] = state
    entering = jnp.concatenate(entering, axis=0)

    h = (h_scr[...].reshape(seg, SUBLANES, C)
         + c_scr[...].reshape(seg, SUBLANES, C) * entering[None]).reshape(ts, C)
    outp = (h * gelu_g).astype(BF16)
    out_ref[...] = jnp.dot(unperm_ref[...], outp,
                           preferred_element_type=F32).astype(out_ref.dtype)


def _rglru(proj, conv_w, conv_b, w_ax, ba, bx, lam, w_down, w_out, *, batch, seq, col_block, ts):
    T = batch * seq
    blocks, bd, _ = w_ax.shape
    C = blocks * bd
    ns = seq // ts
    steps = batch * ns

    def slab(w):
        rows = w.shape[0] // steps
        assert rows * steps == w.shape[0] and rows % 16 == 0
        return pl.BlockSpec((rows, w.shape[1]), lambda b, s: (b * ns + s, 0))
    vec = lambda rows: pl.BlockSpec((rows, C), lambda b, s: (0, 0))
    assert ts % (SUBLANES * SUBLANES) == 0 and bd == LANES
    kern = functools.partial(_lru_kernel, ts=ts, blocks=blocks, bd=bd)
    return pl.pallas_call(
        kern,
        grid=(batch, ns),
        in_specs=[
            pl.BlockSpec((ts, C), lambda b, s: (b * ns + s, col_block)),
            pl.BlockSpec((ts, C), lambda b, s: (b * ns + s, col_block + 1)),
            vec(LRU_CONV), vec(1),
            pl.BlockSpec((blocks, bd, 2 * bd), lambda b, s: (0, 0, 0)),
            vec(1), vec(1), vec(1),
            slab(w_down), slab(w_out),
        ],
        out_specs=[pl.BlockSpec((ts, C), lambda b, s: (b * ns + s, 0)), slab(w_down), slab(w_out)],
        out_shape=[jax.ShapeDtypeStruct((T, C), BF16),
                   jax.ShapeDtypeStruct(w_down.shape, BF16),
                   jax.ShapeDtypeStruct(w_out.shape, BF16)],
        scratch_shapes=[
            pltpu.VMEM((SUBLANES, C), F32),
            pltpu.VMEM((SUBLANES, C), F32),
            pltpu.VMEM((ts, ts), BF16),
            pltpu.VMEM((ts, ts), BF16),
        ] + [pltpu.VMEM((ts, C), F32)] * 4,
        compiler_params=pltpu.CompilerParams(
            dimension_semantics=("arbitrary", "arbitrary"),
            vmem_limit_bytes=VMEM_LIMIT),
        name="rglru",
    )(proj, proj, conv_w, conv_b, w_ax, ba, bx, lam, w_down, w_out)


def _outproj_kernel(x_ref, hm_ref, hr_ref, wm_ref, wr_ref, g_ref, x1_ref, n2_ref):
    x1 = (x_ref[...]
          + jnp.dot(hm_ref[...], wm_ref[...], preferred_element_type=F32)
          + jnp.dot(hr_ref[...], wr_ref[...], preferred_element_type=F32))
    x1_ref[...] = x1
    n2_ref[...] = _rmsnorm(x1, g_ref[...]).astype(n2_ref.dtype)


def _outproj(x, hm, hr, w_out, g, *, tm):
    T, D = x.shape
    Wm = hm.shape[1]
    Wr = hr.shape[1]
    return pl.pallas_call(
        _outproj_kernel,
        grid=(T // tm,),
        in_specs=[
            pl.BlockSpec((tm, D), lambda m: (m, 0)),
            pl.BlockSpec((tm, Wm), lambda m: (m, 0)),
            pl.BlockSpec((tm, Wr), lambda m: (m, 0)),
            pl.BlockSpec((Wm, D), lambda m: (0, 0)),
            pl.BlockSpec((Wr, D), lambda m: (Wm // Wr, 0)),
            pl.BlockSpec((1, D), lambda m: (0, 0)),
        ],
        out_specs=[
            pl.BlockSpec((tm, D), lambda m: (m, 0)),
            pl.BlockSpec((tm, D), lambda m: (m, 0)),
        ],
        out_shape=[
            jax.ShapeDtypeStruct((T, D), F32),
            jax.ShapeDtypeStruct((T, D), BF16),
        ],
        compiler_params=pltpu.CompilerParams(
            dimension_semantics=("arbitrary",),
            vmem_limit_bytes=VMEM_LIMIT),
        name="outproj",
    )(x, hm, hr, w_out, w_out, g)


def _ffn_kernel(n2_ref, x1_ref, wg_ref, wu_ref, cw_ref, cb_ref, wd_ref, gf_ref, y_ref,
                acc_ref, tail_ref, *, tm, tiles_per_seq, final_norm):
    m = pl.program_id(0)
    f = pl.program_id(1)
    nf = pl.num_programs(1)

    @pl.when(f == 0)
    def _():
        acc_ref[...] = x1_ref[...]

    n2 = n2_ref[...]
    gate = jnp.dot(n2, wg_ref[...], preferred_element_type=F32)
    up = jnp.dot(n2, wu_ref[...], preferred_element_type=F32)

    prev = jnp.where(m % tiles_per_seq == 0, 0.0, tail_ref[f])
    tail_ref[f] = gate[tm - SUBLANES:tm, :]

    row8 = lax.broadcasted_iota(jnp.int32, prev.shape, 0)
    cw = cw_ref[...]
    conv = cb_ref[...] + cw[FFN_CONV - 1:FFN_CONV, :] * gate
    for j in range(1, FFN_CONV):
        r = pltpu.roll(gate, j, axis=0)
        top = jnp.where(row8 < j, pltpu.roll(prev, j, axis=0), r[0:SUBLANES, :])
        gs = jnp.concatenate([top, r[SUBLANES:, :]], axis=0)
        conv = conv + cw[FFN_CONV - 1 - j:FFN_CONV - j, :] * gs
    hid = (conv * _sigmoid(conv)) * up
    acc_ref[...] += jnp.dot(hid.astype(BF16), wd_ref[...], preferred_element_type=F32)

    @pl.when(f == nf - 1)
    def _():
        xo = acc_ref[...]
        y_ref[...] = _rmsnorm(xo, gf_ref[...]) if final_norm else xo


def _ffn(n2, x1, w_up, conv_w, conv_b, w_down, gf, *, seq, tm, tf, final_norm):
    T, D = n2.shape
    F = w_down.shape[0]
    nf = F // tf
    kern = functools.partial(_ffn_kernel, tm=tm, tiles_per_seq=seq // tm, final_norm=final_norm)
    return pl.pallas_call(
        kern,
        grid=(T // tm, nf),
        in_specs=[
            pl.BlockSpec((tm, D), lambda m, f: (m, 0)),
            pl.BlockSpec((tm, D), lambda m, f: (m, 0)),
            pl.BlockSpec((D, tf), lambda m, f: (0, f)),
            pl.BlockSpec((D, tf), lambda m, f: (0, nf + f)),
            pl.BlockSpec((FFN_CONV, tf), lambda m, f: (0, f)),
            pl.BlockSpec((1, tf), lambda m, f: (0, f)),
            pl.BlockSpec((tf, D), lambda m, f: (f, 0)),
            pl.BlockSpec((1, D), lambda m, f: (0, 0)),
        ],
        out_specs=pl.BlockSpec((tm, D), lambda m, f: (m, 0)),
        out_shape=jax.ShapeDtypeStruct((T, D), F32),
        scratch_shapes=[
            pltpu.VMEM((tm, D), F32),
            pltpu.VMEM((nf, SUBLANES, tf), F32),
        ],
        compiler_params=pltpu.CompilerParams(
            dimension_semantics=("arbitrary", "arbitrary"),
            vmem_limit_bytes=VMEM_LIMIT),
        name="convffn",
    )(n2, x1, w_up, w_up, conv_w, conv_b, w_down, gf)


def kernel(x, norm_mix_g, w_in, b_gate_m, mlstm_norm_g, lru_conv_w, lru_conv_b, lru_wa, lru_ba,
           lru_wx, lru_bx, lru_lambda, w_out, norm_ffn_g, w_up, ffn_conv_w, ffn_conv_b, w_down,
           norm_final_g):
    B, S, D = x.shape
    T = B * S
    depth = w_in.shape[0]
    H = MLSTM_HEADS
    dh = mlstm_norm_g.shape[-1]
    d_m = H * dh
    d_r = lru_conv_w.shape[-1]
    assert d_m == d_r, "column-block addressing assumes equal head-group widths"
    n_gate = 2 * H
    gate_lo = 4 * d_m

    xf = x.reshape(T, D)
    for l in range(depth):
        w_main, w_if = _win_cast(jnp.swapaxes(w_in[l], 0, 1), gate_lo=gate_lo, n_gate=n_gate, tn=512)
        b_if = jnp.pad(b_gate_m[l].astype(F32), (0, LANES - n_gate)).reshape(1, LANES)
        proj, gates = _inproj(xf, norm_mix_g[l].reshape(1, D), w_main, w_if, b_if, tm=1024, tn=1024)

        hm, w_up_bf = _mlstm(proj, gates, mlstm_norm_g[l].reshape(H, 1, dh), w_up[l],
                             batch=B, seq=S, heads=H, dh=dh, chunk=MLSTM_CHUNK)

        w_ax = (0.5 * jnp.concatenate([lru_wa[l], lru_wx[l]], axis=-1)).astype(BF16)
        hr, w_down_bf, w_out_bf = _rglru(
            proj, lru_conv_w[l], lru_conv_b[l].reshape(1, d_r), w_ax,
            lru_ba[l].reshape(1, d_r), lru_bx[l].reshape(1, d_r), lru_lambda[l].reshape(1, d_r),
            w_down[l], w_out[l], batch=B, seq=S, col_block=gate_lo // d_r, ts=512)

        x1, n2 = _outproj(xf, hm, hr, w_out_bf, norm_ffn_g[l].reshape(1, D), tm=512)

        xf = _ffn(n2, x1, w_up_bf, ffn_conv_w[l], ffn_conv_b[l].reshape(1, -1),
                  w_down_bf, norm_final_g.reshape(1, D), seq=S, tm=512, tf=512,
                  final_norm=(l == depth - 1))
    return xf.reshape(B, S, D)
```

```python
import functools
import math

import jax
import jax.numpy as jnp
from jax import lax
from jax.experimental import pallas as pl
from jax.experimental.pallas import tpu as pltpu

F32 = jnp.float32
BF16 = jnp.bfloat16
EPS = 1e-6
LANES = 128
SUBLANES = 8
VMEM_LIMIT = 56 * 1024 * 1024

MLSTM_HEADS = 4
LRU_BLOCKS = 8
LRU_CONV = 4
LRU_C = 8.0
FFN_CONV = 3
MLSTM_CHUNK = 256


def _sigmoid(x):
    return 0.5 * jnp.tanh(0.5 * x) + 0.5


def _log_sigmoid(x):
    return jnp.minimum(x, 0.0) - jnp.log1p(jnp.exp(-jnp.abs(x)))


def _softplus(x):
    return jnp.maximum(x, 0.0) + jnp.log1p(jnp.exp(-jnp.abs(x)))


def _gelu_tanh(x):
    c = math.sqrt(2.0 / math.pi)
    t = jnp.tanh(x * (c + (c * 0.044715) * (x * x)))
    hx = 0.5 * x
    return hx + hx * t


def _rmsnorm(x, g):
    return x * lax.rsqrt(jnp.mean(x * x, axis=-1, keepdims=True) + EPS) * g


def _scan_lanes(x, op, fill):
    n = x.shape[-1]
    lane = lax.broadcasted_iota(jnp.int32, x.shape, x.ndim - 1)
    d = 1
    while d < n:
        shifted = jnp.where(lane >= d, pltpu.roll(x, d, axis=x.ndim - 1), fill)
        x = op(x, shifted)
        d *= 2
    return x


_NT = (((1,), (1,)), ((), ()))


def _inproj_kernel(x_ref, g_ref, w_ref, nxt_ref, wif_ref, bif_ref, proj_ref, gate_ref, n1_ref,
                   *, first_shifted, shift):
    n = pl.program_id(1)

    @pl.when(n == 0)
    def _():
        n1 = _rmsnorm(x_ref[...], g_ref[...]).astype(BF16)
        n1_ref[...] = n1
        gate_ref[...] = lax.dot_general(n1, wif_ref[...].astype(BF16), _NT,
                                        preferred_element_type=F32) + bif_ref[...]

    def project(w_bf):
        proj_ref[...] = lax.dot_general(n1_ref[...], w_bf, _NT,
                                        preferred_element_type=F32).astype(proj_ref.dtype)

    @pl.when(n < first_shifted)
    def _():
        project(w_ref[...].astype(BF16))

    @pl.when(n >= first_shifted)
    def _():
        project(jnp.concatenate([w_ref[shift:, :], nxt_ref[...]], axis=0).astype(BF16))


def _inproj(x, g, wt, b_if, *, gate_lo, n_gate, tm, tn):
    T, D = x.shape
    P = wt.shape[0]
    N = P - n_gate
    assert gate_lo % tn == 0 and N % tn == 0 and n_gate == SUBLANES and gate_lo % LANES == 0
    first_shifted = gate_lo // tn
    per = tn // n_gate
    kern = functools.partial(_inproj_kernel, first_shifted=first_shifted, shift=n_gate)
    return pl.pallas_call(
        kern,
        grid=(T // tm, N // tn),
        in_specs=[
            pl.BlockSpec((tm, D), lambda m, n: (m, 0)),
            pl.BlockSpec((1, D), lambda m, n: (0, 0)),
            pl.BlockSpec((tn, D), lambda m, n: (n, 0)),
            pl.BlockSpec((n_gate, D), lambda m, n: (jnp.maximum(n, first_shifted) * per + per, 0)),
            pl.BlockSpec((LANES, D), lambda m, n: (gate_lo // LANES, 0)),
            pl.BlockSpec((1, LANES), lambda m, n: (0, 0)),
        ],
        out_specs=[
            pl.BlockSpec((tm, tn), lambda m, n: (m, n)),
            pl.BlockSpec((tm, LANES), lambda m, n: (m, 0)),
        ],
        out_shape=[
            jax.ShapeDtypeStruct((T, N), BF16),
            jax.ShapeDtypeStruct((T, LANES), F32),
        ],
        scratch_shapes=[pltpu.VMEM((tm, D), BF16)],
        compiler_params=pltpu.CompilerParams(
            dimension_semantics=("arbitrary", "arbitrary"),
            vmem_limit_bytes=VMEM_LIMIT),
        name="inproj",
    )(x, g, wt, wt, wt, b_if)


def _mlstm_kernel(q_ref, k_ref, v_ref, o_ref, gate_ref, hg_ref, wup_ref, out_ref, wup_bf_ref,
                  s_ref, m_ref, tri_ref, sel_ref, *, heads, dh, chunk):
    L = chunk
    scale = dh ** -0.5

    wup_bf_ref[...] = wup_ref[...].astype(BF16)

    @pl.when(pl.program_id(1) == 0)
    def _():
        s_ref[...] = jnp.zeros_like(s_ref)
        m_ref[...] = jnp.zeros_like(m_ref)

    row_i = lax.broadcasted_iota(jnp.int32, (L, L), 0)
    col_i = lax.broadcasted_iota(jnp.int32, (L, L), 1)
    causal = row_i >= col_i

    @pl.when((pl.program_id(0) == 0) & (pl.program_id(1) == 0))
    def _():
        tri_ref[...] = jnp.where(causal, 1.0, 0.0).astype(BF16)
        sr = lax.broadcasted_iota(jnp.int32, sel_ref.shape, 0)
        sc = lax.broadcasted_iota(jnp.int32, sel_ref.shape, 1)
        lane = sr & (LANES - 1)
        blk = sc >> 7
        from_gate = sr < 3 * LANES
        hit_gate = from_gate & (blk < heads) & (lane == blk)
        hit_sum = (~from_gate) & (blk >= heads) & ((lane == blk - heads) | (lane == blk - heads + SUBLANES))
        sel_ref[...] = jnp.where(hit_gate | hit_sum, 1.0, 0.0).astype(BF16)

    def split(x, parts):
        out = []
        for _ in range(parts - 1):
            p = x.astype(BF16)
            out.append(p)
            x = x - p.astype(F32)
        out.append(x.astype(BF16))
        return out

    def rep(x, width):
        return jnp.concatenate([x] * (width // LANES), axis=1)

    g = gate_ref[...]
    tri = tri_ref[...]
    sel = sel_ref[...]

    g8 = g.T[0:SUBLANES, :]
    g8r = pltpu.roll(g8, heads, axis=0)
    sub = lax.broadcasted_iota(jnp.int32, g8.shape, 0)
    ig = jnp.where(sub < heads, g8, g8r)
    fg = jnp.where(sub < heads, g8r, g8)
    logf = jnp.concatenate(split(_log_sigmoid(fg), 2), axis=0)
    br = lax.dot_general(logf, tri, _NT, preferred_element_type=F32)
    b = br[0:SUBLANES, :] + br[SUBLANES:, :]

    logf_pad = jnp.concatenate([logf, jnp.zeros((LANES - 2 * SUBLANES, L), BF16)], axis=0)
    b_nat = lax.dot_general(tri, logf_pad, _NT, preferred_element_type=F32)
    reps = jnp.dot(jnp.concatenate(split(g, 3) + split(b_nat, 3), axis=1), sel,
                   preferred_element_type=F32)
    a_row = ig - b
    b_tot = b[:, L - 1:L]
    m_prev = m_ref[:, 0:1]
    m_new = jnp.maximum(b_tot + m_prev, b_tot + jnp.max(a_row, axis=1, keepdims=True))
    decay = jnp.exp(b_tot + m_prev - m_new)
    end_shift = b_tot - m_new
    m_ref[...] = jnp.broadcast_to(m_new, m_ref.shape)
    ones_blk = jnp.ones((L, LANES), BF16)

    for h in range(heads):
        sl = slice(h * dh, (h + 1) * dh)
        q = q_ref[:, sl] * scale
        k = k_ref[:, sl]
        v = v_ref[:, sl]
        ig_r = reps[:, h * LANES:(h + 1) * LANES]
        b_r = reps[:, (heads + h) * LANES:(heads + h + 1) * LANES]
        a_h = a_row[h:h + 1, :]
        mp = m_prev[h:h + 1, :]

        s = lax.dot_general(q, k, _NT, preferred_element_type=F32)
        cm = jnp.max(jnp.where(causal, a_h, -jnp.inf), axis=-1, keepdims=True)
        mm_r = jnp.maximum(jnp.broadcast_to(cm, (L, LANES)), mp)
        w = s * jnp.exp(jnp.where(causal, a_h - rep(mm_r, L), -jnp.inf))
        einter_r = jnp.exp(mp - mm_r)
        floor_r = jnp.exp(-(b_r + mm_r))
        eend_r = jnp.exp(end_shift[h:h + 1, :] + ig_r - b_r)

        state = s_ref[h]
        intra = jnp.dot(w.astype(BF16), jnp.concatenate([v, ones_blk], axis=1),
                        preferred_element_type=F32)
        inter = jnp.dot(q, state.astype(BF16), preferred_element_type=F32)
        tot = intra + rep(einter_r, dh + LANES) * inter
        num = tot[:, :dh]
        den_r = tot[:, dh:]
        inv_r = 1.0 / jnp.maximum(jnp.abs(den_r), floor_r)
        ssq_r = jnp.broadcast_to(jnp.sum(num * num, axis=-1, keepdims=True), (L, LANES))
        rs_r = inv_r * lax.rsqrt(ssq_r * (inv_r * inv_r) * (1.0 / dh) + EPS)
        gate_o = _sigmoid(o_ref[:, sl].astype(F32))
        out_ref[:, sl] = ((num * rep(rs_r, dh)) * hg_ref[h] * gate_o).astype(out_ref.dtype)

        ev = jnp.concatenate(
            [(rep(eend_r, dh) * v.astype(F32)).astype(BF16), eend_r.astype(BF16)], axis=1)
        loc = lax.dot_general(k, ev, (((0,), (0,)), ((), ())), preferred_element_type=F32)
        s_ref[h] = decay[h:h + 1, :] * state + loc


def _mlstm(proj, gates, head_g, w_up, *, batch, seq, heads, dh, chunk):
    T = batch * seq
    W = heads * dh
    nc = seq // chunk
    steps = batch * nc
    up_rows = w_up.shape[0] // steps
    assert up_rows * steps == w_up.shape[0] and up_rows % 16 == 0
    up_spec = pl.BlockSpec((up_rows, w_up.shape[1]), lambda b, c: (b * nc + c, 0))
    blk = lambda j: pl.BlockSpec((chunk, W), lambda b, c, j=j: (b * nc + c, j))
    kern = functools.partial(_mlstm_kernel, heads=heads, dh=dh, chunk=chunk)
    return pl.pallas_call(
        kern,
        grid=(batch, nc),
        in_specs=[
            blk(0), blk(1), blk(2), blk(3),
            pl.BlockSpec((chunk, LANES), lambda b, c: (b * nc + c, 0)),
            pl.BlockSpec((heads, 1, dh), lambda b, c: (0, 0, 0)),
            up_spec,
        ],
        out_specs=[pl.BlockSpec((chunk, W), lambda b, c: (b * nc + c, 0)), up_spec],
        out_shape=[jax.ShapeDtypeStruct((T, W), BF16), jax.ShapeDtypeStruct(w_up.shape, BF16)],
        scratch_shapes=[
            pltpu.VMEM((heads, dh, dh + LANES), F32),
            pltpu.VMEM((SUBLANES, LANES), F32),
            pltpu.VMEM((chunk, chunk), BF16),
            pltpu.VMEM((6 * LANES, 2 * heads * LANES), BF16),
        ],
        compiler_params=pltpu.CompilerParams(
            dimension_semantics=("arbitrary", "arbitrary"),
            vmem_limit_bytes=VMEM_LIMIT),
        name="mlstm",
    )(proj, proj, proj, proj, gates, head_g, w_up)


def _lru_kernel(xr_ref, gr_ref, cw_ref, cb_ref, wax_ref, ba_ref, bx_ref, lam_ref, wdn_ref, wout_ref,
                out_ref, wdn_bf_ref, wout_bf_ref,
                xtail_ref, hc_ref, perm_ref, unperm_ref, a_scr, u_scr, c_scr, h_scr,
                *, ts, blocks, bd):
    C = blocks * bd
    seg = ts // SUBLANES
    halo = LRU_CONV - 1

    wdn_bf_ref[...] = wdn_ref[...].astype(BF16)
    wout_bf_ref[...] = wout_ref[...].astype(BF16)

    @pl.when(pl.program_id(1) == 0)
    def _():
        xtail_ref[...] = jnp.zeros_like(xtail_ref)
        hc_ref[...] = jnp.zeros_like(hc_ref)

    @pl.when((pl.program_id(0) == 0) & (pl.program_id(1) == 0))
    def _():
        ri = lax.broadcasted_iota(jnp.int32, (ts, ts), 0)
        ci = lax.broadcasted_iota(jnp.int32, (ts, ts), 1)
        t_of_row = (ri & (SUBLANES - 1)) * seg + (ri >> 3)
        t_of_col = (ci & (SUBLANES - 1)) * seg + (ci >> 3)
        perm_ref[...] = jnp.where(ci == t_of_row, 1.0, 0.0).astype(BF16)
        unperm_ref[...] = jnp.where(ri == t_of_col, 1.0, 0.0).astype(BF16)

    xb = xr_ref[...]
    xp = jnp.dot(perm_ref[...], xb, preferred_element_type=F32)
    gp = jnp.dot(perm_ref[...], gr_ref[...], preferred_element_type=F32)
    gelu_g = _gelu_tanh(gp)
    prev = xtail_ref[...]
    xtail_ref[...] = xb[ts - SUBLANES:ts, :].astype(F32)
    row8 = lax.broadcasted_iota(jnp.int32, (SUBLANES, C), 0)

    wrapped = []
    for k in range(1, halo + 1):
        r0 = (seg - k) * SUBLANES
        up = pltpu.roll(xp[r0:r0 + SUBLANES, :], 1, axis=0)
        wrapped.append(jnp.where(row8 == 0, prev[SUBLANES - k:SUBLANES - k + 1, :], up))

    cw = cw_ref[...]
    xc = cb_ref[...] + cw[LRU_CONV - 1:LRU_CONV, :] * xp
    for j in range(1, LRU_CONV):
        xs = jnp.concatenate([wrapped[k - 1] for k in range(j, 0, -1)]
                             + [xp[0:ts - j * SUBLANES, :]], axis=0)
        xc = xc + cw[LRU_CONV - 1 - j:LRU_CONV - j, :] * xs

    xcb = xc.astype(BF16)
    half_xc = 0.5 * xc
    k = (-0.5 * LRU_C * math.log2(math.e)) * _softplus(-lam_ref[...])
    half_ba = 0.5 * ba_ref[...]
    half_bx = 0.5 * bx_ref[...]
    for n in range(blocks):
        sl = slice(n * bd, (n + 1) * bd)
        ax = jnp.dot(xcb[:, sl], wax_ref[n], preferred_element_type=F32)
        t_r = jnp.tanh(ax[:, :bd] + half_ba[:, sl])
        t_i = jnp.tanh(ax[:, bd:] + half_bx[:, sl])
        a = jnp.exp2(k[:, sl] + k[:, sl] * t_r)
        ixc = half_xc[:, sl] + half_xc[:, sl] * t_i
        y = 1.0 - a * a
        a_scr[:, sl] = a
        u_scr[:, sl] = jnp.where(y > 0.0, y * lax.rsqrt(y), 0.0) * ixc

    def body(v, carry):
        h, c = carry
        rows = pl.ds(pl.multiple_of(v * SUBLANES, SUBLANES), SUBLANES)
        av = a_scr[rows, :]
        h = av * h + u_scr[rows, :]
        c = av * c
        h_scr[rows, :] = h
        c_scr[rows, :] = c
        return h, c

    init = (jnp.zeros((SUBLANES, C), F32), jnp.ones((SUBLANES, C), F32))
    h_end, c_end = lax.fori_loop(0, seg, body, init, unroll=4)

    state = hc_ref[0:1, :]
    entering = []
    for s in range(SUBLANES):
        entering.append(state)
        state = c_end[s:s + 1, :] * state + h_end[s:s + 1, :]
    hc_ref[0:1, :] = state
    entering = jnp.concatenate(entering, axis=0)

    h = (h_scr[...].reshape(seg, SUBLANES, C)
         + c_scr[...].reshape(seg, SUBLANES, C) * entering[None]).reshape(ts, C)
    outp = (h * gelu_g).astype(BF16)
    out_ref[...] = jnp.dot(unperm_ref[...], outp,
                           preferred_element_type=F32).astype(out_ref.dtype)


def _rglru(proj, conv_w, conv_b, w_ax, ba, bx, lam, w_down, w_out, *, batch, seq, col_block, ts):
    T = batch * seq
    blocks, bd, _ = w_ax.shape
    C = blocks * bd
    ns = seq // ts
    steps = batch * ns

    def slab(w):
        rows = w.shape[0] // steps
        assert rows * steps == w.shape[0] and rows % 16 == 0
        return pl.BlockSpec((rows, w.shape[1]), lambda b, s: (b * ns + s, 0))
    vec = lambda rows: pl.BlockSpec((rows, C), lambda b, s: (0, 0))
    assert ts % (SUBLANES * SUBLANES) == 0 and bd == LANES
    kern = functools.partial(_lru_kernel, ts=ts, blocks=blocks, bd=bd)
    return pl.pallas_call(
        kern,
        grid=(batch, ns),
        in_specs=[
            pl.BlockSpec((ts, C), lambda b, s: (b * ns + s, col_block)),
            pl.BlockSpec((ts, C), lambda b, s: (b * ns + s, col_block + 1)),
            vec(LRU_CONV), vec(1),
            pl.BlockSpec((blocks, bd, 2 * bd), lambda b, s: (0, 0, 0)),
            vec(1), vec(1), vec(1),
            slab(w_down), slab(w_out),
        ],
        out_specs=[pl.BlockSpec((ts, C), lambda b, s: (b * ns + s, 0)), slab(w_down), slab(w_out)],
        out_shape=[jax.ShapeDtypeStruct((T, C), BF16),
                   jax.ShapeDtypeStruct(w_down.shape, BF16),
                   jax.ShapeDtypeStruct(w_out.shape, BF16)],
        scratch_shapes=[
            pltpu.VMEM((SUBLANES, C), F32),
            pltpu.VMEM((SUBLANES, C), F32),
            pltpu.VMEM((ts, ts), BF16),
            pltpu.VMEM((ts, ts), BF16),
        ] + [pltpu.VMEM((ts, C), F32)] * 4,
        compiler_params=pltpu.CompilerParams(
            dimension_semantics=("arbitrary", "arbitrary"),
            vmem_limit_bytes=VMEM_LIMIT),
        name="rglru",
    )(proj, proj, conv_w, conv_b, w_ax, ba, bx, lam, w_down, w_out)


def _outproj_kernel(x_ref, hm_ref, hr_ref, wm_ref, wr_ref, g_ref, x1_ref, n2_ref):
    x1 = (x_ref[...]
          + jnp.dot(hm_ref[...], wm_ref[...], preferred_element_type=F32)
          + jnp.dot(hr_ref[...], wr_ref[...], preferred_element_type=F32))
    x1_ref[...] = x1
    n2_ref[...] = _rmsnorm(x1, g_ref[...]).astype(n2_ref.dtype)


def _outproj(x, hm, hr, w_out, g, *, tm):
    T, D = x.shape
    Wm = hm.shape[1]
    Wr = hr.shape[1]
    return pl.pallas_call(
        _outproj_kernel,
        grid=(T // tm,),
        in_specs=[
            pl.BlockSpec((tm, D), lambda m: (m, 0)),
            pl.BlockSpec((tm, Wm), lambda m: (m, 0)),
            pl.BlockSpec((tm, Wr), lambda m: (m, 0)),
            pl.BlockSpec((Wm, D), lambda m: (0, 0)),
            pl.BlockSpec((Wr, D), lambda m: (Wm // Wr, 0)),
            pl.BlockSpec((1, D), lambda m: (0, 0)),
        ],
        out_specs=[
            pl.BlockSpec((tm, D), lambda m: (m, 0)),
            pl.BlockSpec((tm, D), lambda m: (m, 0)),
        ],
        out_shape=[
            jax.ShapeDtypeStruct((T, D), F32),
            jax.ShapeDtypeStruct((T, D), BF16),
        ],
        compiler_params=pltpu.CompilerParams(
            dimension_semantics=("arbitrary",),
            vmem_limit_bytes=VMEM_LIMIT),
        name="outproj",
    )(x, hm, hr, w_out, w_out, g)


def _ffn_kernel(n2_ref, x1_ref, wg_ref, wu_ref, cw_ref, cb_ref, wd_ref, gf_ref, y_ref,
                acc_ref, tail_ref, *, tm, tiles_per_seq, final_norm):
    m = pl.program_id(0)
    f = pl.program_id(1)
    nf = pl.num_programs(1)

    @pl.when(f == 0)
    def _():
        acc_ref[...] = x1_ref[...]

    n2 = n2_ref[...]
    gate = jnp.dot(n2, wg_ref[...], preferred_element_type=F32)
    up = jnp.dot(n2, wu_ref[...], preferred_element_type=F32)

    prev = jnp.where(m % tiles_per_seq == 0, 0.0, tail_ref[f])
    tail_ref[f] = gate[tm - SUBLANES:tm, :]

    row8 = lax.broadcasted_iota(jnp.int32, prev.shape, 0)
    cw = cw_ref[...]
    conv = cb_ref[...] + cw[FFN_CONV - 1:FFN_CONV, :] * gate
    for j in range(1, FFN_CONV):
        r = pltpu.roll(gate, j, axis=0)
        top = jnp.where(row8 < j, pltpu.roll(prev, j, axis=0), r[0:SUBLANES, :])
        gs = jnp.concatenate([top, r[SUBLANES:, :]], axis=0)
        conv = conv + cw[FFN_CONV - 1 - j:FFN_CONV - j, :] * gs
    hid = (conv * _sigmoid(conv)) * up
    acc_ref[...] += jnp.dot(hid.astype(BF16), wd_ref[...], preferred_element_type=F32)

    @pl.when(f == nf - 1)
    def _():
        xo = acc_ref[...]
        y_ref[...] = _rmsnorm(xo, gf_ref[...]) if final_norm else xo


def _ffn(n2, x1, w_up, conv_w, conv_b, w_down, gf, *, seq, tm, tf, final_norm):
    T, D = n2.shape
    F = w_down.shape[0]
    nf = F // tf
    kern = functools.partial(_ffn_kernel, tm=tm, tiles_per_seq=seq // tm, final_norm=final_norm)
    return pl.pallas_call(
        kern,
        grid=(T // tm, nf),
        in_specs=[
            pl.BlockSpec((tm, D), lambda m, f: (m, 0)),
            pl.BlockSpec((tm, D), lambda m, f: (m, 0)),
            pl.BlockSpec((D, tf), lambda m, f: (0, f)),
            pl.BlockSpec((D, tf), lambda m, f: (0, nf + f)),
            pl.BlockSpec((FFN_CONV, tf), lambda m, f: (0, f)),
            pl.BlockSpec((1, tf), lambda m, f: (0, f)),
            pl.BlockSpec((tf, D), lambda m, f: (f, 0)),
            pl.BlockSpec((1, D), lambda m, f: (0, 0)),
        ],
        out_specs=pl.BlockSpec((tm, D), lambda m, f: (m, 0)),
        out_shape=jax.ShapeDtypeStruct((T, D), F32),
        scratch_shapes=[
            pltpu.VMEM((tm, D), F32),
            pltpu.VMEM((nf, SUBLANES, tf), F32),
        ],
        compiler_params=pltpu.CompilerParams(
            dimension_semantics=("arbitrary", "arbitrary"),
            vmem_limit_bytes=VMEM_LIMIT),
        name="convffn",
    )(n2, x1, w_up, w_up, conv_w, conv_b, w_down, gf)


def kernel(x, norm_mix_g, w_in, b_gate_m, mlstm_norm_g, lru_conv_w, lru_conv_b, lru_wa, lru_ba,
           lru_wx, lru_bx, lru_lambda, w_out, norm_ffn_g, w_up, ffn_conv_w, ffn_conv_b, w_down,
           norm_final_g):
    B, S, D = x.shape
    T = B * S
    depth = w_in.shape[0]
    H = MLSTM_HEADS
    dh = mlstm_norm_g.shape[-1]
    d_m = H * dh
    d_r = lru_conv_w.shape[-1]
    assert d_m == d_r, "column-block addressing assumes equal head-group widths"
    n_gate = 2 * H
    gate_lo = 4 * d_m

    xf = x.reshape(T, D)
    for l in range(depth):
        b_if = jnp.pad(b_gate_m[l].astype(F32), (0, LANES - n_gate)).reshape(1, LANES)
        proj, gates = _inproj(xf, norm_mix_g[l].reshape(1, D), jnp.swapaxes(w_in[l], 0, 1), b_if,
                              gate_lo=gate_lo, n_gate=n_gate, tm=1024, tn=1024)

        hm, w_up_bf = _mlstm(proj, gates, mlstm_norm_g[l].reshape(H, 1, dh), w_up[l],
                             batch=B, seq=S, heads=H, dh=dh, chunk=MLSTM_CHUNK)

        w_ax = (0.5 * jnp.concatenate([lru_wa[l], lru_wx[l]], axis=-1)).astype(BF16)
        hr, w_down_bf, w_out_bf = _rglru(
            proj, lru_conv_w[l], lru_conv_b[l].reshape(1, d_r), w_ax,
            lru_ba[l].reshape(1, d_r), lru_bx[l].reshape(1, d_r), lru_lambda[l].reshape(1, d_r),
            w_down[l], w_out[l], batch=B, seq=S, col_block=gate_lo // d_r, ts=512)

        x1, n2 = _outproj(xf, hm, hr, w_out_bf, norm_ffn_g[l].reshape(1, D), tm=512)

        xf = _ffn(n2, x1, w_up_bf, ffn_conv_w[l], ffn_conv_b[l].reshape(1, -1),
                  w_down_bf, norm_final_g.reshape(1, D), seq=S, tm=512, tf=512,
                  final_norm=(l == depth - 1))
    return xf.reshape(B, S, D)
```

```python
import functools
import math

import jax
import jax.numpy as jnp
from jax import lax
from jax.experimental import pallas as pl
from jax.experimental.pallas import tpu as pltpu

F32 = jnp.float32
BF16 = jnp.bfloat16
EPS = 1e-6
LANES = 128
SUBLANES = 8
VMEM_LIMIT = 56 * 1024 * 1024
VMEM_LIMIT_FFN = 61 * 1024 * 1024

MLSTM_HEADS = 4
LRU_BLOCKS = 8
LRU_CONV = 4
LRU_C = 8.0
FFN_CONV = 3
MLSTM_CHUNK = 256


def _sigmoid(x):
    return 0.5 * jnp.tanh(0.5 * x) + 0.5


def _log_sigmoid(x):
    return jnp.minimum(x, 0.0) - jnp.log1p(jnp.exp(-jnp.abs(x)))


def _softplus(x):
    return jnp.maximum(x, 0.0) + jnp.log1p(jnp.exp(-jnp.abs(x)))


def _gelu_tanh(x):
    c = math.sqrt(2.0 / math.pi)
    t = jnp.tanh(x * (c + (c * 0.044715) * (x * x)))
    hx = 0.5 * x
    return hx + hx * t


def _rmsnorm(x, g):
    return x * lax.rsqrt(jnp.mean(x * x, axis=-1, keepdims=True) + EPS) * g


def _scan_lanes(x, op, fill):
    n = x.shape[-1]
    lane = lax.broadcasted_iota(jnp.int32, x.shape, x.ndim - 1)
    d = 1
    while d < n:
        shifted = jnp.where(lane >= d, pltpu.roll(x, d, axis=x.ndim - 1), fill)
        x = op(x, shifted)
        d *= 2
    return x


_NT = (((1,), (1,)), ((), ()))


def _inproj_kernel(x_ref, g_ref, w_ref, nxt_ref, wif_ref, bif_ref, proj_ref, gate_ref, n1_ref,
                   *, first_shifted, shift):
    n = pl.program_id(1)

    @pl.when(n == 0)
    def _():
        n1 = _rmsnorm(x_ref[...], g_ref[...]).astype(BF16)
        n1_ref[...] = n1
        gate_ref[...] = lax.dot_general(n1, wif_ref[...].astype(BF16), _NT,
                                        preferred_element_type=F32) + bif_ref[...]

    def project(w_bf):
        proj_ref[...] = lax.dot_general(n1_ref[...], w_bf, _NT,
                                        preferred_element_type=F32).astype(proj_ref.dtype)

    @pl.when(n < first_shifted)
    def _():
        project(w_ref[...].astype(BF16))

    @pl.when(n >= first_shifted)
    def _():
        project(jnp.concatenate([w_ref[shift:, :], nxt_ref[...]], axis=0).astype(BF16))


def _inproj(x, g, wt, b_if, *, gate_lo, n_gate, tm, tn):
    T, D = x.shape
    P = wt.shape[0]
    N = P - n_gate
    assert gate_lo % tn == 0 and N % tn == 0 and n_gate == SUBLANES and gate_lo % LANES == 0
    first_shifted = gate_lo // tn
    per = tn // n_gate
    kern = functools.partial(_inproj_kernel, first_shifted=first_shifted, shift=n_gate)
    return pl.pallas_call(
        kern,
        grid=(T // tm, N // tn),
        in_specs=[
            pl.BlockSpec((tm, D), lambda m, n: (m, 0)),
            pl.BlockSpec((1, D), lambda m, n: (0, 0)),
            pl.BlockSpec((tn, D), lambda m, n: (n, 0)),
            pl.BlockSpec((n_gate, D), lambda m, n: (jnp.maximum(n, first_shifted) * per + per, 0)),
            pl.BlockSpec((LANES, D), lambda m, n: (gate_lo // LANES, 0)),
            pl.BlockSpec((1, LANES), lambda m, n: (0, 0)),
        ],
        out_specs=[
            pl.BlockSpec((tm, tn), lambda m, n: (m, n)),
            pl.BlockSpec((tm, LANES), lambda m, n: (m, 0)),
        ],
        out_shape=[
            jax.ShapeDtypeStruct((T, N), BF16),
            jax.ShapeDtypeStruct((T, LANES), F32),
        ],
        scratch_shapes=[pltpu.VMEM((tm, D), BF16)],
        compiler_params=pltpu.CompilerParams(
            dimension_semantics=("arbitrary", "arbitrary"),
            vmem_limit_bytes=VMEM_LIMIT),
        name="inproj",
    )(x, g, wt, wt, wt, b_if)


def _mlstm_kernel(q_ref, k_ref, v_ref, o_ref, gate_ref, hg_ref, wup_ref, out_ref, wup_bf_ref,
                  s_ref, m_ref, tri_ref, sel_ref, *, heads, dh, chunk):
    L = chunk
    scale = dh ** -0.5

    wup_bf_ref[...] = wup_ref[...].astype(BF16)

    @pl.when(pl.program_id(1) == 0)
    def _():
        s_ref[...] = jnp.zeros_like(s_ref)
        m_ref[...] = jnp.zeros_like(m_ref)

    row_i = lax.broadcasted_iota(jnp.int32, (L, L), 0)
    col_i = lax.broadcasted_iota(jnp.int32, (L, L), 1)
    causal = row_i >= col_i

    @pl.when((pl.program_id(0) == 0) & (pl.program_id(1) == 0))
    def _():
        tri_ref[...] = jnp.where(causal, 1.0, 0.0).astype(BF16)
        sr = lax.broadcasted_iota(jnp.int32, sel_ref.shape, 0)
        sc = lax.broadcasted_iota(jnp.int32, sel_ref.shape, 1)
        lane = sr & (LANES - 1)
        blk = sc >> 7
        from_gate = sr < 3 * LANES
        hit_gate = from_gate & (blk < heads) & (lane == blk)
        hit_sum = (~from_gate) & (blk >= heads) & ((lane == blk - heads) | (lane == blk - heads + SUBLANES))
        sel_ref[...] = jnp.where(hit_gate | hit_sum, 1.0, 0.0).astype(BF16)

    def split(x, parts):
        out = []
        for _ in range(parts - 1):
            p = x.astype(BF16)
            out.append(p)
            x = x - p.astype(F32)
        out.append(x.astype(BF16))
        return out

    def rep(x, width):
        return jnp.concatenate([x] * (width // LANES), axis=1)

    g = gate_ref[...]
    tri = tri_ref[...]
    sel = sel_ref[...]

    g8 = g.T[0:SUBLANES, :]
    g8r = pltpu.roll(g8, heads, axis=0)
    sub = lax.broadcasted_iota(jnp.int32, g8.shape, 0)
    ig = jnp.where(sub < heads, g8, g8r)
    fg = jnp.where(sub < heads, g8r, g8)
    logf = jnp.concatenate(split(_log_sigmoid(fg), 2), axis=0)
    br = lax.dot_general(logf, tri, _NT, preferred_element_type=F32)
    b = br[0:SUBLANES, :] + br[SUBLANES:, :]

    logf_pad = jnp.concatenate([logf, jnp.zeros((LANES - 2 * SUBLANES, L), BF16)], axis=0)
    b_nat = lax.dot_general(tri, logf_pad, _NT, preferred_element_type=F32)
    reps = jnp.dot(jnp.concatenate(split(g, 3) + split(b_nat, 3), axis=1), sel,
                   preferred_element_type=F32)
    a_row = ig - b
    b_tot = b[:, L - 1:L]
    m_prev = m_ref[:, 0:1]
    m_new = jnp.maximum(b_tot + m_prev, b_tot + jnp.max(a_row, axis=1, keepdims=True))
    decay = jnp.exp(b_tot + m_prev - m_new)
    end_shift = b_tot - m_new
    m_ref[...] = jnp.broadcast_to(m_new, m_ref.shape)
    ones_blk = jnp.ones((L, LANES), BF16)

    for h in range(heads):
        sl = slice(h * dh, (h + 1) * dh)
        q = q_ref[:, sl] * scale
        k = k_ref[:, sl]
        v = v_ref[:, sl]
        ig_r = reps[:, h * LANES:(h + 1) * LANES]
        b_r = reps[:, (heads + h) * LANES:(heads + h + 1) * LANES]
        a_h = a_row[h:h + 1, :]
        mp = m_prev[h:h + 1, :]

        s = lax.dot_general(q, k, _NT, preferred_element_type=F32)
        cm = jnp.max(jnp.where(causal, a_h, -jnp.inf), axis=-1, keepdims=True)
        mm_r = jnp.maximum(jnp.broadcast_to(cm, (L, LANES)), mp)
        w = s * jnp.exp(jnp.where(causal, a_h - rep(mm_r, L), -jnp.inf))
        einter_r = jnp.exp(mp - mm_r)
        floor_r = jnp.exp(-(b_r + mm_r))
        eend_r = jnp.exp(end_shift[h:h + 1, :] + ig_r - b_r)

        state = s_ref[h]
        intra = jnp.dot(w.astype(BF16), jnp.concatenate([v, ones_blk], axis=1),
                        preferred_element_type=F32)
        inter = jnp.dot(q, state.astype(BF16), preferred_element_type=F32)
        tot = intra + rep(einter_r, dh + LANES) * inter
        num = tot[:, :dh]
        den_r = tot[:, dh:]
        inv_r = 1.0 / jnp.maximum(jnp.abs(den_r), floor_r)
        ssq_r = jnp.broadcast_to(jnp.sum(num * num, axis=-1, keepdims=True), (L, LANES))
        rs_r = inv_r * lax.rsqrt(ssq_r * (inv_r * inv_r) * (1.0 / dh) + EPS)
        gate_o = _sigmoid(o_ref[:, sl].astype(F32))
        out_ref[:, sl] = ((num * rep(rs_r, dh)) * hg_ref[h] * gate_o).astype(out_ref.dtype)

        ev = jnp.concatenate(
            [(rep(eend_r, dh) * v.astype(F32)).astype(BF16), eend_r.astype(BF16)], axis=1)
        loc = lax.dot_general(k, ev, (((0,), (0,)), ((), ())), preferred_element_type=F32)
        s_ref[h] = decay[h:h + 1, :] * state + loc


def _mlstm(proj, gates, head_g, w_up, *, batch, seq, heads, dh, chunk):
    T = batch * seq
    W = heads * dh
    nc = seq // chunk
    steps = batch * nc
    up_rows = w_up.shape[0] // steps
    assert up_rows * steps == w_up.shape[0] and up_rows % 16 == 0
    up_spec = pl.BlockSpec((up_rows, w_up.shape[1]), lambda b, c: (b * nc + c, 0))
    blk = lambda j: pl.BlockSpec((chunk, W), lambda b, c, j=j: (b * nc + c, j))
    kern = functools.partial(_mlstm_kernel, heads=heads, dh=dh, chunk=chunk)
    return pl.pallas_call(
        kern,
        grid=(batch, nc),
        in_specs=[
            blk(0), blk(1), blk(2), blk(3),
            pl.BlockSpec((chunk, LANES), lambda b, c: (b * nc + c, 0)),
            pl.BlockSpec((heads, 1, dh), lambda b, c: (0, 0, 0)),
            up_spec,
        ],
        out_specs=[pl.BlockSpec((chunk, W), lambda b, c: (b * nc + c, 0)), up_spec],
        out_shape=[jax.ShapeDtypeStruct((T, W), BF16), jax.ShapeDtypeStruct(w_up.shape, BF16)],
        scratch_shapes=[
            pltpu.VMEM((heads, dh, dh + LANES), F32),
            pltpu.VMEM((SUBLANES, LANES), F32),
            pltpu.VMEM((chunk, chunk), BF16),
            pltpu.VMEM((6 * LANES, 2 * heads * LANES), BF16),
        ],
        compiler_params=pltpu.CompilerParams(
            dimension_semantics=("arbitrary", "arbitrary"),
            vmem_limit_bytes=VMEM_LIMIT),
        name="mlstm",
    )(proj, proj, proj, proj, gates, head_g, w_up)


def _lru_kernel(xr_ref, gr_ref, cw_ref, cb_ref, wax_ref, ba_ref, bx_ref, lam_ref, wdn_ref, wout_ref,
                out_ref, wdn_bf_ref, wout_bf_ref,
                xtail_ref, hc_ref, perm_ref, unperm_ref, a_scr, u_scr, c_scr, h_scr,
                *, ts, blocks, bd):
    C = blocks * bd
    seg = ts // SUBLANES
    halo = LRU_CONV - 1

    wdn_bf_ref[...] = wdn_ref[...].astype(BF16)
    wout_bf_ref[...] = wout_ref[...].astype(BF16)

    @pl.when(pl.program_id(1) == 0)
    def _():
        xtail_ref[...] = jnp.zeros_like(xtail_ref)
        hc_ref[...] = jnp.zeros_like(hc_ref)

    @pl.when((pl.program_id(0) == 0) & (pl.program_id(1) == 0))
    def _():
        ri = lax.broadcasted_iota(jnp.int32, (ts, ts), 0)
        ci = lax.broadcasted_iota(jnp.int32, (ts, ts), 1)
        t_of_row = (ri & (SUBLANES - 1)) * seg + (ri >> 3)
        t_of_col = (ci & (SUBLANES - 1)) * seg + (ci >> 3)
        perm_ref[...] = jnp.where(ci == t_of_row, 1.0, 0.0).astype(BF16)
        unperm_ref[...] = jnp.where(ri == t_of_col, 1.0, 0.0).astype(BF16)

    xb = xr_ref[...]
    xp = jnp.dot(perm_ref[...], xb, preferred_element_type=F32)
    gp = jnp.dot(perm_ref[...], gr_ref[...], preferred_element_type=F32)
    gelu_g = _gelu_tanh(gp)
    prev = xtail_ref[...]
    xtail_ref[...] = xb[ts - SUBLANES:ts, :].astype(F32)
    row8 = lax.broadcasted_iota(jnp.int32, (SUBLANES, C), 0)

    wrapped = []
    for k in range(1, halo + 1):
        r0 = (seg - k) * SUBLANES
        up = pltpu.roll(xp[r0:r0 + SUBLANES, :], 1, axis=0)
        wrapped.append(jnp.where(row8 == 0, prev[SUBLANES - k:SUBLANES - k + 1, :], up))

    cw = cw_ref[...]
    xc = cb_ref[...] + cw[LRU_CONV - 1:LRU_CONV, :] * xp
    for j in range(1, LRU_CONV):
        xs = jnp.concatenate([wrapped[k - 1] for k in range(j, 0, -1)]
                             + [xp[0:ts - j * SUBLANES, :]], axis=0)
        xc = xc + cw[LRU_CONV - 1 - j:LRU_CONV - j, :] * xs

    xcb = xc.astype(BF16)
    half_xc = 0.5 * xc
    k = (-0.5 * LRU_C * math.log2(math.e)) * _softplus(-lam_ref[...])
    half_ba = 0.5 * ba_ref[...]
    half_bx = 0.5 * bx_ref[...]
    for n in range(blocks):
        sl = slice(n * bd, (n + 1) * bd)
        ax = jnp.dot(xcb[:, sl], wax_ref[n], preferred_element_type=F32)
        t_r = jnp.tanh(ax[:, :bd] + half_ba[:, sl])
        t_i = jnp.tanh(ax[:, bd:] + half_bx[:, sl])
        a = jnp.exp2(k[:, sl] + k[:, sl] * t_r)
        ixc = half_xc[:, sl] + half_xc[:, sl] * t_i
        y = 1.0 - a * a
        a_scr[:, sl] = a
        u_scr[:, sl] = jnp.where(y > 0.0, y * lax.rsqrt(y), 0.0) * ixc

    def body(v, carry):
        h, c = carry
        rows = pl.ds(pl.multiple_of(v * SUBLANES, SUBLANES), SUBLANES)
        av = a_scr[rows, :]
        h = av * h + u_scr[rows, :]
        c = av * c
        h_scr[rows, :] = h
        c_scr[rows, :] = c
        return h, c

    init = (jnp.zeros((SUBLANES, C), F32), jnp.ones((SUBLANES, C), F32))
    h_end, c_end = lax.fori_loop(0, seg, body, init, unroll=4)

    state = hc_ref[0:1, :]
    entering = []
    for s in range(SUBLANES):
        entering.append(state)
        state = c_end[s:s + 1, :] * state + h_end[s:s + 1, :]
    hc_ref[0:1, :] = state
    entering = jnp.concatenate(entering, axis=0)

    h = (h_scr[...].reshape(seg, SUBLANES, C)
         + c_scr[...].reshape(seg, SUBLANES, C) * entering[None]).reshape(ts, C)
    outp = (h * gelu_g).astype(BF16)
    out_ref[...] = jnp.dot(unperm_ref[...], outp,
                           preferred_element_type=F32).astype(out_ref.dtype)


def _rglru(proj, conv_w, conv_b, w_ax, ba, bx, lam, w_down, w_out, *, batch, seq, col_block, ts):
    T = batch * seq
    blocks, bd, _ = w_ax.shape
    C = blocks * bd
    ns = seq // ts
    steps = batch * ns

    def slab(w):
        rows = w.shape[0] // steps
        assert rows * steps == w.shape[0] and rows % 16 == 0
        return pl.BlockSpec((rows, w.shape[1]), lambda b, s: (b * ns + s, 0))
    vec = lambda rows: pl.BlockSpec((rows, C), lambda b, s: (0, 0))
    assert ts % (SUBLANES * SUBLANES) == 0 and bd == LANES
    kern = functools.partial(_lru_kernel, ts=ts, blocks=blocks, bd=bd)
    return pl.pallas_call(
        kern,
        grid=(batch, ns),
        in_specs=[
            pl.BlockSpec((ts, C), lambda b, s: (b * ns + s, col_block)),
            pl.BlockSpec((ts, C), lambda b, s: (b * ns + s, col_block + 1)),
            vec(LRU_CONV), vec(1),
            pl.BlockSpec((blocks, bd, 2 * bd), lambda b, s: (0, 0, 0)),
            vec(1), vec(1), vec(1),
            slab(w_down), slab(w_out),
        ],
        out_specs=[pl.BlockSpec((ts, C), lambda b, s: (b * ns + s, 0)), slab(w_down), slab(w_out)],
        out_shape=[jax.ShapeDtypeStruct((T, C), BF16),
                   jax.ShapeDtypeStruct(w_down.shape, BF16),
                   jax.ShapeDtypeStruct(w_out.shape, BF16)],
        scratch_shapes=[
            pltpu.VMEM((SUBLANES, C), F32),
            pltpu.VMEM((SUBLANES, C), F32),
            pltpu.VMEM((ts, ts), BF16),
            pltpu.VMEM((ts, ts), BF16),
        ] + [pltpu.VMEM((ts, C), F32)] * 4,
        compiler_params=pltpu.CompilerParams(
            dimension_semantics=("arbitrary", "arbitrary"),
            vmem_limit_bytes=VMEM_LIMIT),
        name="rglru",
    )(proj, proj, conv_w, conv_b, w_ax, ba, bx, lam, w_down, w_out)


def _outproj_kernel(x_ref, hm_ref, hr_ref, wm_ref, wr_ref, g_ref, x1_ref, n2_ref):
    x1 = (x_ref[...]
          + jnp.dot(hm_ref[...], wm_ref[...], preferred_element_type=F32)
          + jnp.dot(hr_ref[...], wr_ref[...], preferred_element_type=F32))
    x1_ref[...] = x1
    n2_ref[...] = _rmsnorm(x1, g_ref[...]).astype(n2_ref.dtype)


def _outproj(x, hm, hr, w_out, g, *, tm):
    T, D = x.shape
    Wm = hm.shape[1]
    Wr = hr.shape[1]
    return pl.pallas_call(
        _outproj_kernel,
        grid=(T // tm,),
        in_specs=[
            pl.BlockSpec((tm, D), lambda m: (m, 0)),
            pl.BlockSpec((tm, Wm), lambda m: (m, 0)),
            pl.BlockSpec((tm, Wr), lambda m: (m, 0)),
            pl.BlockSpec((Wm, D), lambda m: (0, 0)),
            pl.BlockSpec((Wr, D), lambda m: (Wm // Wr, 0)),
            pl.BlockSpec((1, D), lambda m: (0, 0)),
        ],
        out_specs=[
            pl.BlockSpec((tm, D), lambda m: (m, 0)),
            pl.BlockSpec((tm, D), lambda m: (m, 0)),
        ],
        out_shape=[
            jax.ShapeDtypeStruct((T, D), F32),
            jax.ShapeDtypeStruct((T, D), BF16),
        ],
        compiler_params=pltpu.CompilerParams(
            dimension_semantics=("arbitrary",),
            vmem_limit_bytes=VMEM_LIMIT),
        name="outproj",
    )(x, hm, hr, w_out, w_out, g)


def _ffn_kernel(n2_ref, x1_ref, wg_ref, wu_ref, cw_ref, cb_ref, wd_ref, gf_ref, y_ref,
                tail_ref, *, tm, tiles_per_seq, final_norm):
    m = pl.program_id(0)
    f = pl.program_id(1)
    nf = pl.num_programs(1)

    @pl.when(f == 0)
    def _():
        y_ref[...] = x1_ref[...]

    n2 = n2_ref[...]
    gate = jnp.dot(n2, wg_ref[...], preferred_element_type=F32)
    up = jnp.dot(n2, wu_ref[...], preferred_element_type=F32)

    prev = jnp.where(m % tiles_per_seq == 0, 0.0, tail_ref[f])
    tail_ref[f] = gate[tm - SUBLANES:tm, :]

    row8 = lax.broadcasted_iota(jnp.int32, prev.shape, 0)
    cw = cw_ref[...]
    conv = cb_ref[...] + cw[FFN_CONV - 1:FFN_CONV, :] * gate
    for j in range(1, FFN_CONV):
        r = pltpu.roll(gate, j, axis=0)
        top = jnp.where(row8 < j, pltpu.roll(prev, j, axis=0), r[0:SUBLANES, :])
        gs = jnp.concatenate([top, r[SUBLANES:, :]], axis=0)
        conv = conv + cw[FFN_CONV - 1 - j:FFN_CONV - j, :] * gs
    hid = (conv * _sigmoid(conv)) * up
    hid_b = hid.astype(BF16)
    tf = hid_b.shape[1]
    for c in range(y_ref.shape[1] // tf):
        cols = slice(c * tf, (c + 1) * tf)
        y_ref[:, cols] += jnp.dot(hid_b, wd_ref[:, cols], preferred_element_type=F32)

    if final_norm:
        @pl.when(f == nf - 1)
        def _():
            y_ref[...] = _rmsnorm(y_ref[...], gf_ref[...])


def _ffn(n2, x1, w_up, conv_w, conv_b, w_down, gf, *, seq, tm, tf, final_norm):
    T, D = n2.shape
    F = w_down.shape[0]
    nf = F // tf
    kern = functools.partial(_ffn_kernel, tm=tm, tiles_per_seq=seq // tm, final_norm=final_norm)
    return pl.pallas_call(
        kern,
        grid=(T // tm, nf),
        in_specs=[
            pl.BlockSpec((tm, D), lambda m, f: (m, 0)),
            pl.BlockSpec((tm, D), lambda m, f: (m, 0)),
            pl.BlockSpec((D, tf), lambda m, f: (0, f)),
            pl.BlockSpec((D, tf), lambda m, f: (0, nf + f)),
            pl.BlockSpec((FFN_CONV, tf), lambda m, f: (0, f)),
            pl.BlockSpec((1, tf), lambda m, f: (0, f)),
            pl.BlockSpec((tf, D), lambda m, f: (f, 0)),
            pl.BlockSpec((1, D), lambda m, f: (0, 0)),
        ],
        out_specs=pl.BlockSpec((tm, D), lambda m, f: (m, 0)),
        out_shape=jax.ShapeDtypeStruct((T, D), F32),
        scratch_shapes=[pltpu.VMEM((nf, SUBLANES, tf), F32)],
        compiler_params=pltpu.CompilerParams(
            dimension_semantics=("arbitrary", "arbitrary"),
            vmem_limit_bytes=VMEM_LIMIT_FFN),
        name="convffn",
    )(n2, x1, w_up, w_up, conv_w, conv_b, w_down, gf)


def kernel(x, norm_mix_g, w_in, b_gate_m, mlstm_norm_g, lru_conv_w, lru_conv_b, lru_wa, lru_ba,
           lru_wx, lru_bx, lru_lambda, w_out, norm_ffn_g, w_up, ffn_conv_w, ffn_conv_b, w_down,
           norm_final_g):
    B, S, D = x.shape
    T = B * S
    depth = w_in.shape[0]
    H = MLSTM_HEADS
    dh = mlstm_norm_g.shape[-1]
    d_m = H * dh
    d_r = lru_conv_w.shape[-1]
    assert d_m == d_r, "column-block addressing assumes equal head-group widths"
    n_gate = 2 * H
    gate_lo = 4 * d_m

    xf = x.reshape(T, D)
    for l in range(depth):
        b_if = jnp.pad(b_gate_m[l].astype(F32), (0, LANES - n_gate)).reshape(1, LANES)
        proj, gates = _inproj(xf, norm_mix_g[l].reshape(1, D), jnp.swapaxes(w_in[l], 0, 1), b_if,
                              gate_lo=gate_lo, n_gate=n_gate, tm=1024, tn=1024)

        hm, w_up_bf = _mlstm(proj, gates, mlstm_norm_g[l].reshape(H, 1, dh), w_up[l],
                             batch=B, seq=S, heads=H, dh=dh, chunk=MLSTM_CHUNK)

        w_ax = (0.5 * jnp.concatenate([lru_wa[l], lru_wx[l]], axis=-1)).astype(BF16)
        hr, w_down_bf, w_out_bf = _rglru(
            proj, lru_conv_w[l], lru_conv_b[l].reshape(1, d_r), w_ax,
            lru_ba[l].reshape(1, d_r), lru_bx[l].reshape(1, d_r), lru_lambda[l].reshape(1, d_r),
            w_down[l], w_out[l], batch=B, seq=S, col_block=gate_lo // d_r, ts=512)

        x1, n2 = _outproj(xf, hm, hr, w_out_bf, norm_ffn_g[l].reshape(1, D), tm=512)

        xf = _ffn(n2, x1, w_up_bf, ffn_conv_w[l], ffn_conv_b[l].reshape(1, -1),
                  w_down_bf, norm_final_g.reshape(1, D), seq=S, tm=1024, tf=512,
                  final_norm=(l == depth - 1))
    return xf.reshape(B, S, D)
```

```python
import functools
import math

import jax
import jax.numpy as jnp
from jax import lax
from jax.experimental import pallas as pl
from jax.experimental.pallas import tpu as pltpu

F32 = jnp.float32
BF16 = jnp.bfloat16
EPS = 1e-6
LANES = 128
SUBLANES = 8
VMEM_LIMIT = 56 * 1024 * 1024
VMEM_LIMIT_FFN = 61 * 1024 * 1024

MLSTM_HEADS = 4
LRU_BLOCKS = 8
LRU_CONV = 4
LRU_C = 8.0
FFN_CONV = 3
MIXER_ROWS = 256


def _sigmoid(x):
    return 0.5 * jnp.tanh(0.5 * x) + 0.5


def _log_sigmoid(x):
    return jnp.minimum(x, 0.0) - jnp.log1p(jnp.exp(-jnp.abs(x)))


def _softplus(x):
    return jnp.maximum(x, 0.0) + jnp.log1p(jnp.exp(-jnp.abs(x)))


def _gelu_tanh(x):
    c = math.sqrt(2.0 / math.pi)
    t = jnp.tanh(x * (c + (c * 0.044715) * (x * x)))
    hx = 0.5 * x
    return hx + hx * t


def _rmsnorm(x, g):
    return x * lax.rsqrt(jnp.mean(x * x, axis=-1, keepdims=True) + EPS) * g


def _scan_lanes(x, op, fill):
    n = x.shape[-1]
    lane = lax.broadcasted_iota(jnp.int32, x.shape, x.ndim - 1)
    d = 1
    while d < n:
        shifted = jnp.where(lane >= d, pltpu.roll(x, d, axis=x.ndim - 1), fill)
        x = op(x, shifted)
        d *= 2
    return x


_NT = (((1,), (1,)), ((), ()))


def _inproj_kernel(x_ref, g_ref, w_ref, nxt_ref, wif_ref, bif_ref, proj_ref, gate_ref, n1_ref,
                   *, first_shifted, shift):
    n = pl.program_id(1)

    @pl.when(n == 0)
    def _():
        n1 = _rmsnorm(x_ref[...], g_ref[...]).astype(BF16)
        n1_ref[...] = n1
        gate_ref[...] = lax.dot_general(n1, wif_ref[...].astype(BF16), _NT,
                                        preferred_element_type=F32) + bif_ref[...]

    def project(w_bf):
        proj_ref[...] = lax.dot_general(n1_ref[...], w_bf, _NT,
                                        preferred_element_type=F32).astype(proj_ref.dtype)

    @pl.when(n < first_shifted)
    def _():
        project(w_ref[...].astype(BF16))

    @pl.when(n >= first_shifted)
    def _():
        project(jnp.concatenate([w_ref[shift:, :], nxt_ref[...]], axis=0).astype(BF16))


def _inproj(x, g, wt, b_if, *, gate_lo, n_gate, tm, tn):
    T, D = x.shape
    P = wt.shape[0]
    N = P - n_gate
    assert gate_lo % tn == 0 and N % tn == 0 and n_gate == SUBLANES and gate_lo % LANES == 0
    first_shifted = gate_lo // tn
    per = tn // n_gate
    kern = functools.partial(_inproj_kernel, first_shifted=first_shifted, shift=n_gate)
    return pl.pallas_call(
        kern,
        grid=(T // tm, N // tn),
        in_specs=[
            pl.BlockSpec((tm, D), lambda m, n: (m, 0)),
            pl.BlockSpec((1, D), lambda m, n: (0, 0)),
            pl.BlockSpec((tn, D), lambda m, n: (n, 0)),
            pl.BlockSpec((n_gate, D), lambda m, n: (jnp.maximum(n, first_shifted) * per + per, 0)),
            pl.BlockSpec((LANES, D), lambda m, n: (gate_lo // LANES, 0)),
            pl.BlockSpec((1, LANES), lambda m, n: (0, 0)),
        ],
        out_specs=[
            pl.BlockSpec((tm, tn), lambda m, n: (m, n)),
            pl.BlockSpec((tm, LANES), lambda m, n: (m, 0)),
        ],
        out_shape=[
            jax.ShapeDtypeStruct((T, N), BF16),
            jax.ShapeDtypeStruct((T, LANES), F32),
        ],
        scratch_shapes=[pltpu.VMEM((tm, D), BF16)],
        compiler_params=pltpu.CompilerParams(
            dimension_semantics=("arbitrary", "arbitrary"),
            vmem_limit_bytes=VMEM_LIMIT),
        name="inproj",
    )(x, g, wt, wt, wt, b_if)


def _causal_mask(L):
    return (lax.broadcasted_iota(jnp.int32, (L, L), 0)
            >= lax.broadcasted_iota(jnp.int32, (L, L), 1))


def _mlstm_init(s_ref, m_ref, tri_ref, sel_ref, *, heads, chunk):
    @pl.when(pl.program_id(1) == 0)
    def _():
        s_ref[...] = jnp.zeros_like(s_ref)
        m_ref[...] = jnp.zeros_like(m_ref)

    @pl.when((pl.program_id(0) == 0) & (pl.program_id(1) == 0))
    def _():
        tri_ref[...] = jnp.where(_causal_mask(chunk), 1.0, 0.0).astype(BF16)
        sr = lax.broadcasted_iota(jnp.int32, sel_ref.shape, 0)
        sc = lax.broadcasted_iota(jnp.int32, sel_ref.shape, 1)
        lane = sr & (LANES - 1)
        blk = sc >> 7
        from_gate = sr < 3 * LANES
        hit_gate = from_gate & (blk < heads) & (lane == blk)
        hit_sum = (~from_gate) & (blk >= heads) & ((lane == blk - heads) | (lane == blk - heads + SUBLANES))
        sel_ref[...] = jnp.where(hit_gate | hit_sum, 1.0, 0.0).astype(BF16)


def _mlstm_main(q_ref, k_ref, v_ref, o_ref, gate_ref, hg_ref, out_ref,
                s_ref, m_ref, tri_ref, sel_ref, *, heads, dh, chunk):
    L = chunk
    scale = dh ** -0.5
    causal = _causal_mask(L)

    def split(x, parts):
        out = []
        for _ in range(parts - 1):
            p = x.astype(BF16)
            out.append(p)
            x = x - p.astype(F32)
        out.append(x.astype(BF16))
        return out

    def rep(x, width):
        return jnp.concatenate([x] * (width // LANES), axis=1)

    g = gate_ref[...]
    tri = tri_ref[...]
    sel = sel_ref[...]

    g8 = g.T[0:SUBLANES, :]
    g8r = pltpu.roll(g8, heads, axis=0)
    sub = lax.broadcasted_iota(jnp.int32, g8.shape, 0)
    ig = jnp.where(sub < heads, g8, g8r)
    fg = jnp.where(sub < heads, g8r, g8)
    logf = jnp.concatenate(split(_log_sigmoid(fg), 2), axis=0)
    br = lax.dot_general(logf, tri, _NT, preferred_element_type=F32)
    b = br[0:SUBLANES, :] + br[SUBLANES:, :]

    logf_pad = jnp.concatenate([logf, jnp.zeros((LANES - 2 * SUBLANES, L), BF16)], axis=0)
    b_nat = lax.dot_general(tri, logf_pad, _NT, preferred_element_type=F32)
    reps = jnp.dot(jnp.concatenate(split(g, 3) + split(b_nat, 3), axis=1), sel,
                   preferred_element_type=F32)
    a_row = ig - b
    b_tot = b[:, L - 1:L]
    m_prev = m_ref[:, 0:1]
    m_new = jnp.maximum(b_tot + m_prev, b_tot + jnp.max(a_row, axis=1, keepdims=True))
    decay = jnp.exp(b_tot + m_prev - m_new)
    end_shift = b_tot - m_new
    m_ref[...] = jnp.broadcast_to(m_new, m_ref.shape)
    ones_blk = jnp.ones((L, LANES), BF16)

    for h in range(heads):
        sl = slice(h * dh, (h + 1) * dh)
        q = q_ref[:, sl] * scale
        k = k_ref[:, sl]
        v = v_ref[:, sl]
        ig_r = reps[:, h * LANES:(h + 1) * LANES]
        b_r = reps[:, (heads + h) * LANES:(heads + h + 1) * LANES]
        a_h = a_row[h:h + 1, :]
        mp = m_prev[h:h + 1, :]

        s = lax.dot_general(q, k, _NT, preferred_element_type=F32)
        cm = jnp.max(jnp.where(causal, a_h, -jnp.inf), axis=-1, keepdims=True)
        mm_r = jnp.maximum(jnp.broadcast_to(cm, (L, LANES)), mp)
        w = s * jnp.exp(jnp.where(causal, a_h - rep(mm_r, L), -jnp.inf))
        einter_r = jnp.exp(mp - mm_r)
        floor_r = jnp.exp(-(b_r + mm_r))
        eend_r = jnp.exp(end_shift[h:h + 1, :] + ig_r - b_r)

        state = s_ref[h]
        intra = jnp.dot(w.astype(BF16), jnp.concatenate([v, ones_blk], axis=1),
                        preferred_element_type=F32)
        inter = jnp.dot(q, state.astype(BF16), preferred_element_type=F32)
        tot = intra + rep(einter_r, dh + LANES) * inter
        num = tot[:, :dh]
        den_r = tot[:, dh:]
        inv_r = 1.0 / jnp.maximum(jnp.abs(den_r), floor_r)
        ssq_r = jnp.broadcast_to(jnp.sum(num * num, axis=-1, keepdims=True), (L, LANES))
        rs_r = inv_r * lax.rsqrt(ssq_r * (inv_r * inv_r) * (1.0 / dh) + EPS)
        gate_o = _sigmoid(o_ref[:, sl].astype(F32))
        out_ref[:, sl] = ((num * rep(rs_r, dh)) * hg_ref[h] * gate_o).astype(out_ref.dtype)

        ev = jnp.concatenate(
            [(rep(eend_r, dh) * v.astype(F32)).astype(BF16), eend_r.astype(BF16)], axis=1)
        loc = lax.dot_general(k, ev, (((0,), (0,)), ((), ())), preferred_element_type=F32)
        s_ref[h] = decay[h:h + 1, :] * state + loc


def _lru_init(xtail_ref, hc_ref, perm_ref, unperm_ref, *, ts):
    seg = ts // SUBLANES

    @pl.when(pl.program_id(1) == 0)
    def _():
        xtail_ref[...] = jnp.zeros_like(xtail_ref)
        hc_ref[...] = jnp.zeros_like(hc_ref)

    @pl.when((pl.program_id(0) == 0) & (pl.program_id(1) == 0))
    def _():
        ri = lax.broadcasted_iota(jnp.int32, (ts, ts), 0)
        ci = lax.broadcasted_iota(jnp.int32, (ts, ts), 1)
        t_of_row = (ri & (SUBLANES - 1)) * seg + (ri >> 3)
        t_of_col = (ci & (SUBLANES - 1)) * seg + (ci >> 3)
        perm_ref[...] = jnp.where(ci == t_of_row, 1.0, 0.0).astype(BF16)
        unperm_ref[...] = jnp.where(ri == t_of_col, 1.0, 0.0).astype(BF16)


def _lru_main(xr_ref, gr_ref, cw_ref, cb_ref, wax_ref, ba_ref, bx_ref, lam_ref, out_ref,
              xtail_ref, hc_ref, perm_ref, unperm_ref, a_scr, u_scr, c_scr, h_scr,
              *, ts, blocks, bd):
    C = blocks * bd
    seg = ts // SUBLANES
    halo = LRU_CONV - 1

    xb = xr_ref[...]
    xp = jnp.dot(perm_ref[...], xb, preferred_element_type=F32)
    gp = jnp.dot(perm_ref[...], gr_ref[...], preferred_element_type=F32)
    gelu_g = _gelu_tanh(gp)
    prev = xtail_ref[...]
    xtail_ref[...] = xb[ts - SUBLANES:ts, :].astype(F32)
    row8 = lax.broadcasted_iota(jnp.int32, (SUBLANES, C), 0)

    wrapped = []
    for k in range(1, halo + 1):
        r0 = (seg - k) * SUBLANES
        up = pltpu.roll(xp[r0:r0 + SUBLANES, :], 1, axis=0)
        wrapped.append(jnp.where(row8 == 0, prev[SUBLANES - k:SUBLANES - k + 1, :], up))

    cw = cw_ref[...]
    xc = cb_ref[...] + cw[LRU_CONV - 1:LRU_CONV, :] * xp
    for j in range(1, LRU_CONV):
        xs = jnp.concatenate([wrapped[k - 1] for k in range(j, 0, -1)]
                             + [xp[0:ts - j * SUBLANES, :]], axis=0)
        xc = xc + cw[LRU_CONV - 1 - j:LRU_CONV - j, :] * xs

    xcb = xc.astype(BF16)
    half_xc = 0.5 * xc
    k = (-0.5 * LRU_C * math.log2(math.e)) * _softplus(-lam_ref[...])
    half_ba = 0.5 * ba_ref[...]
    half_bx = 0.5 * bx_ref[...]
    for n in range(blocks):
        sl = slice(n * bd, (n + 1) * bd)
        ax = jnp.dot(xcb[:, sl], wax_ref[n], preferred_element_type=F32)
        t_r = jnp.tanh(ax[:, :bd] + half_ba[:, sl])
        t_i = jnp.tanh(ax[:, bd:] + half_bx[:, sl])
        a = jnp.exp2(k[:, sl] + k[:, sl] * t_r)
        ixc = half_xc[:, sl] + half_xc[:, sl] * t_i
        y = 1.0 - a * a
        a_scr[:, sl] = a
        u_scr[:, sl] = jnp.where(y > 0.0, y * lax.rsqrt(y), 0.0) * ixc

    def body(v, carry):
        h, c = carry
        rows = pl.ds(pl.multiple_of(v * SUBLANES, SUBLANES), SUBLANES)
        av = a_scr[rows, :]
        h = av * h + u_scr[rows, :]
        c = av * c
        h_scr[rows, :] = h
        c_scr[rows, :] = c
        return h, c

    init = (jnp.zeros((SUBLANES, C), F32), jnp.ones((SUBLANES, C), F32))
    h_end, c_end = lax.fori_loop(0, seg, body, init, unroll=4)

    state = hc_ref[0:1, :]
    entering = []
    for s in range(SUBLANES):
        entering.append(state)
        state = c_end[s:s + 1, :] * state + h_end[s:s + 1, :]
    hc_ref[0:1, :] = state
    entering = jnp.concatenate(entering, axis=0)

    h = (h_scr[...].reshape(seg, SUBLANES, C)
         + c_scr[...].reshape(seg, SUBLANES, C) * entering[None]).reshape(ts, C)
    outp = (h * gelu_g).astype(BF16)
    out_ref[...] = jnp.dot(unperm_ref[...], outp,
                           preferred_element_type=F32).astype(out_ref.dtype)


def _mixers_kernel(q_ref, k_ref, v_ref, o_ref, gate_ref, hg_ref,
                   xr_ref, gr_ref, cw_ref, cb_ref, wax_ref, ba_ref, bx_ref, lam_ref,
                   wup_ref, wdn_ref, wout_ref,
                   hm_ref, hr_ref, wup_bf_ref, wdn_bf_ref, wout_bf_ref,
                   s_ref, m_ref, tri_ref, sel_ref,
                   xtail_ref, hc_ref, perm_ref, unperm_ref, a_scr, u_scr, c_scr, h_scr,
                   *, heads, dh, rows, blocks, bd):
    _mlstm_init(s_ref, m_ref, tri_ref, sel_ref, heads=heads, chunk=rows)
    _lru_init(xtail_ref, hc_ref, perm_ref, unperm_ref, ts=rows)

    wup_bf_ref[...] = wup_ref[...].astype(BF16)
    wdn_bf_ref[...] = wdn_ref[...].astype(BF16)
    wout_bf_ref[...] = wout_ref[...].astype(BF16)

    _mlstm_main(q_ref, k_ref, v_ref, o_ref, gate_ref, hg_ref, hm_ref,
                s_ref, m_ref, tri_ref, sel_ref, heads=heads, dh=dh, chunk=rows)
    _lru_main(xr_ref, gr_ref, cw_ref, cb_ref, wax_ref, ba_ref, bx_ref, lam_ref, hr_ref,
              xtail_ref, hc_ref, perm_ref, unperm_ref, a_scr, u_scr, c_scr, h_scr,
              ts=rows, blocks=blocks, bd=bd)


def _mixers(proj, gates, head_g, conv_w, conv_b, w_ax, ba, bx, lam, w_up, w_down, w_out,
            *, batch, seq, heads, dh, rows):
    T = batch * seq
    W = heads * dh
    blocks, bd, _ = w_ax.shape
    C = blocks * bd
    assert W == C and bd == LANES and rows % (SUBLANES * SUBLANES) == 0
    ns = seq // rows
    steps = batch * ns
    row_blk = lambda b, s: b * ns + s

    def slab(w):
        n = w.shape[0] // steps
        assert n * steps == w.shape[0] and n % 16 == 0
        return pl.BlockSpec((n, w.shape[1]), lambda b, s: (row_blk(b, s), 0))

    col = lambda j: pl.BlockSpec((rows, W), lambda b, s, j=j: (row_blk(b, s), j))
    vec = lambda n: pl.BlockSpec((n, C), lambda b, s: (0, 0))
    kern = functools.partial(_mixers_kernel, heads=heads, dh=dh, rows=rows, blocks=blocks, bd=bd)
    return pl.pallas_call(
        kern,
        grid=(batch, ns),
        in_specs=[
            col(0), col(1), col(2), col(3),
            pl.BlockSpec((rows, LANES), lambda b, s: (row_blk(b, s), 0)),
            pl.BlockSpec((heads, 1, dh), lambda b, s: (0, 0, 0)),
            col(4), col(5),
            vec(LRU_CONV), vec(1),
            pl.BlockSpec((blocks, bd, 2 * bd), lambda b, s: (0, 0, 0)),
            vec(1), vec(1), vec(1),
            slab(w_up), slab(w_down), slab(w_out),
        ],
        out_specs=[col(0), col(0), slab(w_up), slab(w_down), slab(w_out)],
        out_shape=[jax.ShapeDtypeStruct((T, W), BF16),
                   jax.ShapeDtypeStruct((T, C), BF16),
                   jax.ShapeDtypeStruct(w_up.shape, BF16),
                   jax.ShapeDtypeStruct(w_down.shape, BF16),
                   jax.ShapeDtypeStruct(w_out.shape, BF16)],
        scratch_shapes=[
            pltpu.VMEM((heads, dh, dh + LANES), F32),
            pltpu.VMEM((SUBLANES, LANES), F32),
            pltpu.VMEM((rows, rows), BF16),
            pltpu.VMEM((6 * LANES, 2 * heads * LANES), BF16),
            pltpu.VMEM((SUBLANES, C), F32),
            pltpu.VMEM((SUBLANES, C), F32),
            pltpu.VMEM((rows, rows), BF16),
            pltpu.VMEM((rows, rows), BF16),
        ] + [pltpu.VMEM((rows, C), F32)] * 4,
        compiler_params=pltpu.CompilerParams(
            dimension_semantics=("arbitrary", "arbitrary"),
            vmem_limit_bytes=VMEM_LIMIT),
        name="mixers",
    )(proj, proj, proj, proj, gates, head_g, proj, proj, conv_w, conv_b, w_ax, ba, bx, lam,
      w_up, w_down, w_out)


def _outproj_kernel(x_ref, hm_ref, hr_ref, wm_ref, wr_ref, g_ref, x1_ref, n2_ref):
    x1 = (x_ref[...]
          + jnp.dot(hm_ref[...], wm_ref[...], preferred_element_type=F32)
          + jnp.dot(hr_ref[...], wr_ref[...], preferred_element_type=F32))
    x1_ref[...] = x1
    n2_ref[...] = _rmsnorm(x1, g_ref[...]).astype(n2_ref.dtype)


def _outproj(x, hm, hr, w_out, g, *, tm):
    T, D = x.shape
    Wm = hm.shape[1]
    Wr = hr.shape[1]
    return pl.pallas_call(
        _outproj_kernel,
        grid=(T // tm,),
        in_specs=[
            pl.BlockSpec((tm, D), lambda m: (m, 0)),
            pl.BlockSpec((tm, Wm), lambda m: (m, 0)),
            pl.BlockSpec((tm, Wr), lambda m: (m, 0)),
            pl.BlockSpec((Wm, D), lambda m: (0, 0)),
            pl.BlockSpec((Wr, D), lambda m: (Wm // Wr, 0)),
            pl.BlockSpec((1, D), lambda m: (0, 0)),
        ],
        out_specs=[
            pl.BlockSpec((tm, D), lambda m: (m, 0)),
            pl.BlockSpec((tm, D), lambda m: (m, 0)),
        ],
        out_shape=[
            jax.ShapeDtypeStruct((T, D), F32),
            jax.ShapeDtypeStruct((T, D), BF16),
        ],
        compiler_params=pltpu.CompilerParams(
            dimension_semantics=("arbitrary",),
            vmem_limit_bytes=VMEM_LIMIT),
        name="outproj",
    )(x, hm, hr, w_out, w_out, g)


def _ffn_kernel(n2_ref, x1_ref, wg_ref, wu_ref, cw_ref, cb_ref, wd_ref, gf_ref, y_ref,
                tail_ref, *, tm, tiles_per_seq, final_norm):
    m = pl.program_id(0)
    f = pl.program_id(1)
    nf = pl.num_programs(1)

    @pl.when(f == 0)
    def _():
        y_ref[...] = x1_ref[...]

    n2 = n2_ref[...]
    gate = jnp.dot(n2, wg_ref[...], preferred_element_type=F32)
    up = jnp.dot(n2, wu_ref[...], preferred_element_type=F32)

    prev = jnp.where(m % tiles_per_seq == 0, 0.0, tail_ref[f])
    tail_ref[f] = gate[tm - SUBLANES:tm, :]

    row8 = lax.broadcasted_iota(jnp.int32, prev.shape, 0)
    cw = cw_ref[...]
    conv = cb_ref[...] + cw[FFN_CONV - 1:FFN_CONV, :] * gate
    for j in range(1, FFN_CONV):
        r = pltpu.roll(gate, j, axis=0)
        top = jnp.where(row8 < j, pltpu.roll(prev, j, axis=0), r[0:SUBLANES, :])
        gs = jnp.concatenate([top, r[SUBLANES:, :]], axis=0)
        conv = conv + cw[FFN_CONV - 1 - j:FFN_CONV - j, :] * gs
    hid = (conv * _sigmoid(conv)) * up
    hid_b = hid.astype(BF16)
    tf = hid_b.shape[1]
    for c in range(y_ref.shape[1] // tf):
        cols = slice(c * tf, (c + 1) * tf)
        y_ref[:, cols] += jnp.dot(hid_b, wd_ref[:, cols], preferred_element_type=F32)

    if final_norm:
        @pl.when(f == nf - 1)
        def _():
            y_ref[...] = _rmsnorm(y_ref[...], gf_ref[...])


def _ffn(n2, x1, w_up, conv_w, conv_b, w_down, gf, *, seq, tm, tf, final_norm):
    T, D = n2.shape
    F = w_down.shape[0]
    nf = F // tf
    kern = functools.partial(_ffn_kernel, tm=tm, tiles_per_seq=seq // tm, final_norm=final_norm)
    return pl.pallas_call(
        kern,
        grid=(T // tm, nf),
        in_specs=[
            pl.BlockSpec((tm, D), lambda m, f: (m, 0)),
            pl.BlockSpec((tm, D), lambda m, f: (m, 0)),
            pl.BlockSpec((D, tf), lambda m, f: (0, f)),
            pl.BlockSpec((D, tf), lambda m, f: (0, nf + f)),
            pl.BlockSpec((FFN_CONV, tf), lambda m, f: (0, f)),
            pl.BlockSpec((1, tf), lambda m, f: (0, f)),
            pl.BlockSpec((tf, D), lambda m, f: (f, 0)),
            pl.BlockSpec((1, D), lambda m, f: (0, 0)),
        ],
        out_specs=pl.BlockSpec((tm, D), lambda m, f: (m, 0)),
        out_shape=jax.ShapeDtypeStruct((T, D), F32),
        scratch_shapes=[pltpu.VMEM((nf, SUBLANES, tf), F32)],
        compiler_params=pltpu.CompilerParams(
            dimension_semantics=("arbitrary", "arbitrary"),
            vmem_limit_bytes=VMEM_LIMIT_FFN),
        name="convffn",
    )(n2, x1, w_up, w_up, conv_w, conv_b, w_down, gf)


def kernel(x, norm_mix_g, w_in, b_gate_m, mlstm_norm_g, lru_conv_w, lru_conv_b, lru_wa, lru_ba,
           lru_wx, lru_bx, lru_lambda, w_out, norm_ffn_g, w_up, ffn_conv_w, ffn_conv_b, w_down,
           norm_final_g):
    B, S, D = x.shape
    T = B * S
    depth = w_in.shape[0]
    H = MLSTM_HEADS
    dh = mlstm_norm_g.shape[-1]
    d_m = H * dh
    d_r = lru_conv_w.shape[-1]
    assert d_m == d_r, "column-block addressing assumes equal head-group widths"
    n_gate = 2 * H
    gate_lo = 4 * d_m

    xf = x.reshape(T, D)
    for l in range(depth):
        b_if = jnp.pad(b_gate_m[l].astype(F32), (0, LANES - n_gate)).reshape(1, LANES)
        proj, gates = _inproj(xf, norm_mix_g[l].reshape(1, D), jnp.swapaxes(w_in[l], 0, 1), b_if,
                              gate_lo=gate_lo, n_gate=n_gate, tm=1024, tn=1024)

        w_ax = (0.5 * jnp.concatenate([lru_wa[l], lru_wx[l]], axis=-1)).astype(BF16)
        hm, hr, w_up_bf, w_down_bf, w_out_bf = _mixers(
            proj, gates, mlstm_norm_g[l].reshape(H, 1, dh),
            lru_conv_w[l], lru_conv_b[l].reshape(1, d_r), w_ax,
            lru_ba[l].reshape(1, d_r), lru_bx[l].reshape(1, d_r), lru_lambda[l].reshape(1, d_r),
            w_up[l], w_down[l], w_out[l], batch=B, seq=S, heads=H, dh=dh, rows=MIXER_ROWS)

        x1, n2 = _outproj(xf, hm, hr, w_out_bf, norm_ffn_g[l].reshape(1, D), tm=512)

        xf = _ffn(n2, x1, w_up_bf, ffn_conv_w[l], ffn_conv_b[l].reshape(1, -1),
                  w_down_bf, norm_final_g.reshape(1, D), seq=S, tm=1024, tf=512,
                  final_norm=(l == depth - 1))
    return xf.reshape(B, S, D)
```

```python
import functools
import math

import jax
import jax.numpy as jnp
from jax import lax
from jax.experimental import pallas as pl
from jax.experimental.pallas import tpu as pltpu

F32 = jnp.float32
BF16 = jnp.bfloat16
EPS = 1e-6
LANES = 128
SUBLANES = 8
LANE_BITS = LANES.bit_length() - 1
SUBLANE_BITS = SUBLANES.bit_length() - 1
VMEM_LIMIT = 56 * 1024 * 1024
VMEM_LIMIT_FFN = 61 * 1024 * 1024

MLSTM_HEADS = 4
LRU_BLOCKS = 8
LRU_CONV = 4
LRU_C = 8.0
FFN_CONV = 3
MIXER_ROWS = 256


def _sigmoid(x):
    return 0.5 * jnp.tanh(0.5 * x) + 0.5


def _log_sigmoid(x):
    return jnp.minimum(x, 0.0) - jnp.log1p(jnp.exp(-jnp.abs(x)))


def _softplus(x):
    return jnp.maximum(x, 0.0) + jnp.log1p(jnp.exp(-jnp.abs(x)))


def _gelu_tanh(x):
    c = math.sqrt(2.0 / math.pi)
    t = jnp.tanh(x * (c + (c * 0.044715) * (x * x)))
    hx = 0.5 * x
    return hx + hx * t


def _rmsnorm(x, g):
    return x * lax.rsqrt(jnp.mean(x * x, axis=-1, keepdims=True) + EPS) * g


_NT = (((1,), (1,)), ((), ()))


def _inproj_kernel(x_ref, g_ref, w_ref, nxt_ref, wif_ref, bif_ref, proj_ref, gate_ref, n1_ref,
                   *, first_shifted, shift):
    n = pl.program_id(1)

    @pl.when(n == 0)
    def _():
        n1 = _rmsnorm(x_ref[...], g_ref[...]).astype(BF16)
        n1_ref[...] = n1
        gate_ref[...] = lax.dot_general(n1, wif_ref[...].astype(BF16), _NT,
                                        preferred_element_type=F32) + bif_ref[...]

    def project(w_bf):
        proj_ref[...] = lax.dot_general(n1_ref[...], w_bf, _NT,
                                        preferred_element_type=F32).astype(proj_ref.dtype)

    @pl.when(n < first_shifted)
    def _():
        project(w_ref[...].astype(BF16))

    @pl.when(n >= first_shifted)
    def _():
        project(jnp.concatenate([w_ref[shift:, :], nxt_ref[...]], axis=0).astype(BF16))


def _inproj(x, g, wt, b_if, *, gate_lo, n_gate, tm, tn):
    T, D = x.shape
    P = wt.shape[0]
    N = P - n_gate
    assert gate_lo % tn == 0 and N % tn == 0 and n_gate == SUBLANES and gate_lo % LANES == 0
    first_shifted = gate_lo // tn
    per = tn // n_gate
    kern = functools.partial(_inproj_kernel, first_shifted=first_shifted, shift=n_gate)
    return pl.pallas_call(
        kern,
        grid=(T // tm, N // tn),
        in_specs=[
            pl.BlockSpec((tm, D), lambda m, n: (m, 0)),
            pl.BlockSpec((1, D), lambda m, n: (0, 0)),
            pl.BlockSpec((tn, D), lambda m, n: (n, 0)),
            pl.BlockSpec((n_gate, D), lambda m, n: (jnp.maximum(n, first_shifted) * per + per, 0)),
            pl.BlockSpec((LANES, D), lambda m, n: (gate_lo // LANES, 0)),
            pl.BlockSpec((1, LANES), lambda m, n: (0, 0)),
        ],
        out_specs=[
            pl.BlockSpec((tm, tn), lambda m, n: (m, n)),
            pl.BlockSpec((tm, LANES), lambda m, n: (m, 0)),
        ],
        out_shape=[
            jax.ShapeDtypeStruct((T, N), BF16),
            jax.ShapeDtypeStruct((T, LANES), F32),
        ],
        scratch_shapes=[pltpu.VMEM((tm, D), BF16)],
        compiler_params=pltpu.CompilerParams(
            dimension_semantics=("arbitrary", "arbitrary"),
            vmem_limit_bytes=VMEM_LIMIT),
        name="inproj",
    )(x, g, wt, wt, wt, b_if)


def _causal_mask(L):
    return (lax.broadcasted_iota(jnp.int32, (L, L), 0)
            >= lax.broadcasted_iota(jnp.int32, (L, L), 1))


def _mlstm_init(s_ref, m_ref, tri_ref, sel_ref, *, heads, chunk):
    @pl.when(pl.program_id(1) == 0)
    def _():
        s_ref[...] = jnp.zeros_like(s_ref)
        m_ref[...] = jnp.zeros_like(m_ref)

    @pl.when((pl.program_id(0) == 0) & (pl.program_id(1) == 0))
    def _():
        tri_ref[...] = jnp.where(_causal_mask(chunk), 1.0, 0.0).astype(BF16)
        sr = lax.broadcasted_iota(jnp.int32, sel_ref.shape, 0)
        sc = lax.broadcasted_iota(jnp.int32, sel_ref.shape, 1)
        lane = sr & (LANES - 1)
        blk = sc >> LANE_BITS
        from_gate = sr < 3 * LANES
        hit_gate = from_gate & (blk < heads) & (lane == blk)
        hit_sum = (~from_gate) & (blk >= heads) & ((lane == blk - heads) | (lane == blk - heads + SUBLANES))
        sel_ref[...] = jnp.where(hit_gate | hit_sum, 1.0, 0.0).astype(BF16)


def _mlstm_main(q_ref, k_ref, v_ref, o_ref, gate_ref, hg_ref, out_ref,
                s_ref, m_ref, tri_ref, sel_ref, *, heads, dh, chunk):
    L = chunk
    scale = dh ** -0.5
    causal = _causal_mask(L)

    def split(x, parts):
        out = []
        for _ in range(parts - 1):
            p = x.astype(BF16)
            out.append(p)
            x = x - p.astype(F32)
        out.append(x.astype(BF16))
        return out

    def rep(x, width):
        return jnp.concatenate([x] * (width // LANES), axis=1)

    g = gate_ref[...]
    tri = tri_ref[...]
    sel = sel_ref[...]

    g8 = g.T[0:SUBLANES, :]
    g8r = pltpu.roll(g8, heads, axis=0)
    sub = lax.broadcasted_iota(jnp.int32, g8.shape, 0)
    ig = jnp.where(sub < heads, g8, g8r)
    fg = jnp.where(sub < heads, g8r, g8)
    logf = jnp.concatenate(split(_log_sigmoid(fg), 2), axis=0)
    br = lax.dot_general(logf, tri, _NT, preferred_element_type=F32)
    b = br[0:SUBLANES, :] + br[SUBLANES:, :]

    logf_pad = jnp.concatenate([logf, jnp.zeros((LANES - 2 * SUBLANES, L), BF16)], axis=0)
    b_nat = lax.dot_general(tri, logf_pad, _NT, preferred_element_type=F32)
    reps = jnp.dot(jnp.concatenate(split(g, 3) + split(b_nat, 3), axis=1), sel,
                   preferred_element_type=F32)
    a_row = ig - b
    b_tot = b[:, L - 1:L]
    m_prev = m_ref[:, 0:1]
    m_new = jnp.maximum(b_tot + m_prev, b_tot + jnp.max(a_row, axis=1, keepdims=True))
    decay = jnp.exp(b_tot + m_prev - m_new)
    end_shift = b_tot - m_new
    m_ref[...] = jnp.broadcast_to(m_new, m_ref.shape)
    ones_blk = jnp.ones((L, LANES), BF16)

    for h in range(heads):
        sl = slice(h * dh, (h + 1) * dh)
        q = q_ref[:, sl] * scale
        k = k_ref[:, sl]
        v = v_ref[:, sl]
        ig_r = reps[:, h * LANES:(h + 1) * LANES]
        b_r = reps[:, (heads + h) * LANES:(heads + h + 1) * LANES]
        a_h = a_row[h:h + 1, :]
        mp = m_prev[h:h + 1, :]

        s = lax.dot_general(q, k, _NT, preferred_element_type=F32)
        cm = jnp.max(jnp.where(causal, a_h, -jnp.inf), axis=-1, keepdims=True)
        mm_r = jnp.maximum(jnp.broadcast_to(cm, (L, LANES)), mp)
        w = s * jnp.exp(jnp.where(causal, a_h - rep(mm_r, L), -jnp.inf))
        einter_r = jnp.exp(mp - mm_r)
        floor_r = jnp.exp(-(b_r + mm_r))
        eend_r = jnp.exp(end_shift[h:h + 1, :] + ig_r - b_r)

        state = s_ref[h]
        intra = jnp.dot(w.astype(BF16), jnp.concatenate([v, ones_blk], axis=1),
                        preferred_element_type=F32)
        inter = jnp.dot(q, state.astype(BF16), preferred_element_type=F32)
        tot = intra + rep(einter_r, dh + LANES) * inter
        num = tot[:, :dh]
        den_r = tot[:, dh:]
        inv_r = 1.0 / jnp.maximum(jnp.abs(den_r), floor_r)
        ssq_r = jnp.broadcast_to(jnp.sum(num * num, axis=-1, keepdims=True), (L, LANES))
        rs_r = inv_r * lax.rsqrt(ssq_r * (inv_r * inv_r) * (1.0 / dh) + EPS)
        half_g = 0.5 * hg_ref[h]
        gain = half_g + half_g * jnp.tanh(0.5 * o_ref[:, sl].astype(F32))
        out_ref[:, sl] = ((num * rep(rs_r, dh)) * gain).astype(out_ref.dtype)

        ev = jnp.concatenate(
            [(rep(eend_r, dh) * v.astype(F32)).astype(BF16), eend_r.astype(BF16)], axis=1)
        loc = lax.dot_general(k, ev, (((0,), (0,)), ((), ())), preferred_element_type=F32)
        s_ref[h] = decay[h:h + 1, :] * state + loc


def _lru_init(xtail_ref, hc_ref, perm_ref, unperm_ref, *, ts):
    seg = ts // SUBLANES

    @pl.when(pl.program_id(1) == 0)
    def _():
        xtail_ref[...] = jnp.zeros_like(xtail_ref)
        hc_ref[...] = jnp.zeros_like(hc_ref)

    @pl.when((pl.program_id(0) == 0) & (pl.program_id(1) == 0))
    def _():
        ri = lax.broadcasted_iota(jnp.int32, (ts, ts), 0)
        ci = lax.broadcasted_iota(jnp.int32, (ts, ts), 1)
        t_of_row = (ri & (SUBLANES - 1)) * seg + (ri >> SUBLANE_BITS)
        t_of_col = (ci & (SUBLANES - 1)) * seg + (ci >> SUBLANE_BITS)
        perm_ref[...] = jnp.where(ci == t_of_row, 1.0, 0.0).astype(BF16)
        unperm_ref[...] = jnp.where(ri == t_of_col, 1.0, 0.0).astype(BF16)


def _lru_main(xr_ref, gr_ref, cw_ref, cb_ref, wax_ref, ba_ref, bx_ref, lam_ref, out_ref,
              xtail_ref, hc_ref, perm_ref, unperm_ref, a_scr, u_scr, c_scr, h_scr,
              *, ts, blocks, bd):
    C = blocks * bd
    seg = ts // SUBLANES
    halo = LRU_CONV - 1

    xb = xr_ref[...]
    xp = jnp.dot(perm_ref[...], xb, preferred_element_type=F32)
    gp = jnp.dot(perm_ref[...], gr_ref[...], preferred_element_type=F32)
    gelu_g = _gelu_tanh(gp)
    prev = xtail_ref[...]
    xtail_ref[...] = xb[ts - SUBLANES:ts, :].astype(F32)
    row8 = lax.broadcasted_iota(jnp.int32, (SUBLANES, C), 0)

    wrapped = []
    for k in range(1, halo + 1):
        r0 = (seg - k) * SUBLANES
        up = pltpu.roll(xp[r0:r0 + SUBLANES, :], 1, axis=0)
        wrapped.append(jnp.where(row8 == 0, prev[SUBLANES - k:SUBLANES - k + 1, :], up))

    cw = cw_ref[...]
    xc = cb_ref[...] + cw[LRU_CONV - 1:LRU_CONV, :] * xp
    for j in range(1, LRU_CONV):
        xs = jnp.concatenate([wrapped[k - 1] for k in range(j, 0, -1)]
                             + [xp[0:ts - j * SUBLANES, :]], axis=0)
        xc = xc + cw[LRU_CONV - 1 - j:LRU_CONV - j, :] * xs

    xcb = xc.astype(BF16)
    half_xc = 0.5 * xc
    k = (-0.5 * LRU_C * math.log2(math.e)) * _softplus(-lam_ref[...])
    half_ba = 0.5 * ba_ref[...]
    half_bx = 0.5 * bx_ref[...]
    for n in range(blocks):
        sl = slice(n * bd, (n + 1) * bd)
        ax = jnp.dot(xcb[:, sl], wax_ref[n], preferred_element_type=F32)
        t_r = jnp.tanh(ax[:, :bd] + half_ba[:, sl])
        t_i = jnp.tanh(ax[:, bd:] + half_bx[:, sl])
        a = jnp.exp2(k[:, sl] + k[:, sl] * t_r)
        ixc = half_xc[:, sl] + half_xc[:, sl] * t_i
        y = 1.0 - a * a
        a_scr[:, sl] = a
        u_scr[:, sl] = jnp.where(y > 0.0, y * lax.rsqrt(y), 0.0) * ixc

    def body(v, carry):
        h, c = carry
        rows = pl.ds(pl.multiple_of(v * SUBLANES, SUBLANES), SUBLANES)
        av = a_scr[rows, :]
        h = av * h + u_scr[rows, :]
        c = av * c
        h_scr[rows, :] = h
        c_scr[rows, :] = c
        return h, c

    init = (jnp.zeros((SUBLANES, C), F32), jnp.ones((SUBLANES, C), F32))
    h_end, c_end = lax.fori_loop(0, seg, body, init, unroll=4)

    state = hc_ref[0:1, :]
    entering = []
    for s in range(SUBLANES):
        entering.append(state)
        state = c_end[s:s + 1, :] * state + h_end[s:s + 1, :]
    hc_ref[0:1, :] = state
    entering = jnp.concatenate(entering, axis=0)

    h = (h_scr[...].reshape(seg, SUBLANES, C)
         + c_scr[...].reshape(seg, SUBLANES, C) * entering[None]).reshape(ts, C)
    outp = (h * gelu_g).astype(BF16)
    out_ref[...] = jnp.dot(unperm_ref[...], outp,
                           preferred_element_type=F32).astype(out_ref.dtype)


def _mixers_kernel(proj_ref, gate_ref, hg_ref,
                   cw_ref, cb_ref, wax_ref, ba_ref, bx_ref, lam_ref,
                   wup_ref, wdn_ref, wout_ref,
                   mix_ref, wup_bf_ref, wdn_bf_ref, wout_bf_ref,
                   s_ref, m_ref, tri_ref, sel_ref,
                   xtail_ref, hc_ref, perm_ref, unperm_ref, a_scr, u_scr, c_scr, h_scr,
                   *, heads, dh, rows, blocks, bd):
    W = heads * dh
    q_ref, k_ref, v_ref, o_ref, xr_ref, gr_ref = (
        proj_ref.at[:, j * W:(j + 1) * W] for j in range(6))
    hm_ref = mix_ref.at[:, 0:W]
    hr_ref = mix_ref.at[:, W:2 * W]

    _mlstm_init(s_ref, m_ref, tri_ref, sel_ref, heads=heads, chunk=rows)
    _lru_init(xtail_ref, hc_ref, perm_ref, unperm_ref, ts=rows)

    wup_bf_ref[...] = wup_ref[...].astype(BF16)
    wdn_bf_ref[...] = wdn_ref[...].astype(BF16)
    wout_bf_ref[...] = wout_ref[...].astype(BF16)

    _mlstm_main(q_ref, k_ref, v_ref, o_ref, gate_ref, hg_ref, hm_ref,
                s_ref, m_ref, tri_ref, sel_ref, heads=heads, dh=dh, chunk=rows)
    _lru_main(xr_ref, gr_ref, cw_ref, cb_ref, wax_ref, ba_ref, bx_ref, lam_ref, hr_ref,
              xtail_ref, hc_ref, perm_ref, unperm_ref, a_scr, u_scr, c_scr, h_scr,
              ts=rows, blocks=blocks, bd=bd)


def _mixers(proj, gates, head_g, conv_w, conv_b, w_ax, ba, bx, lam, w_up, w_down, w_out,
            *, batch, seq, heads, dh, rows):
    T = batch * seq
    W = heads * dh
    blocks, bd, _ = w_ax.shape
    C = blocks * bd
    assert W == C and bd == LANES and rows % (SUBLANES * SUBLANES) == 0
    ns = seq // rows
    steps = batch * ns
    row_blk = lambda b, s: b * ns + s

    def slab(w):
        n = w.shape[0] // steps
        assert n * steps == w.shape[0] and n % 16 == 0
        return pl.BlockSpec((n, w.shape[1]), lambda b, s: (row_blk(b, s), 0))

    rows_of = lambda width: pl.BlockSpec((rows, width), lambda b, s: (row_blk(b, s), 0))
    vec = lambda n: pl.BlockSpec((n, C), lambda b, s: (0, 0))
    kern = functools.partial(_mixers_kernel, heads=heads, dh=dh, rows=rows, blocks=blocks, bd=bd)
    return pl.pallas_call(
        kern,
        grid=(batch, ns),
        in_specs=[
            rows_of(proj.shape[1]),
            rows_of(LANES),
            pl.BlockSpec((heads, 1, dh), lambda b, s: (0, 0, 0)),
            vec(LRU_CONV), vec(1),
            pl.BlockSpec((blocks, bd, 2 * bd), lambda b, s: (0, 0, 0)),
            vec(1), vec(1), vec(1),
            slab(w_up), slab(w_down), slab(w_out),
        ],
        out_specs=[rows_of(W + C), slab(w_up), slab(w_down), slab(w_out)],
        out_shape=[jax.ShapeDtypeStruct((T, W + C), BF16),
                   jax.ShapeDtypeStruct(w_up.shape, BF16),
                   jax.ShapeDtypeStruct(w_down.shape, BF16),
                   jax.ShapeDtypeStruct(w_out.shape, BF16)],
        scratch_shapes=[
            pltpu.VMEM((heads, dh, dh + LANES), F32),
            pltpu.VMEM((SUBLANES, LANES), F32),
            pltpu.VMEM((rows, rows), BF16),
            pltpu.VMEM((6 * LANES, 2 * heads * LANES), BF16),
            pltpu.VMEM((SUBLANES, C), F32),
            pltpu.VMEM((SUBLANES, C), F32),
            pltpu.VMEM((rows, rows), BF16),
            pltpu.VMEM((rows, rows), BF16),
        ] + [pltpu.VMEM((rows, C), F32)] * 4,
        compiler_params=pltpu.CompilerParams(
            dimension_semantics=("arbitrary", "arbitrary"),
            vmem_limit_bytes=VMEM_LIMIT),
        name="mixers",
    )(proj, gates, head_g, conv_w, conv_b, w_ax, ba, bx, lam, w_up, w_down, w_out)


def _outproj_kernel(x_ref, mix_ref, w_ref, g_ref, x1_ref, n2_ref):
    x1 = x_ref[...] + jnp.dot(mix_ref[...], w_ref[...], preferred_element_type=F32)
    x1_ref[...] = x1
    n2_ref[...] = _rmsnorm(x1, g_ref[...]).astype(n2_ref.dtype)


def _outproj(x, mix, w_out, g, *, tm):
    T, D = x.shape
    Dm = mix.shape[1]
    return pl.pallas_call(
        _outproj_kernel,
        grid=(T // tm,),
        in_specs=[
            pl.BlockSpec((tm, D), lambda m: (m, 0)),
            pl.BlockSpec((tm, Dm), lambda m: (m, 0)),
            pl.BlockSpec((Dm, D), lambda m: (0, 0)),
            pl.BlockSpec((1, D), lambda m: (0, 0)),
        ],
        out_specs=[
            pl.BlockSpec((tm, D), lambda m: (m, 0)),
            pl.BlockSpec((tm, D), lambda m: (m, 0)),
        ],
        out_shape=[
            jax.ShapeDtypeStruct((T, D), F32),
            jax.ShapeDtypeStruct((T, D), BF16),
        ],
        compiler_params=pltpu.CompilerParams(
            dimension_semantics=("arbitrary",),
            vmem_limit_bytes=VMEM_LIMIT),
        name="outproj",
    )(x, mix, w_out, g)


def _ffn_kernel(n2_ref, x1_ref, wg_ref, wu_ref, cw_ref, cb_ref, wd_ref, gf_ref, y_ref,
                tail_ref, *, tm, tiles_per_seq, final_norm):
    m = pl.program_id(0)
    f = pl.program_id(1)
    nf = pl.num_programs(1)

    @pl.when(f == 0)
    def _():
        y_ref[...] = x1_ref[...]

    n2 = n2_ref[...]
    gate = jnp.dot(n2, wg_ref[...], preferred_element_type=F32)
    up = jnp.dot(n2, wu_ref[...], preferred_element_type=F32)

    prev = jnp.where(m % tiles_per_seq == 0, 0.0, tail_ref[f])
    tail_ref[f] = gate[tm - SUBLANES:tm, :]

    row8 = lax.broadcasted_iota(jnp.int32, prev.shape, 0)
    cw = cw_ref[...]
    conv = cb_ref[...] + cw[FFN_CONV - 1:FFN_CONV, :] * gate
    for j in range(1, FFN_CONV):
        r = pltpu.roll(gate, j, axis=0)
        top = jnp.where(row8 < j, pltpu.roll(prev, j, axis=0), r[0:SUBLANES, :])
        gs = jnp.concatenate([top, r[SUBLANES:, :]], axis=0)
        conv = conv + cw[FFN_CONV - 1 - j:FFN_CONV - j, :] * gs
    hid = (conv * _sigmoid(conv)) * up
    hid_b = hid.astype(BF16)
    tf = hid_b.shape[1]
    for c in range(y_ref.shape[1] // tf):
        cols = slice(c * tf, (c + 1) * tf)
        y_ref[:, cols] += jnp.dot(hid_b, wd_ref[:, cols], preferred_element_type=F32)

    if final_norm:
        @pl.when(f == nf - 1)
        def _():
            y_ref[...] = _rmsnorm(y_ref[...], gf_ref[...])


def _ffn(n2, x1, w_up, conv_w, conv_b, w_down, gf, *, seq, tm, tf, final_norm):
    T, D = n2.shape
    F = w_down.shape[0]
    nf = F // tf
    kern = functools.partial(_ffn_kernel, tm=tm, tiles_per_seq=seq // tm, final_norm=final_norm)
    return pl.pallas_call(
        kern,
        grid=(T // tm, nf),
        in_specs=[
            pl.BlockSpec((tm, D), lambda m, f: (m, 0)),
            pl.BlockSpec((tm, D), lambda m, f: (m, 0)),
            pl.BlockSpec((D, tf), lambda m, f: (0, f)),
            pl.BlockSpec((D, tf), lambda m, f: (0, nf + f)),
            pl.BlockSpec((FFN_CONV, tf), lambda m, f: (0, f)),
            pl.BlockSpec((1, tf), lambda m, f: (0, f)),
            pl.BlockSpec((tf, D), lambda m, f: (f, 0)),
            pl.BlockSpec((1, D), lambda m, f: (0, 0)),
        ],
        out_specs=pl.BlockSpec((tm, D), lambda m, f: (m, 0)),
        out_shape=jax.ShapeDtypeStruct((T, D), F32),
        scratch_shapes=[pltpu.VMEM((nf, SUBLANES, tf), F32)],
        compiler_params=pltpu.CompilerParams(
            dimension_semantics=("arbitrary", "arbitrary"),
            vmem_limit_bytes=VMEM_LIMIT_FFN),
        name="convffn",
    )(n2, x1, w_up, w_up, conv_w, conv_b, w_down, gf)


def kernel(x, norm_mix_g, w_in, b_gate_m, mlstm_norm_g, lru_conv_w, lru_conv_b, lru_wa, lru_ba,
           lru_wx, lru_bx, lru_lambda, w_out, norm_ffn_g, w_up, ffn_conv_w, ffn_conv_b, w_down,
           norm_final_g):
    B, S, D = x.shape
    T = B * S
    depth = w_in.shape[0]
    H = MLSTM_HEADS
    dh = mlstm_norm_g.shape[-1]
    d_m = H * dh
    d_r = lru_conv_w.shape[-1]
    assert d_m == d_r, "column-block addressing assumes equal head-group widths"
    n_gate = 2 * H
    gate_lo = 4 * d_m

    xf = x.reshape(T, D)
    for l in range(depth):
        b_if = jnp.pad(b_gate_m[l].astype(F32), (0, LANES - n_gate)).reshape(1, LANES)
        proj, gates = _inproj(xf, norm_mix_g[l].reshape(1, D), jnp.swapaxes(w_in[l], 0, 1), b_if,
                              gate_lo=gate_lo, n_gate=n_gate, tm=1024, tn=1024)

        w_ax = (0.5 * jnp.concatenate([lru_wa[l], lru_wx[l]], axis=-1)).astype(BF16)
        mix, w_up_bf, w_down_bf, w_out_bf = _mixers(
            proj, gates, mlstm_norm_g[l].reshape(H, 1, dh),
            lru_conv_w[l], lru_conv_b[l].reshape(1, d_r), w_ax,
            lru_ba[l].reshape(1, d_r), lru_bx[l].reshape(1, d_r), lru_lambda[l].reshape(1, d_r),
            w_up[l], w_down[l], w_out[l], batch=B, seq=S, heads=H, dh=dh, rows=MIXER_ROWS)

        x1, n2 = _outproj(xf, mix, w_out_bf, norm_ffn_g[l].reshape(1, D), tm=512)

        xf = _ffn(n2, x1, w_up_bf, ffn_conv_w[l], ffn_conv_b[l].reshape(1, -1),
                  w_down_bf, norm_final_g.reshape(1, D), seq=S, tm=1024, tf=512,
                  final_norm=(l == depth - 1))
    return xf.reshape(B, S, D)
```

```python
import functools
import math

import jax
import jax.numpy as jnp
from jax import lax
from jax.experimental import pallas as pl
from jax.experimental.pallas import tpu as pltpu

F32 = jnp.float32
BF16 = jnp.bfloat16
EPS = 1e-6
LANES = 128
SUBLANES = 8
LANE_BITS = LANES.bit_length() - 1
SUBLANE_BITS = SUBLANES.bit_length() - 1
VMEM_LIMIT = 56 * 1024 * 1024
VMEM_LIMIT_FFN = 61 * 1024 * 1024

MLSTM_HEADS = 4
LRU_BLOCKS = 8
LRU_CONV = 4
LRU_C = 8.0
FFN_CONV = 3
MIXER_ROWS = 256


def _sigmoid(x):
    return 0.5 * jnp.tanh(0.5 * x) + 0.5


def _log_sigmoid(x):
    return jnp.minimum(x, 0.0) - jnp.log1p(jnp.exp(-jnp.abs(x)))


def _softplus(x):
    return jnp.maximum(x, 0.0) + jnp.log1p(jnp.exp(-jnp.abs(x)))


def _gelu_tanh(x):
    c = math.sqrt(2.0 / math.pi)
    t = jnp.tanh(x * (c + (c * 0.044715) * (x * x)))
    hx = 0.5 * x
    return hx + hx * t


def _rmsnorm(x, g):
    return x * lax.rsqrt(jnp.mean(x * x, axis=-1, keepdims=True) + EPS) * g


_NT = (((1,), (1,)), ((), ()))


def _inproj_kernel(x_ref, g_ref, w_ref, nxt_ref, wif_ref, bif_ref, proj_ref, gate_ref, n1_ref,
                   *, first_shifted, shift):
    n = pl.program_id(1)

    @pl.when(n == 0)
    def _():
        n1 = _rmsnorm(x_ref[...], g_ref[...]).astype(BF16)
        n1_ref[...] = n1
        gate_ref[...] = lax.dot_general(n1, wif_ref[...].astype(BF16), _NT,
                                        preferred_element_type=F32) + bif_ref[...]

    def project(w_bf):
        proj_ref[...] = lax.dot_general(n1_ref[...], w_bf, _NT,
                                        preferred_element_type=F32).astype(proj_ref.dtype)

    @pl.when(n < first_shifted)
    def _():
        project(w_ref[...].astype(BF16))

    @pl.when(n >= first_shifted)
    def _():
        project(jnp.concatenate([w_ref[shift:, :], nxt_ref[...]], axis=0).astype(BF16))


def _inproj(x, g, wt, b_if, *, gate_lo, n_gate, tm, tn):
    T, D = x.shape
    P = wt.shape[0]
    N = P - n_gate
    assert gate_lo % tn == 0 and N % tn == 0 and n_gate == SUBLANES and gate_lo % LANES == 0
    first_shifted = gate_lo // tn
    per = tn // n_gate
    kern = functools.partial(_inproj_kernel, first_shifted=first_shifted, shift=n_gate)
    return pl.pallas_call(
        kern,
        grid=(T // tm, N // tn),
        in_specs=[
            pl.BlockSpec((tm, D), lambda m, n: (m, 0)),
            pl.BlockSpec((1, D), lambda m, n: (0, 0)),
            pl.BlockSpec((tn, D), lambda m, n: (n, 0)),
            pl.BlockSpec((n_gate, D), lambda m, n: (jnp.maximum(n, first_shifted) * per + per, 0)),
            pl.BlockSpec((LANES, D), lambda m, n: (gate_lo // LANES, 0)),
            pl.BlockSpec((1, LANES), lambda m, n: (0, 0)),
        ],
        out_specs=[
            pl.BlockSpec((tm, tn), lambda m, n: (m, n)),
            pl.BlockSpec((tm, LANES), lambda m, n: (m, 0)),
        ],
        out_shape=[
            jax.ShapeDtypeStruct((T, N), BF16),
            jax.ShapeDtypeStruct((T, LANES), F32),
        ],
        scratch_shapes=[pltpu.VMEM((tm, D), BF16)],
        compiler_params=pltpu.CompilerParams(
            dimension_semantics=("arbitrary", "arbitrary"),
            vmem_limit_bytes=VMEM_LIMIT),
        name="inproj",
    )(x, g, wt, wt, wt, b_if)


def _causal_mask(L):
    return (lax.broadcasted_iota(jnp.int32, (L, L), 0)
            >= lax.broadcasted_iota(jnp.int32, (L, L), 1))


def _mlstm_init(s_ref, m_ref, tri_ref, sel_ref, *, heads, chunk):
    @pl.when(pl.program_id(1) == 0)
    def _():
        s_ref[...] = jnp.zeros_like(s_ref)
        m_ref[...] = jnp.zeros_like(m_ref)

    @pl.when((pl.program_id(0) == 0) & (pl.program_id(1) == 0))
    def _():
        tri_ref[...] = jnp.where(_causal_mask(chunk), 1.0, 0.0).astype(BF16)
        sr = lax.broadcasted_iota(jnp.int32, sel_ref.shape, 0)
        sc = lax.broadcasted_iota(jnp.int32, sel_ref.shape, 1)
        lane = sr & (LANES - 1)
        blk = sc >> LANE_BITS
        from_gate = sr < 3 * LANES
        hit_gate = from_gate & (blk < heads) & (lane == blk)
        hit_sum = (~from_gate) & (blk >= heads) & ((lane == blk - heads) | (lane == blk - heads + SUBLANES))
        sel_ref[...] = jnp.where(hit_gate | hit_sum, 1.0, 0.0).astype(BF16)


def _mlstm_main(q_ref, k_ref, v_ref, o_ref, gate_ref, hg_ref, out_ref,
                s_ref, m_ref, tri_ref, sel_ref, *, heads, dh, chunk):
    L = chunk
    scale = dh ** -0.5
    causal = _causal_mask(L)

    def split(x, parts):
        out = []
        for _ in range(parts - 1):
            p = x.astype(BF16)
            out.append(p)
            x = x - p.astype(F32)
        out.append(x.astype(BF16))
        return out

    def rep(x, width):
        return jnp.concatenate([x] * (width // LANES), axis=1)

    g = gate_ref[...]
    tri = tri_ref[...]
    sel = sel_ref[...]

    g8 = g.T[0:SUBLANES, :]
    g8r = pltpu.roll(g8, heads, axis=0)
    sub = lax.broadcasted_iota(jnp.int32, g8.shape, 0)
    ig = jnp.where(sub < heads, g8, g8r)
    fg = jnp.where(sub < heads, g8r, g8)
    logf = jnp.concatenate(split(_log_sigmoid(fg), 2), axis=0)
    br = lax.dot_general(logf, tri, _NT, preferred_element_type=F32)
    b = br[0:SUBLANES, :] + br[SUBLANES:, :]

    logf_pad = jnp.concatenate([logf, jnp.zeros((LANES - 2 * SUBLANES, L), BF16)], axis=0)
    b_nat = lax.dot_general(tri, logf_pad, _NT, preferred_element_type=F32)
    reps = jnp.dot(jnp.concatenate(split(g, 3) + split(b_nat, 3), axis=1), sel,
                   preferred_element_type=F32)
    a_row = ig - b
    b_tot = b[:, L - 1:L]
    m_prev = m_ref[:, 0:1]
    m_new = jnp.maximum(b_tot + m_prev, b_tot + jnp.max(a_row, axis=1, keepdims=True))
    decay = jnp.exp(b_tot + m_prev - m_new)
    end_shift = b_tot - m_new
    m_ref[...] = jnp.broadcast_to(m_new, m_ref.shape)
    ones_blk = jnp.ones((L, LANES), BF16)

    for h in range(heads):
        sl = slice(h * dh, (h + 1) * dh)
        q = q_ref[:, sl] * scale
        k = k_ref[:, sl]
        v = v_ref[:, sl]
        ig_r = reps[:, h * LANES:(h + 1) * LANES]
        b_r = reps[:, (heads + h) * LANES:(heads + h + 1) * LANES]
        a_h = a_row[h:h + 1, :]
        mp = m_prev[h:h + 1, :]

        s = lax.dot_general(q, k, _NT, preferred_element_type=F32)
        cm = jnp.max(jnp.where(causal, a_h, -jnp.inf), axis=-1, keepdims=True)
        mm_r = jnp.maximum(jnp.broadcast_to(cm, (L, LANES)), mp)
        w = s * jnp.exp(jnp.where(causal, a_h - rep(mm_r, L), -jnp.inf))
        einter_r = jnp.exp(mp - mm_r)
        floor_r = jnp.exp(-(b_r + mm_r))
        eend_r = jnp.exp(end_shift[h:h + 1, :] + ig_r - b_r)

        state = s_ref[h]
        intra = jnp.dot(w.astype(BF16), jnp.concatenate([v, ones_blk], axis=1),
                        preferred_element_type=F32)
        inter = jnp.dot(q, state.astype(BF16), preferred_element_type=F32)
        tot = intra + rep(einter_r, dh + LANES) * inter
        num = tot[:, :dh]
        den_r = tot[:, dh:]
        inv_r = 1.0 / jnp.maximum(jnp.abs(den_r), floor_r)
        ssq_r = jnp.broadcast_to(jnp.sum(num * num, axis=-1, keepdims=True), (L, LANES))
        rs_r = inv_r * lax.rsqrt(ssq_r * (inv_r * inv_r) * (1.0 / dh) + EPS)
        half_g = 0.5 * hg_ref[h]
        gain = half_g + half_g * jnp.tanh(0.5 * o_ref[:, sl].astype(F32))
        out_ref[:, sl] = ((num * rep(rs_r, dh)) * gain).astype(out_ref.dtype)

        ev = jnp.concatenate(
            [(rep(eend_r, dh) * v.astype(F32)).astype(BF16), eend_r.astype(BF16)], axis=1)
        loc = lax.dot_general(k, ev, (((0,), (0,)), ((), ())), preferred_element_type=F32)
        s_ref[h] = decay[h:h + 1, :] * state + loc


def _lru_init(xtail_ref, hc_ref, perm_ref, unperm_ref, *, ts):
    seg = ts // SUBLANES

    @pl.when(pl.program_id(1) == 0)
    def _():
        xtail_ref[...] = jnp.zeros_like(xtail_ref)
        hc_ref[...] = jnp.zeros_like(hc_ref)

    @pl.when((pl.program_id(0) == 0) & (pl.program_id(1) == 0))
    def _():
        ri = lax.broadcasted_iota(jnp.int32, (ts, ts), 0)
        ci = lax.broadcasted_iota(jnp.int32, (ts, ts), 1)
        t_of_row = (ri & (SUBLANES - 1)) * seg + (ri >> SUBLANE_BITS)
        t_of_col = (ci & (SUBLANES - 1)) * seg + (ci >> SUBLANE_BITS)
        perm_ref[...] = jnp.where(ci == t_of_row, 1.0, 0.0).astype(BF16)
        unperm_ref[...] = jnp.where(ri == t_of_col, 1.0, 0.0).astype(BF16)


def _lru_main(xr_ref, gr_ref, cw_ref, cb_ref, wax_ref, ba_ref, bx_ref, lam_ref, out_ref,
              xtail_ref, hc_ref, perm_ref, unperm_ref, a_scr, u_scr, c_scr, h_scr,
              *, ts, blocks, bd):
    C = blocks * bd
    seg = ts // SUBLANES
    halo = LRU_CONV - 1

    xb = xr_ref[...]
    xp = jnp.dot(perm_ref[...], xb, preferred_element_type=F32)
    gp = jnp.dot(perm_ref[...], gr_ref[...], preferred_element_type=F32)
    gelu_g = _gelu_tanh(gp)
    prev = xtail_ref[...]
    xtail_ref[...] = xb[ts - SUBLANES:ts, :].astype(F32)
    row8 = lax.broadcasted_iota(jnp.int32, (SUBLANES, C), 0)

    wrapped = []
    for k in range(1, halo + 1):
        r0 = (seg - k) * SUBLANES
        up = pltpu.roll(xp[r0:r0 + SUBLANES, :], 1, axis=0)
        wrapped.append(jnp.where(row8 == 0, prev[SUBLANES - k:SUBLANES - k + 1, :], up))

    cw = cw_ref[...]
    xc = cb_ref[...] + cw[LRU_CONV - 1:LRU_CONV, :] * xp
    for j in range(1, LRU_CONV):
        xs = jnp.concatenate([wrapped[k - 1] for k in range(j, 0, -1)]
                             + [xp[0:ts - j * SUBLANES, :]], axis=0)
        xc = xc + cw[LRU_CONV - 1 - j:LRU_CONV - j, :] * xs

    xcb = xc.astype(BF16)
    half_xc = 0.5 * xc
    k = (-0.5 * LRU_C * math.log2(math.e)) * _softplus(-lam_ref[...])
    half_ba = 0.5 * ba_ref[...]
    half_bx = 0.5 * bx_ref[...]
    for n in range(blocks):
        sl = slice(n * bd, (n + 1) * bd)
        ax = jnp.dot(xcb[:, sl], wax_ref[n], preferred_element_type=F32)
        t_r = jnp.tanh(ax[:, :bd] + half_ba[:, sl])
        t_i = jnp.tanh(ax[:, bd:] + half_bx[:, sl])
        a = jnp.exp2(k[:, sl] + k[:, sl] * t_r)
        ixc = half_xc[:, sl] + half_xc[:, sl] * t_i
        y = 1.0 - a * a
        a_scr[:, sl] = a
        u_scr[:, sl] = jnp.where(y > 0.0, y * lax.rsqrt(y), 0.0) * ixc

    def body(v, carry):
        h, c = carry
        rows = pl.ds(pl.multiple_of(v * SUBLANES, SUBLANES), SUBLANES)
        av = a_scr[rows, :]
        h = av * h + u_scr[rows, :]
        c = av * c
        h_scr[rows, :] = h
        c_scr[rows, :] = c
        return h, c

    init = (jnp.zeros((SUBLANES, C), F32), jnp.ones((SUBLANES, C), F32))
    h_end, c_end = lax.fori_loop(0, seg, body, init, unroll=4)

    state = hc_ref[0:1, :]
    entering = []
    for s in range(SUBLANES):
        entering.append(state)
        state = c_end[s:s + 1, :] * state + h_end[s:s + 1, :]
    hc_ref[0:1, :] = state
    entering = jnp.concatenate(entering, axis=0)

    h = (h_scr[...].reshape(seg, SUBLANES, C)
         + c_scr[...].reshape(seg, SUBLANES, C) * entering[None]).reshape(ts, C)
    outp = (h * gelu_g).astype(BF16)
    out_ref[...] = jnp.dot(unperm_ref[...], outp,
                           preferred_element_type=F32).astype(out_ref.dtype)


def _mixers_kernel(proj_ref, gate_ref, hg_ref,
                   cw_ref, cb_ref, wax_ref, ba_ref, bx_ref, lam_ref,
                   wup_ref, wdn_ref, wout_ref,
                   mix_ref, wup_bf_ref, wdn_bf_ref, wout_bf_ref,
                   s_ref, m_ref, tri_ref, sel_ref,
                   xtail_ref, hc_ref, perm_ref, unperm_ref, a_scr, u_scr, c_scr, h_scr,
                   *, heads, dh, rows, blocks, bd):
    W = heads * dh
    q_ref, k_ref, v_ref, o_ref, xr_ref, gr_ref = (
        proj_ref.at[:, j * W:(j + 1) * W] for j in range(6))
    hm_ref = mix_ref.at[:, 0:W]
    hr_ref = mix_ref.at[:, W:2 * W]

    _mlstm_init(s_ref, m_ref, tri_ref, sel_ref, heads=heads, chunk=rows)
    _lru_init(xtail_ref, hc_ref, perm_ref, unperm_ref, ts=rows)

    wup_bf_ref[...] = wup_ref[...].astype(BF16)
    wdn_bf_ref[...] = wdn_ref[...].astype(BF16)
    wout_bf_ref[...] = wout_ref[...].astype(BF16)

    _mlstm_main(q_ref, k_ref, v_ref, o_ref, gate_ref, hg_ref, hm_ref,
                s_ref, m_ref, tri_ref, sel_ref, heads=heads, dh=dh, chunk=rows)
    _lru_main(xr_ref, gr_ref, cw_ref, cb_ref, wax_ref, ba_ref, bx_ref, lam_ref, hr_ref,
              xtail_ref, hc_ref, perm_ref, unperm_ref, a_scr, u_scr, c_scr, h_scr,
              ts=rows, blocks=blocks, bd=bd)


def _mixers(proj, gates, head_g, conv_w, conv_b, w_ax, ba, bx, lam, w_up, w_down, w_out,
            *, batch, seq, heads, dh, rows):
    T = batch * seq
    W = heads * dh
    blocks, bd, _ = w_ax.shape
    C = blocks * bd
    assert W == C and bd == LANES and rows % (SUBLANES * SUBLANES) == 0
    ns = seq // rows
    steps = batch * ns
    row_blk = lambda b, s: b * ns + s

    def slab(w):
        n = w.shape[0] // steps
        assert n * steps == w.shape[0] and n % 16 == 0
        return pl.BlockSpec((n, w.shape[1]), lambda b, s: (row_blk(b, s), 0))

    rows_of = lambda width: pl.BlockSpec((rows, width), lambda b, s: (row_blk(b, s), 0))
    vec = lambda n: pl.BlockSpec((n, C), lambda b, s: (0, 0))
    kern = functools.partial(_mixers_kernel, heads=heads, dh=dh, rows=rows, blocks=blocks, bd=bd)
    return pl.pallas_call(
        kern,
        grid=(batch, ns),
        in_specs=[
            rows_of(proj.shape[1]),
            rows_of(LANES),
            pl.BlockSpec((heads, 1, dh), lambda b, s: (0, 0, 0)),
            vec(LRU_CONV), vec(1),
            pl.BlockSpec((blocks, bd, 2 * bd), lambda b, s: (0, 0, 0)),
            vec(1), vec(1), vec(1),
            slab(w_up), slab(w_down), slab(w_out),
        ],
        out_specs=[rows_of(W + C), slab(w_up), slab(w_down), slab(w_out)],
        out_shape=[jax.ShapeDtypeStruct((T, W + C), BF16),
                   jax.ShapeDtypeStruct(w_up.shape, BF16),
                   jax.ShapeDtypeStruct(w_down.shape, BF16),
                   jax.ShapeDtypeStruct(w_out.shape, BF16)],
        scratch_shapes=[
            pltpu.VMEM((heads, dh, dh + LANES), F32),
            pltpu.VMEM((SUBLANES, LANES), F32),
            pltpu.VMEM((rows, rows), BF16),
            pltpu.VMEM((6 * LANES, 2 * heads * LANES), BF16),
            pltpu.VMEM((SUBLANES, C), F32),
            pltpu.VMEM((SUBLANES, C), F32),
            pltpu.VMEM((rows, rows), BF16),
            pltpu.VMEM((rows, rows), BF16),
        ] + [pltpu.VMEM((rows, C), F32)] * 4,
        compiler_params=pltpu.CompilerParams(
            dimension_semantics=("arbitrary", "arbitrary"),
            vmem_limit_bytes=VMEM_LIMIT),
        name="mixers",
    )(proj, gates, head_g, conv_w, conv_b, w_ax, ba, bx, lam, w_up, w_down, w_out)


def _outproj_kernel(x_ref, mix_ref, w_ref, g_ref, x1_ref, n2_ref):
    x1 = x_ref[...] + jnp.dot(mix_ref[...], w_ref[...], preferred_element_type=F32)
    x1_ref[...] = x1
    n2_ref[...] = _rmsnorm(x1, g_ref[...]).astype(n2_ref.dtype)


def _outproj(x, mix, w_out, g, *, tm):
    T, D = x.shape
    Dm = mix.shape[1]
    return pl.pallas_call(
        _outproj_kernel,
        grid=(T // tm,),
        in_specs=[
            pl.BlockSpec((tm, D), lambda m: (m, 0)),
            pl.BlockSpec((tm, Dm), lambda m: (m, 0)),
            pl.BlockSpec((Dm, D), lambda m: (0, 0)),
            pl.BlockSpec((1, D), lambda m: (0, 0)),
        ],
        out_specs=[
            pl.BlockSpec((tm, D), lambda m: (m, 0)),
            pl.BlockSpec((tm, D), lambda m: (m, 0)),
        ],
        out_shape=[
            jax.ShapeDtypeStruct((T, D), F32),
            jax.ShapeDtypeStruct((T, D), BF16),
        ],
        compiler_params=pltpu.CompilerParams(
            dimension_semantics=("arbitrary",),
            vmem_limit_bytes=VMEM_LIMIT),
        name="outproj",
    )(x, mix, w_out, g)


def _ffn_kernel(n2_ref, x1_ref, wg_ref, wu_ref, cw_ref, cb_ref, wd_ref, gf_ref, y_ref,
                tail_ref, *, tm, tiles_per_seq, final_norm):
    m = pl.program_id(0)
    f = pl.program_id(1)
    nf = pl.num_programs(1)

    def step(first):
        n2 = n2_ref[...]
        gate = jnp.dot(n2, wg_ref[...], preferred_element_type=F32)
        up = jnp.dot(n2, wu_ref[...], preferred_element_type=F32)

        prev = jnp.where(m % tiles_per_seq == 0, 0.0, tail_ref[f])
        tail_ref[f] = gate[tm - SUBLANES:tm, :]

        row8 = lax.broadcasted_iota(jnp.int32, prev.shape, 0)
        cw = cw_ref[...]
        conv = cb_ref[...] + cw[FFN_CONV - 1:FFN_CONV, :] * gate
        for j in range(1, FFN_CONV):
            r = pltpu.roll(gate, j, axis=0)
            top = jnp.where(row8 < j, pltpu.roll(prev, j, axis=0), r[0:SUBLANES, :])
            gs = jnp.concatenate([top, r[SUBLANES:, :]], axis=0)
            conv = conv + cw[FFN_CONV - 1 - j:FFN_CONV - j, :] * gs
        hid_b = ((conv * _sigmoid(conv)) * up).astype(BF16)
        tf = hid_b.shape[1]
        for c in range(y_ref.shape[1] // tf):
            cols = slice(c * tf, (c + 1) * tf)
            acc = x1_ref[:, cols] if first else y_ref[:, cols]
            y_ref[:, cols] = acc + jnp.dot(hid_b, wd_ref[:, cols], preferred_element_type=F32)

    @pl.when(f == 0)
    def _():
        step(True)

    @pl.when(f > 0)
    def _():
        step(False)

    if final_norm:
        @pl.when(f == nf - 1)
        def _():
            y_ref[...] = _rmsnorm(y_ref[...], gf_ref[...])


def _ffn(n2, x1, w_up, conv_w, conv_b, w_down, gf, *, seq, tm, tf, final_norm):
    T, D = n2.shape
    F = w_down.shape[0]
    nf = F // tf
    kern = functools.partial(_ffn_kernel, tm=tm, tiles_per_seq=seq // tm, final_norm=final_norm)
    return pl.pallas_call(
        kern,
        grid=(T // tm, nf),
        in_specs=[
            pl.BlockSpec((tm, D), lambda m, f: (m, 0)),
            pl.BlockSpec((tm, D), lambda m, f: (m, 0)),
            pl.BlockSpec((D, tf), lambda m, f: (0, f)),
            pl.BlockSpec((D, tf), lambda m, f: (0, nf + f)),
            pl.BlockSpec((FFN_CONV, tf), lambda m, f: (0, f)),
            pl.BlockSpec((1, tf), lambda m, f: (0, f)),
            pl.BlockSpec((tf, D), lambda m, f: (f, 0)),
            pl.BlockSpec((1, D), lambda m, f: (0, 0)),
        ],
        out_specs=pl.BlockSpec((tm, D), lambda m, f: (m, 0)),
        out_shape=jax.ShapeDtypeStruct((T, D), F32),
        scratch_shapes=[pltpu.VMEM((nf, SUBLANES, tf), F32)],
        compiler_params=pltpu.CompilerParams(
            dimension_semantics=("arbitrary", "arbitrary"),
            vmem_limit_bytes=VMEM_LIMIT_FFN),
        name="convffn",
    )(n2, x1, w_up, w_up, conv_w, conv_b, w_down, gf)


def kernel(x, norm_mix_g, w_in, b_gate_m, mlstm_norm_g, lru_conv_w, lru_conv_b, lru_wa, lru_ba,
           lru_wx, lru_bx, lru_lambda, w_out, norm_ffn_g, w_up, ffn_conv_w, ffn_conv_b, w_down,
           norm_final_g):
    B, S, D = x.shape
    T = B * S
    depth = w_in.shape[0]
    H = MLSTM_HEADS
    dh = mlstm_norm_g.shape[-1]
    d_m = H * dh
    d_r = lru_conv_w.shape[-1]
    assert d_m == d_r, "column-block addressing assumes equal head-group widths"
    n_gate = 2 * H
    gate_lo = 4 * d_m

    xf = x.reshape(T, D)
    for l in range(depth):
        b_if = jnp.pad(b_gate_m[l].astype(F32), (0, LANES - n_gate)).reshape(1, LANES)
        proj, gates = _inproj(xf, norm_mix_g[l].reshape(1, D), jnp.swapaxes(w_in[l], 0, 1), b_if,
                              gate_lo=gate_lo, n_gate=n_gate, tm=1024, tn=1024)

        w_ax = (0.5 * jnp.concatenate([lru_wa[l], lru_wx[l]], axis=-1)).astype(BF16)
        mix, w_up_bf, w_down_bf, w_out_bf = _mixers(
            proj, gates, mlstm_norm_g[l].reshape(H, 1, dh),
            lru_conv_w[l], lru_conv_b[l].reshape(1, d_r), w_ax,
            lru_ba[l].reshape(1, d_r), lru_bx[l].reshape(1, d_r), lru_lambda[l].reshape(1, d_r),
            w_up[l], w_down[l], w_out[l], batch=B, seq=S, heads=H, dh=dh, rows=MIXER_ROWS)

        x1, n2 = _outproj(xf, mix, w_out_bf, norm_ffn_g[l].reshape(1, D), tm=512)

        xf = _ffn(n2, x1, w_up_bf, ffn_conv_w[l], ffn_conv_b[l].reshape(1, -1),
                  w_down_bf, norm_final_g.reshape(1, D), seq=S, tm=1024, tf=512,
                  final_norm=(l == depth - 1))
    return xf.reshape(B, S, D)
```

```python
import functools
import math

import jax
import jax.numpy as jnp
from jax import lax
from jax.experimental import pallas as pl
from jax.experimental.pallas import tpu as pltpu

F32 = jnp.float32
BF16 = jnp.bfloat16
EPS = 1e-6
LANES = 128
SUBLANES = 8
SUBLANE_BITS = SUBLANES.bit_length() - 1
VMEM_LIMIT = 56 * 1024 * 1024
VMEM_LIMIT_FFN = 61 * 1024 * 1024

MLSTM_HEADS = 4
LRU_BLOCKS = 8
LRU_CONV = 4
LRU_C = 8.0
FFN_CONV = 3
MIXER_ROWS = 256


def _sigmoid(x):
    return 0.5 * jnp.tanh(0.5 * x) + 0.5


def _log_sigmoid(x):
    return jnp.minimum(x, 0.0) - jnp.log1p(jnp.exp(-jnp.abs(x)))


def _softplus(x):
    return jnp.maximum(x, 0.0) + jnp.log1p(jnp.exp(-jnp.abs(x)))


def _gelu_tanh(x):
    c = math.sqrt(2.0 / math.pi)
    t = jnp.tanh(x * (c + (c * 0.044715) * (x * x)))
    hx = 0.5 * x
    return hx + hx * t


def _rmsnorm(x, g):
    return x * lax.rsqrt(jnp.mean(x * x, axis=-1, keepdims=True) + EPS) * g


_NT = (((1,), (1,)), ((), ()))


def _inproj_kernel(x_ref, g_ref, w_ref, nxt_ref, wif_ref, bif_ref, proj_ref, gate_ref, n1_ref,
                   *, first_shifted, shift):
    n = pl.program_id(1)

    @pl.when(n == 0)
    def _():
        n1 = _rmsnorm(x_ref[...], g_ref[...]).astype(BF16)
        n1_ref[...] = n1
        gate_ref[...] = lax.dot_general(n1, wif_ref[...].astype(BF16), _NT,
                                        preferred_element_type=F32) + bif_ref[...]

    def project(w_bf):
        proj_ref[...] = lax.dot_general(n1_ref[...], w_bf, _NT,
                                        preferred_element_type=F32).astype(proj_ref.dtype)

    @pl.when(n < first_shifted)
    def _():
        project(w_ref[...].astype(BF16))

    @pl.when(n >= first_shifted)
    def _():
        project(jnp.concatenate([w_ref[shift:, :], nxt_ref[...]], axis=0).astype(BF16))


def _inproj(x, g, wt, b_if, *, gate_lo, n_gate, tm, tn):
    T, D = x.shape
    P = wt.shape[0]
    N = P - n_gate
    assert gate_lo % tn == 0 and N % tn == 0 and n_gate == SUBLANES and gate_lo % LANES == 0
    first_shifted = gate_lo // tn
    per = tn // n_gate
    kern = functools.partial(_inproj_kernel, first_shifted=first_shifted, shift=n_gate)
    return pl.pallas_call(
        kern,
        grid=(T // tm, N // tn),
        in_specs=[
            pl.BlockSpec((tm, D), lambda m, n: (m, 0)),
            pl.BlockSpec((1, D), lambda m, n: (0, 0)),
            pl.BlockSpec((tn, D), lambda m, n: (n, 0)),
            pl.BlockSpec((n_gate, D), lambda m, n: (jnp.maximum(n, first_shifted) * per + per, 0)),
            pl.BlockSpec((LANES, D), lambda m, n: (gate_lo // LANES, 0)),
            pl.BlockSpec((1, LANES), lambda m, n: (0, 0)),
        ],
        out_specs=[
            pl.BlockSpec((tm, tn), lambda m, n: (m, n)),
            pl.BlockSpec((tm, LANES), lambda m, n: (m, 0)),
        ],
        out_shape=[
            jax.ShapeDtypeStruct((T, N), BF16),
            jax.ShapeDtypeStruct((T, LANES), F32),
        ],
        scratch_shapes=[pltpu.VMEM((tm, D), BF16)],
        compiler_params=pltpu.CompilerParams(
            dimension_semantics=("arbitrary", "arbitrary"),
            vmem_limit_bytes=VMEM_LIMIT),
        name="inproj",
    )(x, g, wt, wt, wt, b_if)


def _causal_mask(L):
    return (lax.broadcasted_iota(jnp.int32, (L, L), 0)
            >= lax.broadcasted_iota(jnp.int32, (L, L), 1))


def _mlstm_init(s_ref, m_ref, tri_ref, *, chunk):
    @pl.when(pl.program_id(1) == 0)
    def _():
        s_ref[...] = jnp.zeros_like(s_ref)
        m_ref[...] = jnp.zeros_like(m_ref)

    @pl.when((pl.program_id(0) == 0) & (pl.program_id(1) == 0))
    def _():
        tri_ref[...] = jnp.where(_causal_mask(chunk), 1.0, 0.0).astype(BF16)


def _mlstm_main(q_ref, k_ref, v_ref, o_ref, gate_ref, hg_ref, out_ref,
                s_ref, m_ref, tri_ref, *, heads, dh, chunk):
    L = chunk
    scale = dh ** -0.5
    causal = _causal_mask(L)

    def split(x, parts):
        out = []
        for _ in range(parts - 1):
            p = x.astype(BF16)
            out.append(p)
            x = x - p.astype(F32)
        out.append(x.astype(BF16))
        return out

    def rep(x, width):
        return jnp.concatenate([x] * (width // LANES), axis=1)

    g = gate_ref[...]
    tri = tri_ref[...]

    g8 = g.T[0:SUBLANES, :]
    g8r = pltpu.roll(g8, heads, axis=0)
    sub = lax.broadcasted_iota(jnp.int32, g8.shape, 0)
    ig = jnp.where(sub < heads, g8, g8r)
    fg = jnp.where(sub < heads, g8r, g8)
    logf = jnp.concatenate(split(_log_sigmoid(fg), 2), axis=0)
    br = lax.dot_general(logf, tri, _NT, preferred_element_type=F32)
    b = br[0:SUBLANES, :] + br[SUBLANES:, :]

    logf_pad = jnp.concatenate([logf, jnp.zeros((LANES - 2 * SUBLANES, L), BF16)], axis=0)
    b_nat = lax.dot_general(tri, logf_pad, _NT, preferred_element_type=F32)
    a_row = ig - b
    b_tot = b[:, L - 1:L]
    m_prev = m_ref[:, 0:1]
    m_new = jnp.maximum(b_tot + m_prev, b_tot + jnp.max(a_row, axis=1, keepdims=True))
    decay = jnp.exp(b_tot + m_prev - m_new)
    end_shift = b_tot - m_new
    m_ref[...] = jnp.broadcast_to(m_new, m_ref.shape)
    ones_blk = jnp.ones((L, LANES), BF16)

    for h in range(heads):
        sl = slice(h * dh, (h + 1) * dh)
        q = q_ref[:, sl] * scale
        k = k_ref[:, sl]
        v = v_ref[:, sl]
        ig_r = jnp.broadcast_to(g[:, h:h + 1], (L, LANES))
        b_r = jnp.broadcast_to(b_nat[:, h:h + 1] + b_nat[:, SUBLANES + h:SUBLANES + h + 1],
                               (L, LANES))
        a_h = a_row[h:h + 1, :]
        mp = m_prev[h:h + 1, :]

        s = lax.dot_general(q, k, _NT, preferred_element_type=F32)
        cm = jnp.max(jnp.where(causal, a_h, -jnp.inf), axis=-1, keepdims=True)
        mm_r = jnp.maximum(jnp.broadcast_to(cm, (L, LANES)), mp)
        w = s * jnp.exp(jnp.where(causal, a_h - rep(mm_r, L), -jnp.inf))
        einter_r = jnp.exp(mp - mm_r)
        floor_r = jnp.exp(-(b_r + mm_r))
        eend_r = jnp.exp(end_shift[h:h + 1, :] + ig_r - b_r)

        state = s_ref[h]
        intra = jnp.dot(w.astype(BF16), jnp.concatenate([v, ones_blk], axis=1),
                        preferred_element_type=F32)
        inter = jnp.dot(q, state.astype(BF16), preferred_element_type=F32)
        tot = intra + rep(einter_r, dh + LANES) * inter
        num = tot[:, :dh]
        den_r = tot[:, dh:]
        inv_r = 1.0 / jnp.maximum(jnp.abs(den_r), floor_r)
        ssq_r = jnp.broadcast_to(jnp.sum(num * num, axis=-1, keepdims=True), (L, LANES))
        rs_r = inv_r * lax.rsqrt(ssq_r * (inv_r * inv_r) * (1.0 / dh) + EPS)
        half_g = 0.5 * hg_ref[h]
        gain = half_g + half_g * jnp.tanh(0.5 * o_ref[:, sl].astype(F32))
        out_ref[:, sl] = ((num * rep(rs_r, dh)) * gain).astype(out_ref.dtype)

        ev = jnp.concatenate(
            [(rep(eend_r, dh) * v.astype(F32)).astype(BF16), eend_r.astype(BF16)], axis=1)
        loc = lax.dot_general(k, ev, (((0,), (0,)), ((), ())), preferred_element_type=F32)
        s_ref[h] = decay[h:h + 1, :] * state + loc


def _lru_init(xtail_ref, hc_ref, perm_ref, unperm_ref, *, ts):
    seg = ts // SUBLANES

    @pl.when(pl.program_id(1) == 0)
    def _():
        xtail_ref[...] = jnp.zeros_like(xtail_ref)
        hc_ref[...] = jnp.zeros_like(hc_ref)

    @pl.when((pl.program_id(0) == 0) & (pl.program_id(1) == 0))
    def _():
        ri = lax.broadcasted_iota(jnp.int32, (ts, ts), 0)
        ci = lax.broadcasted_iota(jnp.int32, (ts, ts), 1)
        t_of_row = (ri & (SUBLANES - 1)) * seg + (ri >> SUBLANE_BITS)
        t_of_col = (ci & (SUBLANES - 1)) * seg + (ci >> SUBLANE_BITS)
        perm_ref[...] = jnp.where(ci == t_of_row, 1.0, 0.0).astype(BF16)
        unperm_ref[...] = jnp.where(ri == t_of_col, 1.0, 0.0).astype(BF16)


def _lru_main(xr_ref, gr_ref, cw_ref, cb_ref, wax_ref, ba_ref, bx_ref, lam_ref, out_ref,
              xtail_ref, hc_ref, perm_ref, unperm_ref, a_scr, u_scr, c_scr, h_scr,
              *, ts, blocks, bd):
    C = blocks * bd
    seg = ts // SUBLANES
    halo = LRU_CONV - 1

    xb = xr_ref[...]
    xp = jnp.dot(perm_ref[...], xb, preferred_element_type=F32)
    gp = jnp.dot(perm_ref[...], gr_ref[...], preferred_element_type=F32)
    gelu_g = _gelu_tanh(gp)
    prev = xtail_ref[...]
    xtail_ref[...] = xb[ts - SUBLANES:ts, :].astype(F32)
    row8 = lax.broadcasted_iota(jnp.int32, (SUBLANES, C), 0)

    wrapped = []
    for k in range(1, halo + 1):
        r0 = (seg - k) * SUBLANES
        up = pltpu.roll(xp[r0:r0 + SUBLANES, :], 1, axis=0)
        wrapped.append(jnp.where(row8 == 0, prev[SUBLANES - k:SUBLANES - k + 1, :], up))

    cw = cw_ref[...]
    xc = cb_ref[...] + cw[LRU_CONV - 1:LRU_CONV, :] * xp
    for j in range(1, LRU_CONV):
        xs = jnp.concatenate([wrapped[k - 1] for k in range(j, 0, -1)]
                             + [xp[0:ts - j * SUBLANES, :]], axis=0)
        xc = xc + cw[LRU_CONV - 1 - j:LRU_CONV - j, :] * xs

    xcb = xc.astype(BF16)
    half_xc = 0.5 * xc
    k = (-0.5 * LRU_C * math.log2(math.e)) * _softplus(-lam_ref[...])
    half_ba = 0.5 * ba_ref[...]
    half_bx = 0.5 * bx_ref[...]
    for n in range(blocks):
        sl = slice(n * bd, (n + 1) * bd)
        ax = jnp.dot(xcb[:, sl], wax_ref[n], preferred_element_type=F32)
        t_r = jnp.tanh(ax[:, :bd] + half_ba[:, sl])
        t_i = jnp.tanh(ax[:, bd:] + half_bx[:, sl])
        a = jnp.exp2(k[:, sl] + k[:, sl] * t_r)
        ixc = half_xc[:, sl] + half_xc[:, sl] * t_i
        y = 1.0 - a * a
        a_scr[:, sl] = a
        u_scr[:, sl] = jnp.where(y > 0.0, y * lax.rsqrt(y), 0.0) * ixc

    def body(v, carry):
        h, c = carry
        rows = pl.ds(pl.multiple_of(v * SUBLANES, SUBLANES), SUBLANES)
        av = a_scr[rows, :]
        h = av * h + u_scr[rows, :]
        c = av * c
        h_scr[rows, :] = h
        c_scr[rows, :] = c
        return h, c

    init = (jnp.zeros((SUBLANES, C), F32), jnp.ones((SUBLANES, C), F32))
    h_end, c_end = lax.fori_loop(0, seg, body, init, unroll=4)

    state = hc_ref[0:1, :]
    entering = []
    for s in range(SUBLANES):
        entering.append(state)
        state = c_end[s:s + 1, :] * state + h_end[s:s + 1, :]
    hc_ref[0:1, :] = state
    entering = jnp.concatenate(entering, axis=0)

    h = (h_scr[...].reshape(seg, SUBLANES, C)
         + c_scr[...].reshape(seg, SUBLANES, C) * entering[None]).reshape(ts, C)
    outp = (h * gelu_g).astype(BF16)
    out_ref[...] = jnp.dot(unperm_ref[...], outp,
                           preferred_element_type=F32).astype(out_ref.dtype)


def _mixers_kernel(proj_ref, gate_ref, hg_ref,
                   cw_ref, cb_ref, wax_ref, ba_ref, bx_ref, lam_ref,
                   wup_ref, wdn_ref, wout_ref,
                   mix_ref, wup_bf_ref, wdn_bf_ref, wout_bf_ref,
                   s_ref, m_ref, tri_ref,
                   xtail_ref, hc_ref, perm_ref, unperm_ref, a_scr, u_scr, c_scr, h_scr,
                   *, heads, dh, rows, blocks, bd):
    W = heads * dh
    q_ref, k_ref, v_ref, o_ref, xr_ref, gr_ref = (
        proj_ref.at[:, j * W:(j + 1) * W] for j in range(6))
    hm_ref = mix_ref.at[:, 0:W]
    hr_ref = mix_ref.at[:, W:2 * W]

    _mlstm_init(s_ref, m_ref, tri_ref, chunk=rows)
    _lru_init(xtail_ref, hc_ref, perm_ref, unperm_ref, ts=rows)

    wup_bf_ref[...] = wup_ref[...].astype(BF16)
    wdn_bf_ref[...] = wdn_ref[...].astype(BF16)
    wout_bf_ref[...] = wout_ref[...].astype(BF16)

    _mlstm_main(q_ref, k_ref, v_ref, o_ref, gate_ref, hg_ref, hm_ref,
                s_ref, m_ref, tri_ref, heads=heads, dh=dh, chunk=rows)
    _lru_main(xr_ref, gr_ref, cw_ref, cb_ref, wax_ref, ba_ref, bx_ref, lam_ref, hr_ref,
              xtail_ref, hc_ref, perm_ref, unperm_ref, a_scr, u_scr, c_scr, h_scr,
              ts=rows, blocks=blocks, bd=bd)


def _mixers(proj, gates, head_g, conv_w, conv_b, w_ax, ba, bx, lam, w_up, w_down, w_out,
            *, batch, seq, heads, dh, rows):
    T = batch * seq
    W = heads * dh
    blocks, bd, _ = w_ax.shape
    C = blocks * bd
    assert W == C and bd == LANES and rows % (SUBLANES * SUBLANES) == 0
    ns = seq // rows
    steps = batch * ns
    row_blk = lambda b, s: b * ns + s

    def slab(w):
        n = w.shape[0] // steps
        assert n * steps == w.shape[0] and n % 16 == 0
        return pl.BlockSpec((n, w.shape[1]), lambda b, s: (row_blk(b, s), 0))

    rows_of = lambda width: pl.BlockSpec((rows, width), lambda b, s: (row_blk(b, s), 0))
    vec = lambda n: pl.BlockSpec((n, C), lambda b, s: (0, 0))
    kern = functools.partial(_mixers_kernel, heads=heads, dh=dh, rows=rows, blocks=blocks, bd=bd)
    return pl.pallas_call(
        kern,
        grid=(batch, ns),
        in_specs=[
            rows_of(proj.shape[1]),
            rows_of(LANES),
            pl.BlockSpec((heads, 1, dh), lambda b, s: (0, 0, 0)),
            vec(LRU_CONV), vec(1),
            pl.BlockSpec((blocks, bd, 2 * bd), lambda b, s: (0, 0, 0)),
            vec(1), vec(1), vec(1),
            slab(w_up), slab(w_down), slab(w_out),
        ],
        out_specs=[rows_of(W + C), slab(w_up), slab(w_down), slab(w_out)],
        out_shape=[jax.ShapeDtypeStruct((T, W + C), BF16),
                   jax.ShapeDtypeStruct(w_up.shape, BF16),
                   jax.ShapeDtypeStruct(w_down.shape, BF16),
                   jax.ShapeDtypeStruct(w_out.shape, BF16)],
        scratch_shapes=[
            pltpu.VMEM((heads, dh, dh + LANES), F32),
            pltpu.VMEM((SUBLANES, LANES), F32),
            pltpu.VMEM((rows, rows), BF16),
            pltpu.VMEM((SUBLANES, C), F32),
            pltpu.VMEM((SUBLANES, C), F32),
            pltpu.VMEM((rows, rows), BF16),
            pltpu.VMEM((rows, rows), BF16),
        ] + [pltpu.VMEM((rows, C), F32)] * 4,
        compiler_params=pltpu.CompilerParams(
            dimension_semantics=("arbitrary", "arbitrary"),
            vmem_limit_bytes=VMEM_LIMIT),
        name="mixers",
    )(proj, gates, head_g, conv_w, conv_b, w_ax, ba, bx, lam, w_up, w_down, w_out)


def _outproj_kernel(x_ref, mix_ref, w_ref, g_ref, x1_ref, n2_ref):
    x1 = x_ref[...] + jnp.dot(mix_ref[...], w_ref[...], preferred_element_type=F32)
    x1_ref[...] = x1
    n2_ref[...] = _rmsnorm(x1, g_ref[...]).astype(n2_ref.dtype)


def _outproj(x, mix, w_out, g, *, tm):
    T, D = x.shape
    Dm = mix.shape[1]
    return pl.pallas_call(
        _outproj_kernel,
        grid=(T // tm,),
        in_specs=[
            pl.BlockSpec((tm, D), lambda m: (m, 0)),
            pl.BlockSpec((tm, Dm), lambda m: (m, 0)),
            pl.BlockSpec((Dm, D), lambda m: (0, 0)),
            pl.BlockSpec((1, D), lambda m: (0, 0)),
        ],
        out_specs=[
            pl.BlockSpec((tm, D), lambda m: (m, 0)),
            pl.BlockSpec((tm, D), lambda m: (m, 0)),
        ],
        out_shape=[
            jax.ShapeDtypeStruct((T, D), F32),
            jax.ShapeDtypeStruct((T, D), BF16),
        ],
        compiler_params=pltpu.CompilerParams(
            dimension_semantics=("arbitrary",),
            vmem_limit_bytes=VMEM_LIMIT),
        name="outproj",
    )(x, mix, w_out, g)


def _ffn_kernel(n2_ref, x1_ref, wg_ref, wu_ref, cw_ref, cb_ref, wd_ref, gf_ref, y_ref,
                tail_ref, *, tm, tiles_per_seq, final_norm):
    m = pl.program_id(0)
    f = pl.program_id(1)
    nf = pl.num_programs(1)

    def step(first):
        n2 = n2_ref[...]
        gate = jnp.dot(n2, wg_ref[...], preferred_element_type=F32)
        up = jnp.dot(n2, wu_ref[...], preferred_element_type=F32)

        prev = jnp.where(m % tiles_per_seq == 0, 0.0, tail_ref[f])
        tail_ref[f] = gate[tm - SUBLANES:tm, :]

        row8 = lax.broadcasted_iota(jnp.int32, prev.shape, 0)
        cw = cw_ref[...]
        conv = cb_ref[...] + cw[FFN_CONV - 1:FFN_CONV, :] * gate
        for j in range(1, FFN_CONV):
            r = pltpu.roll(gate, j, axis=0)
            top = jnp.where(row8 < j, pltpu.roll(prev, j, axis=0), r[0:SUBLANES, :])
            gs = jnp.concatenate([top, r[SUBLANES:, :]], axis=0)
            conv = conv + cw[FFN_CONV - 1 - j:FFN_CONV - j, :] * gs
        hid_b = ((conv * _sigmoid(conv)) * up).astype(BF16)
        tf = hid_b.shape[1]
        for c in range(y_ref.shape[1] // tf):
            cols = slice(c * tf, (c + 1) * tf)
            acc = x1_ref[:, cols] if first else y_ref[:, cols]
            y_ref[:, cols] = acc + jnp.dot(hid_b, wd_ref[:, cols], preferred_element_type=F32)

    @pl.when(f == 0)
    def _():
        step(True)

    @pl.when(f > 0)
    def _():
        step(False)

    if final_norm:
        @pl.when(f == nf - 1)
        def _():
            y_ref[...] = _rmsnorm(y_ref[...], gf_ref[...])


def _ffn(n2, x1, w_up, conv_w, conv_b, w_down, gf, *, seq, tm, tf, final_norm):
    T, D = n2.shape
    F = w_down.shape[0]
    nf = F // tf
    kern = functools.partial(_ffn_kernel, tm=tm, tiles_per_seq=seq // tm, final_norm=final_norm)
    return pl.pallas_call(
        kern,
        grid=(T // tm, nf),
        in_specs=[
            pl.BlockSpec((tm, D), lambda m, f: (m, 0)),
            pl.BlockSpec((tm, D), lambda m, f: (m, 0)),
            pl.BlockSpec((D, tf), lambda m, f: (0, f)),
            pl.BlockSpec((D, tf), lambda m, f: (0, nf + f)),
            pl.BlockSpec((FFN_CONV, tf), lambda m, f: (0, f)),
            pl.BlockSpec((1, tf), lambda m, f: (0, f)),
            pl.BlockSpec((tf, D), lambda m, f: (f, 0)),
            pl.BlockSpec((1, D), lambda m, f: (0, 0)),
        ],
        out_specs=pl.BlockSpec((tm, D), lambda m, f: (m, 0)),
        out_shape=jax.ShapeDtypeStruct((T, D), F32),
        scratch_shapes=[pltpu.VMEM((nf, SUBLANES, tf), F32)],
        compiler_params=pltpu.CompilerParams(
            dimension_semantics=("arbitrary", "arbitrary"),
            vmem_limit_bytes=VMEM_LIMIT_FFN),
        name="convffn",
    )(n2, x1, w_up, w_up, conv_w, conv_b, w_down, gf)


def kernel(x, norm_mix_g, w_in, b_gate_m, mlstm_norm_g, lru_conv_w, lru_conv_b, lru_wa, lru_ba,
           lru_wx, lru_bx, lru_lambda, w_out, norm_ffn_g, w_up, ffn_conv_w, ffn_conv_b, w_down,
           norm_final_g):
    B, S, D = x.shape
    T = B * S
    depth = w_in.shape[0]
    H = MLSTM_HEADS
    dh = mlstm_norm_g.shape[-1]
    d_m = H * dh
    d_r = lru_conv_w.shape[-1]
    assert d_m == d_r, "column-block addressing assumes equal head-group widths"
    n_gate = 2 * H
    gate_lo = 4 * d_m

    xf = x.reshape(T, D)
    for l in range(depth):
        b_if = jnp.pad(b_gate_m[l].astype(F32), (0, LANES - n_gate)).reshape(1, LANES)
        proj, gates = _inproj(xf, norm_mix_g[l].reshape(1, D), jnp.swapaxes(w_in[l], 0, 1), b_if,
                              gate_lo=gate_lo, n_gate=n_gate, tm=1024, tn=1024)

        w_ax = (0.5 * jnp.concatenate([lru_wa[l], lru_wx[l]], axis=-1)).astype(BF16)
        mix, w_up_bf, w_down_bf, w_out_bf = _mixers(
            proj, gates, mlstm_norm_g[l].reshape(H, 1, dh),
            lru_conv_w[l], lru_conv_b[l].reshape(1, d_r), w_ax,
            lru_ba[l].reshape(1, d_r), lru_bx[l].reshape(1, d_r), lru_lambda[l].reshape(1, d_r),
            w_up[l], w_down[l], w_out[l], batch=B, seq=S, heads=H, dh=dh, rows=MIXER_ROWS)

        x1, n2 = _outproj(xf, mix, w_out_bf, norm_ffn_g[l].reshape(1, D), tm=512)

        xf = _ffn(n2, x1, w_up_bf, ffn_conv_w[l], ffn_conv_b[l].reshape(1, -1),
                  w_down_bf, norm_final_g.reshape(1, D), seq=S, tm=1024, tf=512,
                  final_norm=(l == depth - 1))
    return xf.reshape(B, S, D)
```

```python
import functools
import math

import jax
import jax.numpy as jnp
from jax import lax
from jax.experimental import pallas as pl
from jax.experimental.pallas import tpu as pltpu

F32 = jnp.float32
BF16 = jnp.bfloat16
EPS = 1e-6
LANES = 128
SUBLANES = 8
SUBLANE_BITS = SUBLANES.bit_length() - 1
VMEM_LIMIT = 56 * 1024 * 1024
VMEM_LIMIT_FFN = 61 * 1024 * 1024

MLSTM_HEADS = 4
LRU_BLOCKS = 8
LRU_CONV = 4
LRU_C = 8.0
FFN_CONV = 3
MIXER_ROWS = 256


def _sigmoid(x):
    return 0.5 * jnp.tanh(0.5 * x) + 0.5


def _log_sigmoid(x):
    return jnp.minimum(x, 0.0) - jnp.log1p(jnp.exp(-jnp.abs(x)))


def _softplus(x):
    return jnp.maximum(x, 0.0) + jnp.log1p(jnp.exp(-jnp.abs(x)))


def _gelu_tanh(x):
    c = math.sqrt(2.0 / math.pi)
    t = jnp.tanh(x * (c + (c * 0.044715) * (x * x)))
    hx = 0.5 * x
    return hx + hx * t


def _rmsnorm(x, g):
    return x * lax.rsqrt(jnp.mean(x * x, axis=-1, keepdims=True) + EPS) * g


_NT = (((1,), (1,)), ((), ()))


def _inproj_kernel(x_ref, g_ref, w_ref, nxt_ref, wif_ref, bif_ref, proj_ref, gate_ref, n1_ref,
                   *, first_shifted, shift):
    n = pl.program_id(1)

    @pl.when(n == 0)
    def _():
        n1 = _rmsnorm(x_ref[...], g_ref[...]).astype(BF16)
        n1_ref[...] = n1
        gate_ref[...] = lax.dot_general(n1, wif_ref[...].astype(BF16), _NT,
                                        preferred_element_type=F32) + bif_ref[...]

    tn = w_ref.shape[0]
    group = proj_ref.shape[1] // tn

    def project(w_bf, slot):
        proj_ref[:, slot * tn:(slot + 1) * tn] = lax.dot_general(
            n1_ref[...], w_bf, _NT, preferred_element_type=F32).astype(proj_ref.dtype)

    for slot in range(group):
        @pl.when((n < first_shifted) & (n % group == slot))
        def _(slot=slot):
            project(w_ref[...].astype(BF16), slot)

        @pl.when((n >= first_shifted) & (n % group == slot))
        def _(slot=slot):
            project(jnp.concatenate([w_ref[shift:, :], nxt_ref[...]], axis=0).astype(BF16), slot)


def _inproj(x, g, wt, b_if, *, gate_lo, n_gate, tm, tn, group):
    T, D = x.shape
    P = wt.shape[0]
    N = P - n_gate
    assert gate_lo % tn == 0 and N % (group * tn) == 0 and n_gate == SUBLANES and gate_lo % LANES == 0
    first_shifted = gate_lo // tn
    per = tn // n_gate
    kern = functools.partial(_inproj_kernel, first_shifted=first_shifted, shift=n_gate)
    return pl.pallas_call(
        kern,
        grid=(T // tm, N // tn),
        in_specs=[
            pl.BlockSpec((tm, D), lambda m, n: (m, 0)),
            pl.BlockSpec((1, D), lambda m, n: (0, 0)),
            pl.BlockSpec((tn, D), lambda m, n: (n, 0)),
            pl.BlockSpec((n_gate, D), lambda m, n: (jnp.maximum(n, first_shifted) * per + per, 0)),
            pl.BlockSpec((LANES, D), lambda m, n: (gate_lo // LANES, 0)),
            pl.BlockSpec((1, LANES), lambda m, n: (0, 0)),
        ],
        out_specs=[
            pl.BlockSpec((tm, group * tn), lambda m, n: (m, n // group)),
            pl.BlockSpec((tm, LANES), lambda m, n: (m, 0)),
        ],
        out_shape=[
            jax.ShapeDtypeStruct((T, N), BF16),
            jax.ShapeDtypeStruct((T, LANES), F32),
        ],
        scratch_shapes=[pltpu.VMEM((tm, D), BF16)],
        compiler_params=pltpu.CompilerParams(
            dimension_semantics=("arbitrary", "arbitrary"),
            vmem_limit_bytes=VMEM_LIMIT),
        name="inproj",
    )(x, g, wt, wt, wt, b_if)


def _causal_mask(L):
    return (lax.broadcasted_iota(jnp.int32, (L, L), 0)
            >= lax.broadcasted_iota(jnp.int32, (L, L), 1))


def _mlstm_init(s_ref, m_ref, tri_ref, *, chunk):
    @pl.when(pl.program_id(1) == 0)
    def _():
        s_ref[...] = jnp.zeros_like(s_ref)
        m_ref[...] = jnp.zeros_like(m_ref)

    @pl.when((pl.program_id(0) == 0) & (pl.program_id(1) == 0))
    def _():
        tri_ref[...] = jnp.where(_causal_mask(chunk), 1.0, 0.0).astype(BF16)


def _mlstm_main(q_ref, k_ref, v_ref, o_ref, gate_ref, hg_ref, out_ref,
                s_ref, m_ref, tri_ref, *, heads, dh, chunk):
    L = chunk
    scale = dh ** -0.5
    causal = _causal_mask(L)

    def split(x, parts):
        out = []
        for _ in range(parts - 1):
            p = x.astype(BF16)
            out.append(p)
            x = x - p.astype(F32)
        out.append(x.astype(BF16))
        return out

    def rep(x, width):
        return jnp.concatenate([x] * (width // LANES), axis=1)

    g = gate_ref[...]
    tri = tri_ref[...]

    g8 = g.T[0:SUBLANES, :]
    g8r = pltpu.roll(g8, heads, axis=0)
    sub = lax.broadcasted_iota(jnp.int32, g8.shape, 0)
    ig = jnp.where(sub < heads, g8, g8r)
    fg = jnp.where(sub < heads, g8r, g8)
    logf = jnp.concatenate(split(_log_sigmoid(fg), 2), axis=0)
    br = lax.dot_general(logf, tri, _NT, preferred_element_type=F32)
    b = br[0:SUBLANES, :] + br[SUBLANES:, :]

    logf_pad = jnp.concatenate([logf, jnp.zeros((LANES - 2 * SUBLANES, L), BF16)], axis=0)
    b_nat = lax.dot_general(tri, logf_pad, _NT, preferred_element_type=F32)
    a_row = ig - b
    b_tot = b[:, L - 1:L]
    m_prev = m_ref[:, 0:1]
    m_new = jnp.maximum(b_tot + m_prev, b_tot + jnp.max(a_row, axis=1, keepdims=True))
    decay = jnp.exp(b_tot + m_prev - m_new)
    end_shift = b_tot - m_new
    m_ref[...] = jnp.broadcast_to(m_new, m_ref.shape)
    ones_blk = jnp.ones((L, LANES), BF16)

    for h in range(heads):
        sl = slice(h * dh, (h + 1) * dh)
        q = q_ref[:, sl] * scale
        k = k_ref[:, sl]
        v = v_ref[:, sl]
        ig_r = jnp.broadcast_to(g[:, h:h + 1], (L, LANES))
        b_r = jnp.broadcast_to(b_nat[:, h:h + 1] + b_nat[:, SUBLANES + h:SUBLANES + h + 1],
                               (L, LANES))
        a_h = a_row[h:h + 1, :]
        mp = m_prev[h:h + 1, :]

        s = lax.dot_general(q, k, _NT, preferred_element_type=F32)
        cm = jnp.max(jnp.where(causal, a_h, -jnp.inf), axis=-1, keepdims=True)
        mm_r = jnp.maximum(jnp.broadcast_to(cm, (L, LANES)), mp)
        w = s * jnp.exp(jnp.where(causal, a_h - rep(mm_r, L), -jnp.inf))
        einter_r = jnp.exp(mp - mm_r)
        floor_r = jnp.exp(-(b_r + mm_r))
        eend_r = jnp.exp(end_shift[h:h + 1, :] + ig_r - b_r)

        state = s_ref[h]
        intra = jnp.dot(w.astype(BF16), jnp.concatenate([v, ones_blk], axis=1),
                        preferred_element_type=F32)
        inter = jnp.dot(q, state.astype(BF16), preferred_element_type=F32)
        tot = intra + rep(einter_r, dh + LANES) * inter
        num = tot[:, :dh]
        den_r = tot[:, dh:]
        inv_r = 1.0 / jnp.maximum(jnp.abs(den_r), floor_r)
        ssq_r = jnp.broadcast_to(jnp.sum(num * num, axis=-1, keepdims=True), (L, LANES))
        rs_r = inv_r * lax.rsqrt(ssq_r * (inv_r * inv_r) * (1.0 / dh) + EPS)
        half_g = 0.5 * hg_ref[h]
        gain = half_g + half_g * jnp.tanh(0.5 * o_ref[:, sl].astype(F32))
        out_ref[:, sl] = ((num * rep(rs_r, dh)) * gain).astype(out_ref.dtype)

        ev = jnp.concatenate(
            [(rep(eend_r, dh) * v.astype(F32)).astype(BF16), eend_r.astype(BF16)], axis=1)
        loc = lax.dot_general(k, ev, (((0,), (0,)), ((), ())), preferred_element_type=F32)
        s_ref[h] = decay[h:h + 1, :] * state + loc


def _lru_init(xtail_ref, hc_ref, perm_ref, unperm_ref, *, ts):
    seg = ts // SUBLANES

    @pl.when(pl.program_id(1) == 0)
    def _():
        xtail_ref[...] = jnp.zeros_like(xtail_ref)
        hc_ref[...] = jnp.zeros_like(hc_ref)

    @pl.when((pl.program_id(0) == 0) & (pl.program_id(1) == 0))
    def _():
        ri = lax.broadcasted_iota(jnp.int32, (ts, ts), 0)
        ci = lax.broadcasted_iota(jnp.int32, (ts, ts), 1)
        t_of_row = (ri & (SUBLANES - 1)) * seg + (ri >> SUBLANE_BITS)
        t_of_col = (ci & (SUBLANES - 1)) * seg + (ci >> SUBLANE_BITS)
        perm_ref[...] = jnp.where(ci == t_of_row, 1.0, 0.0).astype(BF16)
        unperm_ref[...] = jnp.where(ri == t_of_col, 1.0, 0.0).astype(BF16)


def _lru_main(xr_ref, gr_ref, cw_ref, cb_ref, wax_ref, ba_ref, bx_ref, lam_ref, out_ref,
              xtail_ref, hc_ref, perm_ref, unperm_ref, a_scr, u_scr, c_scr, h_scr,
              *, ts, blocks, bd):
    C = blocks * bd
    seg = ts // SUBLANES
    halo = LRU_CONV - 1

    xb = xr_ref[...]
    xp = jnp.dot(perm_ref[...], xb, preferred_element_type=F32)
    gp = jnp.dot(perm_ref[...], gr_ref[...], preferred_element_type=F32)
    gelu_g = _gelu_tanh(gp)
    prev = xtail_ref[...]
    xtail_ref[...] = xb[ts - SUBLANES:ts, :].astype(F32)
    row8 = lax.broadcasted_iota(jnp.int32, (SUBLANES, C), 0)

    wrapped = []
    for k in range(1, halo + 1):
        r0 = (seg - k) * SUBLANES
        up = pltpu.roll(xp[r0:r0 + SUBLANES, :], 1, axis=0)
        wrapped.append(jnp.where(row8 == 0, prev[SUBLANES - k:SUBLANES - k + 1, :], up))

    cw = cw_ref[...]
    xc = cb_ref[...] + cw[LRU_CONV - 1:LRU_CONV, :] * xp
    for j in range(1, LRU_CONV):
        xs = jnp.concatenate([wrapped[k - 1] for k in range(j, 0, -1)]
                             + [xp[0:ts - j * SUBLANES, :]], axis=0)
        xc = xc + cw[LRU_CONV - 1 - j:LRU_CONV - j, :] * xs

    xcb = xc.astype(BF16)
    half_xc = 0.5 * xc
    k = (-0.5 * LRU_C * math.log2(math.e)) * _softplus(-lam_ref[...])
    half_ba = 0.5 * ba_ref[...]
    half_bx = 0.5 * bx_ref[...]
    for n in range(blocks):
        sl = slice(n * bd, (n + 1) * bd)
        ax = jnp.dot(xcb[:, sl], wax_ref[n], preferred_element_type=F32)
        t_r = jnp.tanh(ax[:, :bd] + half_ba[:, sl])
        t_i = jnp.tanh(ax[:, bd:] + half_bx[:, sl])
        a = jnp.exp2(k[:, sl] + k[:, sl] * t_r)
        ixc = half_xc[:, sl] + half_xc[:, sl] * t_i
        y = 1.0 - a * a
        a_scr[:, sl] = a
        u_scr[:, sl] = jnp.where(y > 0.0, y * lax.rsqrt(y), 0.0) * ixc

    def body(v, carry):
        h, c = carry
        rows = pl.ds(pl.multiple_of(v * SUBLANES, SUBLANES), SUBLANES)
        av = a_scr[rows, :]
        h = av * h + u_scr[rows, :]
        c = av * c
        h_scr[rows, :] = h
        c_scr[rows, :] = c
        return h, c

    init = (jnp.zeros((SUBLANES, C), F32), jnp.ones((SUBLANES, C), F32))
    h_end, c_end = lax.fori_loop(0, seg, body, init, unroll=4)

    state = hc_ref[0:1, :]
    entering = []
    for s in range(SUBLANES):
        entering.append(state)
        state = c_end[s:s + 1, :] * state + h_end[s:s + 1, :]
    hc_ref[0:1, :] = state
    entering = jnp.concatenate(entering, axis=0)

    h = (h_scr[...].reshape(seg, SUBLANES, C)
         + c_scr[...].reshape(seg, SUBLANES, C) * entering[None]).reshape(ts, C)
    outp = (h * gelu_g).astype(BF16)
    out_ref[...] = jnp.dot(unperm_ref[...], outp,
                           preferred_element_type=F32).astype(out_ref.dtype)


def _mixers_kernel(proj_ref, gate_ref, hg_ref,
                   cw_ref, cb_ref, wax_ref, ba_ref, bx_ref, lam_ref,
                   wup_ref, wdn_ref, wout_ref,
                   mix_ref, wup_bf_ref, wdn_bf_ref, wout_bf_ref,
                   s_ref, m_ref, tri_ref,
                   xtail_ref, hc_ref, perm_ref, unperm_ref, a_scr, u_scr, c_scr, h_scr,
                   *, heads, dh, rows, blocks, bd, slab_every):
    W = heads * dh
    q_ref, k_ref, v_ref, o_ref, xr_ref, gr_ref = (
        proj_ref.at[:, j * W:(j + 1) * W] for j in range(6))
    hm_ref = mix_ref.at[:, 0:W]
    hr_ref = mix_ref.at[:, W:2 * W]

    _mlstm_init(s_ref, m_ref, tri_ref, chunk=rows)
    _lru_init(xtail_ref, hc_ref, perm_ref, unperm_ref, ts=rows)

    step = pl.program_id(0) * pl.num_programs(1) + pl.program_id(1)
    for src, dst, every in zip((wup_ref, wdn_ref, wout_ref),
                               (wup_bf_ref, wdn_bf_ref, wout_bf_ref), slab_every):
        share = src.shape[0] // every
        part = pl.ds(pl.multiple_of((step % every) * share, share), share)
        dst[part, :] = src[part, :].astype(BF16)

    _mlstm_main(q_ref, k_ref, v_ref, o_ref, gate_ref, hg_ref, hm_ref,
                s_ref, m_ref, tri_ref, heads=heads, dh=dh, chunk=rows)
    _lru_main(xr_ref, gr_ref, cw_ref, cb_ref, wax_ref, ba_ref, bx_ref, lam_ref, hr_ref,
              xtail_ref, hc_ref, perm_ref, unperm_ref, a_scr, u_scr, c_scr, h_scr,
              ts=rows, blocks=blocks, bd=bd)


def _mixers(proj, gates, head_g, conv_w, conv_b, w_ax, ba, bx, lam, w_up, w_down, w_out,
            *, batch, seq, heads, dh, rows):
    T = batch * seq
    W = heads * dh
    blocks, bd, _ = w_ax.shape
    C = blocks * bd
    assert W == C and bd == LANES and rows % (SUBLANES * SUBLANES) == 0
    ns = seq // rows
    steps = batch * ns
    row_blk = lambda b, s: b * ns + s

    def slab(w, every):
        n = w.shape[0] // steps
        assert n * steps == w.shape[0] and n % 16 == 0 and steps % every == 0
        return pl.BlockSpec((n * every, w.shape[1]), lambda b, s: (row_blk(b, s) // every, 0))

    slab_every = (2, 2, 4)
    slabs = [slab(w, e) for w, e in zip((w_up, w_down, w_out), slab_every)]

    rows_of = lambda width: pl.BlockSpec((rows, width), lambda b, s: (row_blk(b, s), 0))
    vec = lambda n: pl.BlockSpec((n, C), lambda b, s: (0, 0))
    kern = functools.partial(_mixers_kernel, heads=heads, dh=dh, rows=rows, blocks=blocks, bd=bd,
                             slab_every=slab_every)
    return pl.pallas_call(
        kern,
        grid=(batch, ns),
        in_specs=[
            rows_of(proj.shape[1]),
            rows_of(LANES),
            pl.BlockSpec((heads, 1, dh), lambda b, s: (0, 0, 0)),
            vec(LRU_CONV), vec(1),
            pl.BlockSpec((blocks, bd, 2 * bd), lambda b, s: (0, 0, 0)),
            vec(1), vec(1), vec(1),
            *slabs,
        ],
        out_specs=[rows_of(W + C), *slabs],
        out_shape=[jax.ShapeDtypeStruct((T, W + C), BF16),
                   jax.ShapeDtypeStruct(w_up.shape, BF16),
                   jax.ShapeDtypeStruct(w_down.shape, BF16),
                   jax.ShapeDtypeStruct(w_out.shape, BF16)],
        scratch_shapes=[
            pltpu.VMEM((heads, dh, dh + LANES), F32),
            pltpu.VMEM((SUBLANES, LANES), F32),
            pltpu.VMEM((rows, rows), BF16),
            pltpu.VMEM((SUBLANES, C), F32),
            pltpu.VMEM((SUBLANES, C), F32),
            pltpu.VMEM((rows, rows), BF16),
            pltpu.VMEM((rows, rows), BF16),
        ] + [pltpu.VMEM((rows, C), F32)] * 4,
        compiler_params=pltpu.CompilerParams(
            dimension_semantics=("arbitrary", "arbitrary"),
            vmem_limit_bytes=VMEM_LIMIT),
        name="mixers",
    )(proj, gates, head_g, conv_w, conv_b, w_ax, ba, bx, lam, w_up, w_down, w_out)


def _outproj_kernel(x_ref, mix_ref, w_ref, g_ref, x1_ref, n2_ref):
    x1 = x_ref[...] + jnp.dot(mix_ref[...], w_ref[...], preferred_element_type=F32)
    x1_ref[...] = x1
    n2_ref[...] = _rmsnorm(x1, g_ref[...]).astype(n2_ref.dtype)


def _outproj(x, mix, w_out, g, *, tm):
    T, D = x.shape
    Dm = mix.shape[1]
    return pl.pallas_call(
        _outproj_kernel,
        grid=(T // tm,),
        in_specs=[
            pl.BlockSpec((tm, D), lambda m: (m, 0)),
            pl.BlockSpec((tm, Dm), lambda m: (m, 0)),
            pl.BlockSpec((Dm, D), lambda m: (0, 0)),
            pl.BlockSpec((1, D), lambda m: (0, 0)),
        ],
        out_specs=[
            pl.BlockSpec((tm, D), lambda m: (m, 0)),
            pl.BlockSpec((tm, D), lambda m: (m, 0)),
        ],
        out_shape=[
            jax.ShapeDtypeStruct((T, D), F32),
            jax.ShapeDtypeStruct((T, D), BF16),
        ],
        compiler_params=pltpu.CompilerParams(
            dimension_semantics=("arbitrary",),
            vmem_limit_bytes=VMEM_LIMIT),
        name="outproj",
    )(x, mix, w_out, g)


def _ffn_kernel(n2_ref, x1_ref, wg_ref, wu_ref, cw_ref, cb_ref, wd_ref, gf_ref, y_ref,
                tail_ref, *, tm, tiles_per_seq, final_norm):
    m = pl.program_id(0)
    f = pl.program_id(1)
    nf = pl.num_programs(1)

    def step(first):
        n2 = n2_ref[...]
        gate = jnp.dot(n2, wg_ref[...], preferred_element_type=F32)
        up = jnp.dot(n2, wu_ref[...], preferred_element_type=F32)

        prev = jnp.where(m % tiles_per_seq == 0, 0.0, tail_ref[f])
        tail_ref[f] = gate[tm - SUBLANES:tm, :]

        row8 = lax.broadcasted_iota(jnp.int32, prev.shape, 0)
        cw = cw_ref[...]
        conv = cb_ref[...] + cw[FFN_CONV - 1:FFN_CONV, :] * gate
        for j in range(1, FFN_CONV):
            r = pltpu.roll(gate, j, axis=0)
            top = jnp.where(row8 < j, pltpu.roll(prev, j, axis=0), r[0:SUBLANES, :])
            gs = jnp.concatenate([top, r[SUBLANES:, :]], axis=0)
            conv = conv + cw[FFN_CONV - 1 - j:FFN_CONV - j, :] * gs
        hid_b = ((conv * _sigmoid(conv)) * up).astype(BF16)
        tf = hid_b.shape[1]
        for c in range(y_ref.shape[1] // tf):
            cols = slice(c * tf, (c + 1) * tf)
            acc = x1_ref[:, cols] if first else y_ref[:, cols]
            y_ref[:, cols] = acc + jnp.dot(hid_b, wd_ref[:, cols], preferred_element_type=F32)

    @pl.when(f == 0)
    def _():
        step(True)

    @pl.when(f > 0)
    def _():
        step(False)

    if final_norm:
        @pl.when(f == nf - 1)
        def _():
            y_ref[...] = _rmsnorm(y_ref[...], gf_ref[...])


def _ffn(n2, x1, w_up, conv_w, conv_b, w_down, gf, *, seq, tm, tf, final_norm):
    T, D = n2.shape
    F = w_down.shape[0]
    nf = F // tf
    kern = functools.partial(_ffn_kernel, tm=tm, tiles_per_seq=seq // tm, final_norm=final_norm)
    return pl.pallas_call(
        kern,
        grid=(T // tm, nf),
        in_specs=[
            pl.BlockSpec((tm, D), lambda m, f: (m, 0)),
            pl.BlockSpec((tm, D), lambda m, f: (m, 0)),
            pl.BlockSpec((D, tf), lambda m, f: (0, f)),
            pl.BlockSpec((D, tf), lambda m, f: (0, nf + f)),
            pl.BlockSpec((FFN_CONV, tf), lambda m, f: (0, f)),
            pl.BlockSpec((1, tf), lambda m, f: (0, f)),
            pl.BlockSpec((tf, D), lambda m, f: (f, 0)),
            pl.BlockSpec((1, D), lambda m, f: (0, 0)),
        ],
        out_specs=pl.BlockSpec((tm, D), lambda m, f: (m, 0)),
        out_shape=jax.ShapeDtypeStruct((T, D), F32),
        scratch_shapes=[pltpu.VMEM((nf, SUBLANES, tf), F32)],
        compiler_params=pltpu.CompilerParams(
            dimension_semantics=("arbitrary", "arbitrary"),
            vmem_limit_bytes=VMEM_LIMIT_FFN),
        name="convffn",
    )(n2, x1, w_up, w_up, conv_w, conv_b, w_down, gf)


def kernel(x, norm_mix_g, w_in, b_gate_m, mlstm_norm_g, lru_conv_w, lru_conv_b, lru_wa, lru_ba,
           lru_wx, lru_bx, lru_lambda, w_out, norm_ffn_g, w_up, ffn_conv_w, ffn_conv_b, w_down,
           norm_final_g):
    B, S, D = x.shape
    T = B * S
    depth = w_in.shape[0]
    H = MLSTM_HEADS
    dh = mlstm_norm_g.shape[-1]
    d_m = H * dh
    d_r = lru_conv_w.shape[-1]
    assert d_m == d_r, "column-block addressing assumes equal head-group widths"
    n_gate = 2 * H
    gate_lo = 4 * d_m

    xf = x.reshape(T, D)
    for l in range(depth):
        b_if = jnp.pad(b_gate_m[l].astype(F32), (0, LANES - n_gate)).reshape(1, LANES)
        proj, gates = _inproj(xf, norm_mix_g[l].reshape(1, D), jnp.swapaxes(w_in[l], 0, 1), b_if,
                              gate_lo=gate_lo, n_gate=n_gate, tm=1024, tn=1024, group=3)

        w_ax = (0.5 * jnp.concatenate([lru_wa[l], lru_wx[l]], axis=-1)).astype(BF16)
        mix, w_up_bf, w_down_bf, w_out_bf = _mixers(
            proj, gates, mlstm_norm_g[l].reshape(H, 1, dh),
            lru_conv_w[l], lru_conv_b[l].reshape(1, d_r), w_ax,
            lru_ba[l].reshape(1, d_r), lru_bx[l].reshape(1, d_r), lru_lambda[l].reshape(1, d_r),
            w_up[l], w_down[l], w_out[l], batch=B, seq=S, heads=H, dh=dh, rows=MIXER_ROWS)

        x1, n2 = _outproj(xf, mix, w_out_bf, norm_ffn_g[l].reshape(1, D), tm=512)

        xf = _ffn(n2, x1, w_up_bf, ffn_conv_w[l], ffn_conv_b[l].reshape(1, -1),
                  w_down_bf, norm_final_g.reshape(1, D), seq=S, tm=1024, tf=512,
                  final_norm=(l == depth - 1))
    return xf.reshape(B, S, D)
```

```python
import functools
import math

import jax
import jax.numpy as jnp
from jax import lax
from jax.experimental import pallas as pl
from jax.experimental.pallas import tpu as pltpu

F32 = jnp.float32
BF16 = jnp.bfloat16
EPS = 1e-6
LANES = 128
SUBLANES = 8
BF16_SUBLANES = 16
SUBLANE_BITS = SUBLANES.bit_length() - 1
VMEM_LIMIT = 56 * 1024 * 1024
VMEM_LIMIT_FFN = 61 * 1024 * 1024

MLSTM_HEADS = 4
LRU_BLOCKS = 8
LRU_CONV = 4
LRU_C = 8.0
FFN_CONV = 3
MIXER_ROWS = 256


def _sigmoid(x):
    return 0.5 * jnp.tanh(0.5 * x) + 0.5


def _log_sigmoid(x):
    return jnp.minimum(x, 0.0) - jnp.log1p(jnp.exp(-jnp.abs(x)))


def _softplus(x):
    return jnp.maximum(x, 0.0) + jnp.log1p(jnp.exp(-jnp.abs(x)))


def _gelu_tanh(x):
    c = math.sqrt(2.0 / math.pi)
    t = jnp.tanh(x * (c + (c * 0.044715) * (x * x)))
    hx = 0.5 * x
    return hx + hx * t


def _rmsnorm(x, g):
    return x * lax.rsqrt(jnp.mean(x * x, axis=-1, keepdims=True) + EPS) * g


_NT = (((1,), (1,)), ((), ()))


def _inproj_kernel(x_ref, g_ref, w_ref, nxt_ref, wif_ref, bif_ref, proj_ref, gate_ref, n1_ref,
                   *, first_shifted, shift):
    n = pl.program_id(1)

    @pl.when(n == 0)
    def _():
        n1 = _rmsnorm(x_ref[...], g_ref[...]).astype(BF16)
        n1_ref[...] = n1
        gate_ref[...] = lax.dot_general(n1, wif_ref[...].astype(BF16), _NT,
                                        preferred_element_type=F32) + bif_ref[...]

    tn = w_ref.shape[0]
    group = proj_ref.shape[1] // tn

    def project(w_bf, slot):
        proj_ref[:, slot * tn:(slot + 1) * tn] = lax.dot_general(
            n1_ref[...], w_bf, _NT, preferred_element_type=F32).astype(proj_ref.dtype)

    for slot in range(group):
        @pl.when((n < first_shifted) & (n % group == slot))
        def _(slot=slot):
            project(w_ref[...].astype(BF16), slot)

        @pl.when((n >= first_shifted) & (n % group == slot))
        def _(slot=slot):
            project(jnp.concatenate([w_ref[shift:, :], nxt_ref[...]], axis=0).astype(BF16), slot)


def _inproj(x, g, wt, b_if, *, gate_lo, n_gate, tm, tn, group):
    T, D = x.shape
    P = wt.shape[0]
    N = P - n_gate
    assert gate_lo % tn == 0 and N % (group * tn) == 0 and n_gate == SUBLANES and gate_lo % LANES == 0
    first_shifted = gate_lo // tn
    per = tn // n_gate
    kern = functools.partial(_inproj_kernel, first_shifted=first_shifted, shift=n_gate)
    return pl.pallas_call(
        kern,
        grid=(T // tm, N // tn),
        in_specs=[
            pl.BlockSpec((tm, D), lambda m, n: (jnp.minimum(m + jnp.minimum(n, 1), T // tm - 1), 0)),
            pl.BlockSpec((1, D), lambda m, n: (0, 0)),
            pl.BlockSpec((tn, D), lambda m, n: (n, 0)),
            pl.BlockSpec((n_gate, D), lambda m, n: (jnp.maximum(n, first_shifted) * per + per, 0)),
            pl.BlockSpec((LANES, D), lambda m, n: (gate_lo // LANES, 0)),
            pl.BlockSpec((1, LANES), lambda m, n: (0, 0)),
        ],
        out_specs=[
            pl.BlockSpec((tm, group * tn), lambda m, n: (m, n // group)),
            pl.BlockSpec((tm, LANES), lambda m, n: (m, 0)),
        ],
        out_shape=[
            jax.ShapeDtypeStruct((T, N), BF16),
            jax.ShapeDtypeStruct((T, LANES), F32),
        ],
        scratch_shapes=[pltpu.VMEM((tm, D), BF16)],
        compiler_params=pltpu.CompilerParams(
            dimension_semantics=("arbitrary", "arbitrary"),
            vmem_limit_bytes=VMEM_LIMIT),
        name="inproj",
    )(x, g, wt, wt, wt, b_if)


def _causal_mask(L):
    return (lax.broadcasted_iota(jnp.int32, (L, L), 0)
            >= lax.broadcasted_iota(jnp.int32, (L, L), 1))


def _mlstm_init(s_ref, m_ref, tri_ref, *, chunk):
    @pl.when(pl.program_id(1) == 0)
    def _():
        s_ref[...] = jnp.zeros_like(s_ref)
        m_ref[...] = jnp.zeros_like(m_ref)

    @pl.when((pl.program_id(0) == 0) & (pl.program_id(1) == 0))
    def _():
        tri_ref[...] = jnp.where(_causal_mask(chunk), 1.0, 0.0).astype(BF16)


def _mlstm_main(q_ref, k_ref, v_ref, o_ref, gate_ref, hg_ref, out_ref,
                s_ref, m_ref, tri_ref, *, heads, dh, chunk):
    L = chunk
    scale = dh ** -0.5
    causal = _causal_mask(L)

    def split(x, parts):
        out = []
        for _ in range(parts - 1):
            p = x.astype(BF16)
            out.append(p)
            x = x - p.astype(F32)
        out.append(x.astype(BF16))
        return out

    def rep(x, width):
        return jnp.concatenate([x] * (width // LANES), axis=1)

    g = gate_ref[...]
    tri = tri_ref[...]

    g8 = g.T[0:SUBLANES, :]
    g8r = pltpu.roll(g8, heads, axis=0)
    sub = lax.broadcasted_iota(jnp.int32, g8.shape, 0)
    ig = jnp.where(sub < heads, g8, g8r)
    fg = jnp.where(sub < heads, g8r, g8)
    logf = jnp.concatenate(split(_log_sigmoid(fg), 2), axis=0)
    br = lax.dot_general(logf, tri, _NT, preferred_element_type=F32)
    b = br[0:SUBLANES, :] + br[SUBLANES:, :]

    logf_pad = jnp.concatenate([logf, jnp.zeros((LANES - 2 * SUBLANES, L), BF16)], axis=0)
    b_nat = lax.dot_general(tri, logf_pad, _NT, preferred_element_type=F32)
    a_row = ig - b
    b_tot = b[:, L - 1:L]
    m_prev = m_ref[:, 0:1]
    m_new = jnp.maximum(b_tot + m_prev, b_tot + jnp.max(a_row, axis=1, keepdims=True))
    decay = jnp.exp(b_tot + m_prev - m_new)
    end_shift = b_tot - m_new
    m_ref[...] = jnp.broadcast_to(m_new, m_ref.shape)
    ones_blk = jnp.ones((L, LANES), BF16)

    for h in range(heads):
        sl = slice(h * dh, (h + 1) * dh)
        q = q_ref[:, sl] * scale
        k = k_ref[:, sl]
        v = v_ref[:, sl]
        ig_r = jnp.broadcast_to(g[:, h:h + 1], (L, LANES))
        b_r = jnp.broadcast_to(b_nat[:, h:h + 1] + b_nat[:, SUBLANES + h:SUBLANES + h + 1],
                               (L, LANES))
        a_h = a_row[h:h + 1, :]
        mp = m_prev[h:h + 1, :]

        s = lax.dot_general(q, k, _NT, preferred_element_type=F32)
        cm = jnp.max(jnp.where(causal, a_h, -jnp.inf), axis=-1, keepdims=True)
        mm_r = jnp.maximum(jnp.broadcast_to(cm, (L, LANES)), mp)
        w = s * jnp.exp(jnp.where(causal, a_h - rep(mm_r, L), -jnp.inf))
        einter_r = jnp.exp(mp - mm_r)
        floor_r = jnp.exp(-(b_r + mm_r))
        eend_r = jnp.exp(end_shift[h:h + 1, :] + ig_r - b_r)

        state = s_ref[h]
        intra = jnp.dot(w.astype(BF16), jnp.concatenate([v, ones_blk], axis=1),
                        preferred_element_type=F32)
        inter = jnp.dot(q, state.astype(BF16), preferred_element_type=F32)
        tot = intra + rep(einter_r, dh + LANES) * inter
        num = tot[:, :dh]
        den_r = tot[:, dh:]
        inv_r = 1.0 / jnp.maximum(jnp.abs(den_r), floor_r)
        ssq_r = jnp.broadcast_to(jnp.sum(num * num, axis=-1, keepdims=True), (L, LANES))
        rs_r = inv_r * lax.rsqrt(ssq_r * (inv_r * inv_r) * (1.0 / dh) + EPS)
        half_g = 0.5 * hg_ref[h]
        gain = half_g + half_g * jnp.tanh(0.5 * o_ref[:, sl].astype(F32))
        out_ref[:, sl] = ((num * rep(rs_r, dh)) * gain).astype(out_ref.dtype)

        ev = jnp.concatenate(
            [(rep(eend_r, dh) * v.astype(F32)).astype(BF16), eend_r.astype(BF16)], axis=1)
        loc = lax.dot_general(k, ev, (((0,), (0,)), ((), ())), preferred_element_type=F32)
        s_ref[h] = decay[h:h + 1, :] * state + loc


def _lru_init(xtail_ref, hc_ref, perm_ref, unperm_ref, *, ts):
    seg = ts // SUBLANES

    @pl.when(pl.program_id(1) == 0)
    def _():
        xtail_ref[...] = jnp.zeros_like(xtail_ref)
        hc_ref[...] = jnp.zeros_like(hc_ref)

    @pl.when((pl.program_id(0) == 0) & (pl.program_id(1) == 0))
    def _():
        ri = lax.broadcasted_iota(jnp.int32, (ts, ts), 0)
        ci = lax.broadcasted_iota(jnp.int32, (ts, ts), 1)
        t_of_row = (ri & (SUBLANES - 1)) * seg + (ri >> SUBLANE_BITS)
        t_of_col = (ci & (SUBLANES - 1)) * seg + (ci >> SUBLANE_BITS)
        perm_ref[...] = jnp.where(ci == t_of_row, 1.0, 0.0).astype(BF16)
        unperm_ref[...] = jnp.where(ri == t_of_col, 1.0, 0.0).astype(BF16)


def _lru_main(xr_ref, gr_ref, cw_ref, cb_ref, wax_ref, ba_ref, bx_ref, lam_ref, out_ref,
              xtail_ref, hc_ref, perm_ref, unperm_ref, a_scr, u_scr, c_scr, h_scr,
              *, ts, blocks, bd):
    C = blocks * bd
    seg = ts // SUBLANES
    halo = LRU_CONV - 1

    xb = xr_ref[...]
    xp = jnp.dot(perm_ref[...], xb, preferred_element_type=F32)
    gp = jnp.dot(perm_ref[...], gr_ref[...], preferred_element_type=F32)
    gelu_g = _gelu_tanh(gp)
    prev = xtail_ref[...]
    xtail_ref[...] = xb[ts - SUBLANES:ts, :].astype(F32)
    row8 = lax.broadcasted_iota(jnp.int32, (SUBLANES, C), 0)

    wrapped = []
    for k in range(1, halo + 1):
        r0 = (seg - k) * SUBLANES
        up = pltpu.roll(xp[r0:r0 + SUBLANES, :], 1, axis=0)
        wrapped.append(jnp.where(row8 == 0, prev[SUBLANES - k:SUBLANES - k + 1, :], up))

    cw = cw_ref[...]
    xc = cb_ref[...] + cw[LRU_CONV - 1:LRU_CONV, :] * xp
    for j in range(1, LRU_CONV):
        xs = jnp.concatenate([wrapped[k - 1] for k in range(j, 0, -1)]
                             + [xp[0:ts - j * SUBLANES, :]], axis=0)
        xc = xc + cw[LRU_CONV - 1 - j:LRU_CONV - j, :] * xs

    xcb = xc.astype(BF16)
    half_xc = 0.5 * xc
    k = (-0.5 * LRU_C * math.log2(math.e)) * _softplus(-lam_ref[...])
    half_ba = 0.5 * ba_ref[...]
    half_bx = 0.5 * bx_ref[...]
    for n in range(blocks):
        sl = slice(n * bd, (n + 1) * bd)
        ax = jnp.dot(xcb[:, sl], wax_ref[n], preferred_element_type=F32)
        t_r = jnp.tanh(ax[:, :bd] + half_ba[:, sl])
        t_i = jnp.tanh(ax[:, bd:] + half_bx[:, sl])
        a = jnp.exp2(k[:, sl] + k[:, sl] * t_r)
        ixc = half_xc[:, sl] + half_xc[:, sl] * t_i
        y = 1.0 - a * a
        a_scr[:, sl] = a
        u_scr[:, sl] = jnp.where(y > 0.0, y * lax.rsqrt(y), 0.0) * ixc

    def body(v, carry):
        h, c = carry
        rows = pl.ds(pl.multiple_of(v * SUBLANES, SUBLANES), SUBLANES)
        av = a_scr[rows, :]
        h = av * h + u_scr[rows, :]
        c = av * c
        h_scr[rows, :] = h
        c_scr[rows, :] = c
        return h, c

    init = (jnp.zeros((SUBLANES, C), F32), jnp.ones((SUBLANES, C), F32))
    h_end, c_end = lax.fori_loop(0, seg, body, init, unroll=4)

    state = hc_ref[0:1, :]
    entering = []
    for s in range(SUBLANES):
        entering.append(state)
        state = c_end[s:s + 1, :] * state + h_end[s:s + 1, :]
    hc_ref[0:1, :] = state
    entering = jnp.concatenate(entering, axis=0)

    h = (h_scr[...].reshape(seg, SUBLANES, C)
         + c_scr[...].reshape(seg, SUBLANES, C) * entering[None]).reshape(ts, C)
    outp = (h * gelu_g).astype(BF16)
    out_ref[...] = jnp.dot(unperm_ref[...], outp,
                           preferred_element_type=F32).astype(out_ref.dtype)


def _mixers_kernel(proj_ref, gate_ref, hg_ref,
                   cw_ref, cb_ref, wax_ref, ba_ref, bx_ref, lam_ref,
                   wup_ref, wdn_ref, wout_ref,
                   mix_ref, wup_bf_ref, wdn_bf_ref, wout_bf_ref,
                   s_ref, m_ref, tri_ref,
                   xtail_ref, hc_ref, perm_ref, unperm_ref, a_scr, u_scr, c_scr, h_scr,
                   *, heads, dh, rows, blocks, bd):
    W = heads * dh
    q_ref, k_ref, v_ref, o_ref, xr_ref, gr_ref = (
        proj_ref.at[:, j * W:(j + 1) * W] for j in range(6))
    hm_ref = mix_ref.at[:, 0:W]
    hr_ref = mix_ref.at[:, W:2 * W]

    _mlstm_init(s_ref, m_ref, tri_ref, chunk=rows)
    _lru_init(xtail_ref, hc_ref, perm_ref, unperm_ref, ts=rows)

    wup_bf_ref[...] = wup_ref[...].astype(BF16)
    wdn_bf_ref[...] = wdn_ref[...].astype(BF16)
    wout_bf_ref[...] = wout_ref[...].astype(BF16)

    _mlstm_main(q_ref, k_ref, v_ref, o_ref, gate_ref, hg_ref, hm_ref,
                s_ref, m_ref, tri_ref, heads=heads, dh=dh, chunk=rows)
    _lru_main(xr_ref, gr_ref, cw_ref, cb_ref, wax_ref, ba_ref, bx_ref, lam_ref, hr_ref,
              xtail_ref, hc_ref, perm_ref, unperm_ref, a_scr, u_scr, c_scr, h_scr,
              ts=rows, blocks=blocks, bd=bd)


def _mixers(proj, gates, head_g, conv_w, conv_b, w_ax, ba, bx, lam, w_up, w_down, w_out,
            *, batch, seq, heads, dh, rows):
    T = batch * seq
    W = heads * dh
    blocks, bd, _ = w_ax.shape
    C = blocks * bd
    assert W == C and bd == LANES and rows % (SUBLANES * SUBLANES) == 0
    ns = seq // rows
    steps = batch * ns
    row_blk = lambda b, s: b * ns + s

    def slab(w):
        n = w.shape[0] // steps
        assert n * steps == w.shape[0] and n % BF16_SUBLANES == 0
        return pl.BlockSpec((n, w.shape[1]), lambda b, s: (row_blk(b, s), 0))

    rows_of = lambda width: pl.BlockSpec((rows, width), lambda b, s: (row_blk(b, s), 0))
    vec = lambda n: pl.BlockSpec((n, C), lambda b, s: (0, 0))
    kern = functools.partial(_mixers_kernel, heads=heads, dh=dh, rows=rows, blocks=blocks, bd=bd)
    return pl.pallas_call(
        kern,
        grid=(batch, ns),
        in_specs=[
            rows_of(proj.shape[1]),
            rows_of(LANES),
            pl.BlockSpec((heads, 1, dh), lambda b, s: (0, 0, 0)),
            vec(LRU_CONV), vec(1),
            pl.BlockSpec((blocks, bd, 2 * bd), lambda b, s: (0, 0, 0)),
            vec(1), vec(1), vec(1),
            slab(w_up), slab(w_down), slab(w_out),
        ],
        out_specs=[rows_of(W + C), slab(w_up), slab(w_down), slab(w_out)],
        out_shape=[jax.ShapeDtypeStruct((T, W + C), BF16),
                   jax.ShapeDtypeStruct(w_up.shape, BF16),
                   jax.ShapeDtypeStruct(w_down.shape, BF16),
                   jax.ShapeDtypeStruct(w_out.shape, BF16)],
        scratch_shapes=[
            pltpu.VMEM((heads, dh, dh + LANES), F32),
            pltpu.VMEM((SUBLANES, LANES), F32),
            pltpu.VMEM((rows, rows), BF16),
            pltpu.VMEM((SUBLANES, C), F32),
            pltpu.VMEM((SUBLANES, C), F32),
            pltpu.VMEM((rows, rows), BF16),
            pltpu.VMEM((rows, rows), BF16),
        ] + [pltpu.VMEM((rows, C), F32)] * 4,
        compiler_params=pltpu.CompilerParams(
            dimension_semantics=("arbitrary", "arbitrary"),
            vmem_limit_bytes=VMEM_LIMIT),
        name="mixers",
    )(proj, gates, head_g, conv_w, conv_b, w_ax, ba, bx, lam, w_up, w_down, w_out)


def _outproj_kernel(x_ref, mix_ref, w_ref, g_ref, x1_ref, n2_ref):
    x1 = x_ref[...] + jnp.dot(mix_ref[...], w_ref[...], preferred_element_type=F32)
    x1_ref[...] = x1
    n2_ref[...] = _rmsnorm(x1, g_ref[...]).astype(n2_ref.dtype)


def _outproj(x, mix, w_out, g, *, tm):
    T, D = x.shape
    Dm = mix.shape[1]
    return pl.pallas_call(
        _outproj_kernel,
        grid=(T // tm,),
        in_specs=[
            pl.BlockSpec((tm, D), lambda m: (m, 0)),
            pl.BlockSpec((tm, Dm), lambda m: (m, 0)),
            pl.BlockSpec((Dm, D), lambda m: (0, 0)),
            pl.BlockSpec((1, D), lambda m: (0, 0)),
        ],
        out_specs=[
            pl.BlockSpec((tm, D), lambda m: (m, 0)),
            pl.BlockSpec((tm, D), lambda m: (m, 0)),
        ],
        out_shape=[
            jax.ShapeDtypeStruct((T, D), F32),
            jax.ShapeDtypeStruct((T, D), BF16),
        ],
        compiler_params=pltpu.CompilerParams(
            dimension_semantics=("arbitrary",),
            vmem_limit_bytes=VMEM_LIMIT),
        name="outproj",
    )(x, mix, w_out, g)


def _ffn_kernel(n2_ref, x1_ref, wg_ref, wu_ref, cw_ref, cb_ref, wd_ref, gf_ref, y_ref,
                tail_ref, *, tm, tiles_per_seq, final_norm):
    m = pl.program_id(0)
    f = pl.program_id(1)
    nf = pl.num_programs(1)

    def step(first):
        n2 = n2_ref[...]
        gate = jnp.dot(n2, wg_ref[...], preferred_element_type=F32)
        up = jnp.dot(n2, wu_ref[...], preferred_element_type=F32)

        prev = jnp.where(m % tiles_per_seq == 0, 0.0, tail_ref[f])
        tail_ref[f] = gate[tm - SUBLANES:tm, :]

        row8 = lax.broadcasted_iota(jnp.int32, prev.shape, 0)
        cw = cw_ref[...]
        conv = cb_ref[...] + cw[FFN_CONV - 1:FFN_CONV, :] * gate
        for j in range(1, FFN_CONV):
            r = pltpu.roll(gate, j, axis=0)
            top = jnp.where(row8 < j, pltpu.roll(prev, j, axis=0), r[0:SUBLANES, :])
            gs = jnp.concatenate([top, r[SUBLANES:, :]], axis=0)
            conv = conv + cw[FFN_CONV - 1 - j:FFN_CONV - j, :] * gs
        hid_b = ((conv * _sigmoid(conv)) * up).astype(BF16)
        tf = hid_b.shape[1]
        for c in range(y_ref.shape[1] // tf):
            cols = slice(c * tf, (c + 1) * tf)
            acc = x1_ref[:, cols] if first else y_ref[:, cols]
            y_ref[:, cols] = acc + jnp.dot(hid_b, wd_ref[:, cols], preferred_element_type=F32)

    @pl.when(f == 0)
    def _():
        step(True)

    @pl.when(f > 0)
    def _():
        step(False)

    if final_norm:
        @pl.when(f == nf - 1)
        def _():
            y_ref[...] = _rmsnorm(y_ref[...], gf_ref[...])


def _ffn(n2, x1, w_up, conv_w, conv_b, w_down, gf, *, seq, tm, tf, final_norm):
    T, D = n2.shape
    F = w_down.shape[0]
    nf = F // tf
    kern = functools.partial(_ffn_kernel, tm=tm, tiles_per_seq=seq // tm, final_norm=final_norm)
    return pl.pallas_call(
        kern,
        grid=(T // tm, nf),
        in_specs=[
            pl.BlockSpec((tm, D), lambda m, f: (m, 0)),
            pl.BlockSpec((tm, D), lambda m, f: (jnp.minimum(m + jnp.minimum(f, 1), T // tm - 1), 0)),
            pl.BlockSpec((D, tf), lambda m, f: (0, f)),
            pl.BlockSpec((D, tf), lambda m, f: (0, nf + f)),
            pl.BlockSpec((FFN_CONV, tf), lambda m, f: (0, f)),
            pl.BlockSpec((1, tf), lambda m, f: (0, f)),
            pl.BlockSpec((tf, D), lambda m, f: (f, 0)),
            pl.BlockSpec((1, D), lambda m, f: (0, 0)),
        ],
        out_specs=pl.BlockSpec((tm, D), lambda m, f: (m, 0)),
        out_shape=jax.ShapeDtypeStruct((T, D), F32),
        scratch_shapes=[pltpu.VMEM((nf, SUBLANES, tf), F32)],
        compiler_params=pltpu.CompilerParams(
            dimension_semantics=("arbitrary", "arbitrary"),
            vmem_limit_bytes=VMEM_LIMIT_FFN),
        name="convffn",
    )(n2, x1, w_up, w_up, conv_w, conv_b, w_down, gf)


def kernel(x, norm_mix_g, w_in, b_gate_m, mlstm_norm_g, lru_conv_w, lru_conv_b, lru_wa, lru_ba,
           lru_wx, lru_bx, lru_lambda, w_out, norm_ffn_g, w_up, ffn_conv_w, ffn_conv_b, w_down,
           norm_final_g):
    B, S, D = x.shape
    T = B * S
    depth = w_in.shape[0]
    H = MLSTM_HEADS
    dh = mlstm_norm_g.shape[-1]
    d_m = H * dh
    d_r = lru_conv_w.shape[-1]
    assert d_m == d_r, "column-block addressing assumes equal head-group widths"
    n_gate = 2 * H
    gate_lo = 4 * d_m

    xf = x.reshape(T, D)
    for l in range(depth):
        b_if = jnp.pad(b_gate_m[l].astype(F32), (0, LANES - n_gate)).reshape(1, LANES)
        proj, gates = _inproj(xf, norm_mix_g[l].reshape(1, D), jnp.swapaxes(w_in[l], 0, 1), b_if,
                              gate_lo=gate_lo, n_gate=n_gate, tm=1024, tn=1024, group=2)

        w_ax = (0.5 * jnp.concatenate([lru_wa[l], lru_wx[l]], axis=-1)).astype(BF16)
        mix, w_up_bf, w_down_bf, w_out_bf = _mixers(
            proj, gates, mlstm_norm_g[l].reshape(H, 1, dh),
            lru_conv_w[l], lru_conv_b[l].reshape(1, d_r), w_ax,
            lru_ba[l].reshape(1, d_r), lru_bx[l].reshape(1, d_r), lru_lambda[l].reshape(1, d_r),
            w_up[l], w_down[l], w_out[l], batch=B, seq=S, heads=H, dh=dh, rows=MIXER_ROWS)

        x1, n2 = _outproj(xf, mix, w_out_bf, norm_ffn_g[l].reshape(1, D), tm=512)

        xf = _ffn(n2, x1, w_up_bf, ffn_conv_w[l], ffn_conv_b[l].reshape(1, -1),
                  w_down_bf, norm_final_g.reshape(1, D), seq=S, tm=1024, tf=512,
                  final_norm=(l == depth - 1))
    return xf.reshape(B, S, D)
```

```python
import functools
import math

import jax
import jax.numpy as jnp
from jax import lax
from jax.experimental import pallas as pl
from jax.experimental.pallas import tpu as pltpu

F32 = jnp.float32
BF16 = jnp.bfloat16
EPS = 1e-6
LANES = 128
SUBLANES = 8
BF16_SUBLANES = 16
SUBLANE_BITS = SUBLANES.bit_length() - 1
VMEM_LIMIT = 56 * 1024 * 1024
VMEM_LIMIT_FFN = 61 * 1024 * 1024

MLSTM_HEADS = 4
LRU_BLOCKS = 8
LRU_CONV = 4
LRU_C = 8.0
FFN_CONV = 3
MIXER_ROWS = 256


def _sigmoid(x):
    return 0.5 * jnp.tanh(0.5 * x) + 0.5


def _log_sigmoid(x):
    return jnp.minimum(x, 0.0) - jnp.log1p(jnp.exp(-jnp.abs(x)))


def _softplus(x):
    return jnp.maximum(x, 0.0) + jnp.log1p(jnp.exp(-jnp.abs(x)))


def _gelu_tanh(x):
    c = math.sqrt(2.0 / math.pi)
    t = jnp.tanh(x * (c + (c * 0.044715) * (x * x)))
    hx = 0.5 * x
    return hx + hx * t


def _rmsnorm(x, g):
    return x * lax.rsqrt(jnp.mean(x * x, axis=-1, keepdims=True) + EPS) * g


_NT = (((1,), (1,)), ((), ()))


def _inproj_kernel(x_ref, g_ref, w_ref, nxt_ref, wif_ref, bif_ref, proj_ref, gate_ref, n1_ref,
                   *, first_shifted, shift):
    n = pl.program_id(1)

    @pl.when(n == 0)
    def _():
        n1 = _rmsnorm(x_ref[...], g_ref[...]).astype(BF16)
        n1_ref[...] = n1
        gate_ref[...] = lax.dot_general(n1, wif_ref[...].astype(BF16), _NT,
                                        preferred_element_type=F32) + bif_ref[...]

    tn = w_ref.shape[0]
    group = proj_ref.shape[1] // tn

    def project(w_bf, slot):
        proj_ref[:, slot * tn:(slot + 1) * tn] = lax.dot_general(
            n1_ref[...], w_bf, _NT, preferred_element_type=F32).astype(proj_ref.dtype)

    for slot in range(group):
        @pl.when((n < first_shifted) & (n % group == slot))
        def _(slot=slot):
            project(w_ref[...].astype(BF16), slot)

        @pl.when((n >= first_shifted) & (n % group == slot))
        def _(slot=slot):
            project(jnp.concatenate([w_ref[shift:, :], nxt_ref[...]], axis=0).astype(BF16), slot)


def _inproj(x, g, wt, b_if, *, gate_lo, n_gate, tm, tn, group):
    T, D = x.shape
    P = wt.shape[0]
    N = P - n_gate
    assert gate_lo % tn == 0 and N % (group * tn) == 0 and n_gate == SUBLANES and gate_lo % LANES == 0
    first_shifted = gate_lo // tn
    per = tn // n_gate
    kern = functools.partial(_inproj_kernel, first_shifted=first_shifted, shift=n_gate)
    return pl.pallas_call(
        kern,
        grid=(T // tm, N // tn),
        in_specs=[
            pl.BlockSpec((tm, D), lambda m, n: (jnp.minimum(m + jnp.minimum(n, 1), T // tm - 1), 0)),
            pl.BlockSpec((1, D), lambda m, n: (0, 0)),
            pl.BlockSpec((tn, D), lambda m, n: (n, 0)),
            pl.BlockSpec((n_gate, D), lambda m, n: (jnp.maximum(n, first_shifted) * per + per, 0)),
            pl.BlockSpec((LANES, D), lambda m, n: (gate_lo // LANES, 0)),
            pl.BlockSpec((1, LANES), lambda m, n: (0, 0)),
        ],
        out_specs=[
            pl.BlockSpec((tm, group * tn), lambda m, n: (m, n // group)),
            pl.BlockSpec((tm, LANES), lambda m, n: (m, 0)),
        ],
        out_shape=[
            jax.ShapeDtypeStruct((T, N), BF16),
            jax.ShapeDtypeStruct((T, LANES), F32),
        ],
        scratch_shapes=[pltpu.VMEM((tm, D), BF16)],
        compiler_params=pltpu.CompilerParams(
            dimension_semantics=("arbitrary", "arbitrary"),
            vmem_limit_bytes=VMEM_LIMIT),
        name="inproj",
    )(x, g, wt, wt, wt, b_if)


def _causal_mask(L):
    return (lax.broadcasted_iota(jnp.int32, (L, L), 0)
            >= lax.broadcasted_iota(jnp.int32, (L, L), 1))


def _mlstm_init(s_ref, m_ref, tri_ref, *, chunk):
    @pl.when(pl.program_id(1) == 0)
    def _():
        s_ref[...] = jnp.zeros_like(s_ref)
        m_ref[...] = jnp.zeros_like(m_ref)

    @pl.when((pl.program_id(0) == 0) & (pl.program_id(1) == 0))
    def _():
        tri_ref[...] = jnp.where(_causal_mask(chunk), 1.0, 0.0).astype(BF16)


def _mlstm_main(q_ref, k_ref, v_ref, o_ref, gate_ref, hg_ref, out_ref,
                s_ref, m_ref, tri_ref, *, heads, dh, chunk):
    L = chunk
    scale = dh ** -0.5
    causal = _causal_mask(L)

    def split(x, parts):
        out = []
        for _ in range(parts - 1):
            p = x.astype(BF16)
            out.append(p)
            x = x - p.astype(F32)
        out.append(x.astype(BF16))
        return out

    def rep(x, width):
        return jnp.concatenate([x] * (width // LANES), axis=1)

    g = gate_ref[...]
    tri = tri_ref[...]

    g8 = g.T[0:SUBLANES, :]
    g8r = pltpu.roll(g8, heads, axis=0)
    sub = lax.broadcasted_iota(jnp.int32, g8.shape, 0)
    ig = jnp.where(sub < heads, g8, g8r)
    fg = jnp.where(sub < heads, g8r, g8)
    logf = jnp.concatenate(split(_log_sigmoid(fg), 2), axis=0)
    br = lax.dot_general(logf, tri, _NT, preferred_element_type=F32)
    b = br[0:SUBLANES, :] + br[SUBLANES:, :]

    logf_pad = jnp.concatenate([logf, jnp.zeros((LANES - 2 * SUBLANES, L), BF16)], axis=0)
    b_nat = lax.dot_general(tri, logf_pad, _NT, preferred_element_type=F32)
    a_row = ig - b
    b_tot = b[:, L - 1:L]
    m_prev = m_ref[:, 0:1]
    m_new = jnp.maximum(b_tot + m_prev, b_tot + jnp.max(a_row, axis=1, keepdims=True))
    decay = jnp.exp(b_tot + m_prev - m_new)
    end_shift = b_tot - m_new
    m_ref[...] = jnp.broadcast_to(m_new, m_ref.shape)
    ones_blk = jnp.ones((L, LANES), BF16)

    for h in range(heads):
        sl = slice(h * dh, (h + 1) * dh)
        q = q_ref[:, sl] * scale
        k = k_ref[:, sl]
        v = v_ref[:, sl]
        ig_r = jnp.broadcast_to(g[:, h:h + 1], (L, LANES))
        b_r = jnp.broadcast_to(b_nat[:, h:h + 1] + b_nat[:, SUBLANES + h:SUBLANES + h + 1],
                               (L, LANES))
        a_h = a_row[h:h + 1, :]
        mp = m_prev[h:h + 1, :]

        s = lax.dot_general(q, k, _NT, preferred_element_type=F32)
        cm = jnp.max(jnp.where(causal, a_h, -jnp.inf), axis=-1, keepdims=True)
        mm_r = jnp.maximum(jnp.broadcast_to(cm, (L, LANES)), mp)
        w = s * jnp.exp(jnp.where(causal, a_h - rep(mm_r, L), -jnp.inf))
        einter_r = jnp.exp(mp - mm_r)
        floor_r = jnp.exp(-(b_r + mm_r))
        eend_r = jnp.exp(end_shift[h:h + 1, :] + ig_r - b_r)

        state = s_ref[h]
        intra = jnp.dot(w.astype(BF16), jnp.concatenate([v, ones_blk], axis=1),
                        preferred_element_type=F32)
        inter = jnp.dot(q, state.astype(BF16), preferred_element_type=F32)
        tot = intra + rep(einter_r, dh + LANES) * inter
        num = tot[:, :dh]
        den_r = tot[:, dh:]
        inv_r = 1.0 / jnp.maximum(jnp.abs(den_r), floor_r)
        ssq_r = jnp.broadcast_to(jnp.sum(num * num, axis=-1, keepdims=True), (L, LANES))
        rs_r = inv_r * lax.rsqrt(ssq_r * (inv_r * inv_r) * (1.0 / dh) + EPS)
        half_g = 0.5 * hg_ref[h]
        gain = half_g + half_g * jnp.tanh(0.5 * o_ref[:, sl].astype(F32))
        out_ref[:, sl] = ((num * rep(rs_r, dh)) * gain).astype(out_ref.dtype)

        ev = jnp.concatenate(
            [(rep(eend_r, dh) * v.astype(F32)).astype(BF16), eend_r.astype(BF16)], axis=1)
        loc = lax.dot_general(k, ev, (((0,), (0,)), ((), ())), preferred_element_type=F32)
        s_ref[h] = decay[h:h + 1, :] * state + loc


def _lru_init(xtail_ref, hc_ref, perm_ref, unperm_ref, *, ts):
    seg = ts // SUBLANES

    @pl.when(pl.program_id(1) == 0)
    def _():
        xtail_ref[...] = jnp.zeros_like(xtail_ref)
        hc_ref[...] = jnp.zeros_like(hc_ref)

    @pl.when((pl.program_id(0) == 0) & (pl.program_id(1) == 0))
    def _():
        ri = lax.broadcasted_iota(jnp.int32, (ts, ts), 0)
        ci = lax.broadcasted_iota(jnp.int32, (ts, ts), 1)
        t_of_row = (ri & (SUBLANES - 1)) * seg + (ri >> SUBLANE_BITS)
        t_of_col = (ci & (SUBLANES - 1)) * seg + (ci >> SUBLANE_BITS)
        perm_ref[...] = jnp.where(ci == t_of_row, 1.0, 0.0).astype(BF16)
        unperm_ref[...] = jnp.where(ri == t_of_col, 1.0, 0.0).astype(BF16)


def _lru_main(xr_ref, gr_ref, cw_ref, cb_ref, wax_ref, ba_ref, bx_ref, lam_ref, out_ref,
              xtail_ref, hc_ref, perm_ref, unperm_ref, a_scr, u_scr, c_scr, h_scr,
              *, ts, blocks, bd):
    C = blocks * bd
    seg = ts // SUBLANES
    halo = LRU_CONV - 1

    xb = xr_ref[...]
    xp = jnp.dot(perm_ref[...], xb, preferred_element_type=F32)
    gp = jnp.dot(perm_ref[...], gr_ref[...], preferred_element_type=F32)
    gelu_g = _gelu_tanh(gp)
    prev = xtail_ref[...]
    xtail_ref[...] = xb[ts - SUBLANES:ts, :].astype(F32)
    row8 = lax.broadcasted_iota(jnp.int32, (SUBLANES, C), 0)

    wrapped = []
    for k in range(1, halo + 1):
        r0 = (seg - k) * SUBLANES
        up = pltpu.roll(xp[r0:r0 + SUBLANES, :], 1, axis=0)
        wrapped.append(jnp.where(row8 == 0, prev[SUBLANES - k:SUBLANES - k + 1, :], up))

    cw = cw_ref[...]
    xc = cb_ref[...] + cw[LRU_CONV - 1:LRU_CONV, :] * xp
    for j in range(1, LRU_CONV):
        xs = jnp.concatenate([wrapped[k - 1] for k in range(j, 0, -1)]
                             + [xp[0:ts - j * SUBLANES, :]], axis=0)
        xc = xc + cw[LRU_CONV - 1 - j:LRU_CONV - j, :] * xs

    xcb = xc.astype(BF16)
    half_xc = 0.5 * xc
    k = (-0.5 * LRU_C * math.log2(math.e)) * _softplus(-lam_ref[...])
    half_ba = 0.5 * ba_ref[...]
    half_bx = 0.5 * bx_ref[...]
    for n in range(blocks):
        sl = slice(n * bd, (n + 1) * bd)
        ax = jnp.dot(xcb[:, sl], wax_ref[n], preferred_element_type=F32)
        t_r = jnp.tanh(ax[:, :bd] + half_ba[:, sl])
        t_i = jnp.tanh(ax[:, bd:] + half_bx[:, sl])
        a = jnp.exp2(k[:, sl] + k[:, sl] * t_r)
        ixc = half_xc[:, sl] + half_xc[:, sl] * t_i
        y = 1.0 - a * a
        a_scr[:, sl] = a
        u_scr[:, sl] = jnp.where(y > 0.0, y * lax.rsqrt(y), 0.0) * ixc

    def body(v, carry):
        h, c = carry
        rows = pl.ds(pl.multiple_of(v * SUBLANES, SUBLANES), SUBLANES)
        av = a_scr[rows, :]
        h = av * h + u_scr[rows, :]
        c = av * c
        h_scr[rows, :] = h
        c_scr[rows, :] = c
        return h, c

    init = (jnp.zeros((SUBLANES, C), F32), jnp.ones((SUBLANES, C), F32))
    h_end, c_end = lax.fori_loop(0, seg, body, init, unroll=4)

    state = hc_ref[0:1, :]
    entering = []
    for s in range(SUBLANES):
        entering.append(state)
        state = c_end[s:s + 1, :] * state + h_end[s:s + 1, :]
    hc_ref[0:1, :] = state
    entering = jnp.concatenate(entering, axis=0)

    h = (h_scr[...].reshape(seg, SUBLANES, C)
         + c_scr[...].reshape(seg, SUBLANES, C) * entering[None]).reshape(ts, C)
    outp = (h * gelu_g).astype(BF16)
    out_ref[...] = jnp.dot(unperm_ref[...], outp,
                           preferred_element_type=F32).astype(out_ref.dtype)


def _mixers_kernel(proj_ref, gate_ref, hg_ref,
                   cw_ref, cb_ref, wax_ref, ba_ref, bx_ref, lam_ref,
                   wup_ref, wdn_ref, wout_ref,
                   mix_ref, wup_bf_ref, wdn_bf_ref, wout_bf_ref,
                   s_ref, m_ref, tri_ref,
                   xtail_ref, hc_ref, perm_ref, unperm_ref, a_scr, u_scr, c_scr, h_scr,
                   *, heads, dh, rows, blocks, bd):
    W = heads * dh
    q_ref, k_ref, v_ref, o_ref, xr_ref, gr_ref = (
        proj_ref.at[:, j * W:(j + 1) * W] for j in range(6))
    step = pl.program_id(0) * pl.num_programs(1) + pl.program_id(1)
    mine = pl.ds(pl.multiple_of((step % 2) * rows, rows), rows)
    hm_ref = mix_ref.at[mine, 0:W]
    hr_ref = mix_ref.at[mine, W:2 * W]

    _mlstm_init(s_ref, m_ref, tri_ref, chunk=rows)
    _lru_init(xtail_ref, hc_ref, perm_ref, unperm_ref, ts=rows)

    wup_bf_ref[...] = wup_ref[...].astype(BF16)
    wdn_bf_ref[...] = wdn_ref[...].astype(BF16)
    wout_bf_ref[...] = wout_ref[...].astype(BF16)

    _mlstm_main(q_ref, k_ref, v_ref, o_ref, gate_ref, hg_ref, hm_ref,
                s_ref, m_ref, tri_ref, heads=heads, dh=dh, chunk=rows)
    _lru_main(xr_ref, gr_ref, cw_ref, cb_ref, wax_ref, ba_ref, bx_ref, lam_ref, hr_ref,
              xtail_ref, hc_ref, perm_ref, unperm_ref, a_scr, u_scr, c_scr, h_scr,
              ts=rows, blocks=blocks, bd=bd)


def _mixers(proj, gates, head_g, conv_w, conv_b, w_ax, ba, bx, lam, w_up, w_down, w_out,
            *, batch, seq, heads, dh, rows):
    T = batch * seq
    W = heads * dh
    blocks, bd, _ = w_ax.shape
    C = blocks * bd
    assert W == C and bd == LANES and rows % (SUBLANES * SUBLANES) == 0
    ns = seq // rows
    steps = batch * ns
    row_blk = lambda b, s: b * ns + s

    def slab(w):
        n = w.shape[0] // steps
        assert n * steps == w.shape[0] and n % BF16_SUBLANES == 0
        return pl.BlockSpec((n, w.shape[1]), lambda b, s: (row_blk(b, s), 0))

    rows_of = lambda width: pl.BlockSpec((rows, width), lambda b, s: (row_blk(b, s), 0))
    vec = lambda n: pl.BlockSpec((n, C), lambda b, s: (0, 0))
    kern = functools.partial(_mixers_kernel, heads=heads, dh=dh, rows=rows, blocks=blocks, bd=bd)
    return pl.pallas_call(
        kern,
        grid=(batch, ns),
        in_specs=[
            rows_of(proj.shape[1]),
            rows_of(LANES),
            pl.BlockSpec((heads, 1, dh), lambda b, s: (0, 0, 0)),
            vec(LRU_CONV), vec(1),
            pl.BlockSpec((blocks, bd, 2 * bd), lambda b, s: (0, 0, 0)),
            vec(1), vec(1), vec(1),
            slab(w_up), slab(w_down), slab(w_out),
        ],
        out_specs=[pl.BlockSpec((2 * rows, W + C), lambda b, s: (row_blk(b, s) // 2, 0)),
                   slab(w_up), slab(w_down), slab(w_out)],
        out_shape=[jax.ShapeDtypeStruct((T, W + C), BF16),
                   jax.ShapeDtypeStruct(w_up.shape, BF16),
                   jax.ShapeDtypeStruct(w_down.shape, BF16),
                   jax.ShapeDtypeStruct(w_out.shape, BF16)],
        scratch_shapes=[
            pltpu.VMEM((heads, dh, dh + LANES), F32),
            pltpu.VMEM((SUBLANES, LANES), F32),
            pltpu.VMEM((rows, rows), BF16),
            pltpu.VMEM((SUBLANES, C), F32),
            pltpu.VMEM((SUBLANES, C), F32),
            pltpu.VMEM((rows, rows), BF16),
            pltpu.VMEM((rows, rows), BF16),
        ] + [pltpu.VMEM((rows, C), F32)] * 4,
        compiler_params=pltpu.CompilerParams(
            dimension_semantics=("arbitrary", "arbitrary"),
            vmem_limit_bytes=VMEM_LIMIT),
        name="mixers",
    )(proj, gates, head_g, conv_w, conv_b, w_ax, ba, bx, lam, w_up, w_down, w_out)


def _outproj_kernel(x_ref, mix_ref, w_ref, g_ref, x1_ref, n2_ref):
    tm = x_ref.shape[0]
    x1 = x_ref[...] + jnp.dot(mix_ref[...], w_ref[...], preferred_element_type=F32)
    x1_ref[...] = x1
    half = pl.ds(pl.multiple_of((pl.program_id(0) % 2) * tm, tm), tm)
    n2_ref[half, :] = _rmsnorm(x1, g_ref[...]).astype(n2_ref.dtype)


def _outproj(x, mix, w_out, g, *, tm):
    T, D = x.shape
    Dm = mix.shape[1]
    return pl.pallas_call(
        _outproj_kernel,
        grid=(T // tm,),
        in_specs=[
            pl.BlockSpec((tm, D), lambda m: (m, 0)),
            pl.BlockSpec((tm, Dm), lambda m: (m, 0)),
            pl.BlockSpec((Dm, D), lambda m: (0, 0)),
            pl.BlockSpec((1, D), lambda m: (0, 0)),
        ],
        out_specs=[
            pl.BlockSpec((tm, D), lambda m: (m, 0)),
            pl.BlockSpec((2 * tm, D), lambda m: (m // 2, 0)),
        ],
        out_shape=[
            jax.ShapeDtypeStruct((T, D), F32),
            jax.ShapeDtypeStruct((T, D), BF16),
        ],
        compiler_params=pltpu.CompilerParams(
            dimension_semantics=("arbitrary",),
            vmem_limit_bytes=VMEM_LIMIT),
        name="outproj",
    )(x, mix, w_out, g)


def _ffn_kernel(n2_ref, x1_ref, wg_ref, wu_ref, cw_ref, cb_ref, wd_ref, gf_ref, y_ref,
                tail_ref, *, tm, tiles_per_seq, final_norm):
    m = pl.program_id(0)
    f = pl.program_id(1)
    nf = pl.num_programs(1)

    def step(first):
        n2 = n2_ref[...]
        gate = jnp.dot(n2, wg_ref[...], preferred_element_type=F32)
        up = jnp.dot(n2, wu_ref[...], preferred_element_type=F32)

        prev = jnp.where(m % tiles_per_seq == 0, 0.0, tail_ref[f])
        tail_ref[f] = gate[tm - SUBLANES:tm, :]

        row8 = lax.broadcasted_iota(jnp.int32, prev.shape, 0)
        cw = cw_ref[...]
        conv = cb_ref[...] + cw[FFN_CONV - 1:FFN_CONV, :] * gate
        for j in range(1, FFN_CONV):
            r = pltpu.roll(gate, j, axis=0)
            top = jnp.where(row8 < j, pltpu.roll(prev, j, axis=0), r[0:SUBLANES, :])
            gs = jnp.concatenate([top, r[SUBLANES:, :]], axis=0)
            conv = conv + cw[FFN_CONV - 1 - j:FFN_CONV - j, :] * gs
        hid_b = ((conv * _sigmoid(conv)) * up).astype(BF16)
        tf = hid_b.shape[1]
        for c in range(y_ref.shape[1] // tf):
            cols = slice(c * tf, (c + 1) * tf)
            acc = x1_ref[:, cols] if first else y_ref[:, cols]
            y_ref[:, cols] = acc + jnp.dot(hid_b, wd_ref[:, cols], preferred_element_type=F32)

    @pl.when(f == 0)
    def _():
        step(True)

    @pl.when(f > 0)
    def _():
        step(False)

    if final_norm:
        @pl.when(f == nf - 1)
        def _():
            y_ref[...] = _rmsnorm(y_ref[...], gf_ref[...])


def _ffn(n2, x1, w_up, conv_w, conv_b, w_down, gf, *, seq, tm, tf, final_norm):
    T, D = n2.shape
    F = w_down.shape[0]
    nf = F // tf
    kern = functools.partial(_ffn_kernel, tm=tm, tiles_per_seq=seq // tm, final_norm=final_norm)
    return pl.pallas_call(
        kern,
        grid=(T // tm, nf),
        in_specs=[
            pl.BlockSpec((tm, D), lambda m, f: (m, 0)),
            pl.BlockSpec((tm, D), lambda m, f: (m, 0)),
            pl.BlockSpec((D, tf), lambda m, f: (0, f)),
            pl.BlockSpec((D, tf), lambda m, f: (0, nf + f)),
            pl.BlockSpec((FFN_CONV, tf), lambda m, f: (0, f)),
            pl.BlockSpec((1, tf), lambda m, f: (0, f)),
            pl.BlockSpec((tf, D), lambda m, f: (f, 0)),
            pl.BlockSpec((1, D), lambda m, f: (0, 0)),
        ],
        out_specs=pl.BlockSpec((tm, D), lambda m, f: (m, 0)),
        out_shape=jax.ShapeDtypeStruct((T, D), F32),
        scratch_shapes=[pltpu.VMEM((nf, SUBLANES, tf), F32)],
        compiler_params=pltpu.CompilerParams(
            dimension_semantics=("arbitrary", "arbitrary"),
            vmem_limit_bytes=VMEM_LIMIT_FFN),
        name="convffn",
    )(n2, x1, w_up, w_up, conv_w, conv_b, w_down, gf)


def kernel(x, norm_mix_g, w_in, b_gate_m, mlstm_norm_g, lru_conv_w, lru_conv_b, lru_wa, lru_ba,
           lru_wx, lru_bx, lru_lambda, w_out, norm_ffn_g, w_up, ffn_conv_w, ffn_conv_b, w_down,
           norm_final_g):
    B, S, D = x.shape
    T = B * S
    depth = w_in.shape[0]
    H = MLSTM_HEADS
    dh = mlstm_norm_g.shape[-1]
    d_m = H * dh
    d_r = lru_conv_w.shape[-1]
    assert d_m == d_r, "column-block addressing assumes equal head-group widths"
    n_gate = 2 * H
    gate_lo = 4 * d_m

    xf = x.reshape(T, D)
    for l in range(depth):
        b_if = jnp.pad(b_gate_m[l].astype(F32), (0, LANES - n_gate)).reshape(1, LANES)
        proj, gates = _inproj(xf, norm_mix_g[l].reshape(1, D), jnp.swapaxes(w_in[l], 0, 1), b_if,
                              gate_lo=gate_lo, n_gate=n_gate, tm=1024, tn=1024, group=2)

        w_ax = (0.5 * jnp.concatenate([lru_wa[l], lru_wx[l]], axis=-1)).astype(BF16)
        mix, w_up_bf, w_down_bf, w_out_bf = _mixers(
            proj, gates, mlstm_norm_g[l].reshape(H, 1, dh),
            lru_conv_w[l], lru_conv_b[l].reshape(1, d_r), w_ax,
            lru_ba[l].reshape(1, d_r), lru_bx[l].reshape(1, d_r), lru_lambda[l].reshape(1, d_r),
            w_up[l], w_down[l], w_out[l], batch=B, seq=S, heads=H, dh=dh, rows=MIXER_ROWS)

        x1, n2 = _outproj(xf, mix, w_out_bf, norm_ffn_g[l].reshape(1, D), tm=512)

        xf = _ffn(n2, x1, w_up_bf, ffn_conv_w[l], ffn_conv_b[l].reshape(1, -1),
                  w_down_bf, norm_final_g.reshape(1, D), seq=S, tm=1024, tf=512,
                  final_norm=(l == depth - 1))
    return xf.reshape(B, S, D)
```
